```python
import jax, jax.numpy as jnp
from jax import lax
import numpy as np

D_MODEL = 2048
BATCH = 4
SEQ = 2048
DEPTH = 4
DEC_BATCH = 128
DEC_SEQ = 1
PAST_LEN = 16384
PAGE_SIZE = 128

N_MIXERS = 3
N_LAYERS_A = (DEPTH + 2) // 3
N_LAYERS_B = (DEPTH + 1) // 3
N_LAYERS_C = DEPTH // 3
D_FF = 5632
FFN_HALF = 0.5
D_A = D_MODEL
CONV_A_WIDTH = 31
D_SGU = 2 * D_MODEL
CHUNK = 128
N_SGU_GROUPS = 8
D_SGU_GROUP = D_SGU // N_SGU_GROUPS
D_C = D_MODEL
CONV_C_WIDTH = 3
EPS = 1e-6

kernel_name = "hybrid_conformerconv_chunkgmlp_shortconv_step"


def rmsnorm(x, g):
    xf = x.astype(jnp.float32)
    ms = jnp.mean(xf * xf, axis=-1, keepdims=True)
    return (xf * lax.rsqrt(ms + EPS)).astype(x.dtype) * g


def layernorm(x, g, b):
    xf = x.astype(jnp.float32)
    mu = jnp.mean(xf, axis=-1, keepdims=True)
    var = jnp.mean(jnp.square(xf - mu), axis=-1, keepdims=True)
    return ((xf - mu) * lax.rsqrt(var + EPS)).astype(x.dtype) * g + b


def swiglu(h, w_gate, w_up, w_down):
    return (jax.nn.silu(h @ w_gate) * (h @ w_up)) @ w_down


def causal_dwconv(xp, w, t):
    k_w = w.shape[0]
    y = xp[:, 0:t] * w[0]
    for k in range(1, k_w):
        y = y + xp[:, k:k + t] * w[k]
    return y


def conformer_conv(h, buf, w_pw1, b_pw1, w_dw, b_dw, ln_g, ln_b, w_pw2):
    t = h.shape[1]
    a, gate = jnp.split(h @ w_pw1 + b_pw1, 2, axis=-1)
    glu = a * jax.nn.sigmoid(gate)
    xp = jnp.concatenate([buf.astype(glu.dtype), glu], axis=1)
    y = causal_dwconv(xp, w_dw, t) + b_dw
    y = jax.nn.silu(layernorm(y, ln_g, ln_b))
    return y @ w_pw2, xp[:, -(CONV_A_WIDTH - 1):]


def chunked_sgu(h, w_in, b_in, ln_g, ln_b, w_s, b_s, w_out):
    bsz, t, _ = h.shape
    z = jax.nn.gelu(h @ w_in + b_in, approximate=False)
    u, v = jnp.split(z, 2, axis=-1)
    v = layernorm(v, ln_g, ln_b)
    rows = min(t, CHUNK)
    n_chunks = t // rows
    mask = jnp.tril(jnp.ones((rows, rows), dtype=bool))
    ws = jnp.where(mask[None], w_s[:, :rows, :rows], jnp.zeros((), w_s.dtype))
    vg = v.reshape(bsz, n_chunks, rows, N_SGU_GROUPS, D_SGU_GROUP)
    s = jnp.einsum('gts,bnsgd->bntgd', ws, vg) + jnp.transpose(b_s[:, :rows])[None, None, :, :, None]
    y = u * s.reshape(bsz, t, D_SGU)
    return y @ w_out, v[:, t - rows:]


def short_gated_conv(h, buf, w_in, w_conv, w_out):
    t = h.shape[1]
    b_gate, c_gate, xin = jnp.split(h @ w_in, 3, axis=-1)
    cx = c_gate * xin
    xp = jnp.concatenate([buf.astype(cx.dtype), cx], axis=1)
    y = b_gate * causal_dwconv(xp, w_conv, t)
    return y @ w_out, xp[:, -(CONV_C_WIDTH - 1):]


def setup_inputs(seed: int = 0) -> dict:
    key = jax.random.key(seed)
    ks = jax.random.split(key, 32)
    nrm = lambda k, shape, scale: jax.random.normal(k, shape, jnp.float32) * scale
    gain = lambda k, shape: 1.0 + nrm(k, shape, 0.01)
    return {
        "x_prompt": nrm(ks[0], (BATCH, SEQ, D_MODEL), 1.0),
        "x_sample": nrm(ks[1], (DEC_BATCH, DEC_SEQ, D_MODEL), 1.0),
        "state_conv_a": nrm(ks[2], (N_LAYERS_A, DEC_BATCH, CONV_A_WIDTH - 1, D_A), 0.5),
        "state_conv_c": nrm(ks[3], (N_LAYERS_C, DEC_BATCH, CONV_C_WIDTH - 1, D_C), 0.5),
        "g_ffn1": gain(ks[4], (DEPTH, D_MODEL)),
        "g_mix": gain(ks[5], (DEPTH, D_MODEL)),
        "g_ffn2": gain(ks[6], (DEPTH, D_MODEL)),
        "g_final": gain(ks[7], (D_MODEL,)),
        "w_ffn_gate": nrm(ks[8], (DEPTH, 2, D_MODEL, D_FF), D_MODEL ** -0.5),
        "w_ffn_up": nrm(ks[9], (DEPTH, 2, D_MODEL, D_FF), D_MODEL ** -0.5),
        "w_ffn_down": nrm(ks[10], (DEPTH, 2, D_FF, D_MODEL), D_FF ** -0.5),
        "a_w_pw1": nrm(ks[11], (N_LAYERS_A, D_MODEL, 2 * D_A), D_MODEL ** -0.5),
        "a_b_pw1": nrm(ks[12], (N_LAYERS_A, 2 * D_A), 0.02),
        "a_w_dw": nrm(ks[13], (N_LAYERS_A, CONV_A_WIDTH, D_A), CONV_A_WIDTH ** -0.5),
        "a_b_dw": nrm(ks[14], (N_LAYERS_A, D_A), 0.02),
        "a_ln_g": gain(ks[15], (N_LAYERS_A, D_A)),
        "a_ln_b": nrm(ks[16], (N_LAYERS_A, D_A), 0.02),
        "a_w_pw2": nrm(ks[17], (N_LAYERS_A, D_A, D_MODEL), D_A ** -0.5),
        "b_w_in": nrm(ks[18], (N_LAYERS_B, D_MODEL, 2 * D_SGU), D_MODEL ** -0.5),
        "b_b_in": nrm(ks[19], (N_LAYERS_B, 2 * D_SGU), 0.02),
        "b_ln_g": gain(ks[20], (N_LAYERS_B, D_SGU)),
        "b_ln_b": nrm(ks[21], (N_LAYERS_B, D_SGU), 0.02),
        "b_w_s": nrm(ks[22], (N_LAYERS_B, N_SGU_GROUPS, CHUNK, CHUNK), CHUNK ** -0.5),
        "b_b_s": 1.0 + nrm(ks[23], (N_LAYERS_B, N_SGU_GROUPS, CHUNK), 0.02),
        "b_w_out": nrm(ks[24], (N_LAYERS_B, D_SGU, D_MODEL), D_SGU ** -0.5),
        "c_w_in": nrm(ks[25], (N_LAYERS_C, D_MODEL, 3 * D_C), D_MODEL ** -0.5),
        "c_w_conv": nrm(ks[26], (N_LAYERS_C, CONV_C_WIDTH, D_C), CONV_C_WIDTH ** -0.5),
        "c_w_out": nrm(ks[27], (N_LAYERS_C, D_C, D_MODEL), D_C ** -0.5),
    }


def reference(x_prompt, x_sample, state_conv_a, state_conv_c,
              g_ffn1, g_mix, g_ffn2, g_final,
              w_ffn_gate, w_ffn_up, w_ffn_down,
              a_w_pw1, a_b_pw1, a_w_dw, a_b_dw, a_ln_g, a_ln_b, a_w_pw2,
              b_w_in, b_b_in, b_ln_g, b_ln_b, b_w_s, b_b_s, b_w_out,
              c_w_in, c_w_conv, c_w_out):

    def run(x, buf_a, buf_c):
        new_a, new_b, new_c = [], [], []
        for i in range(DEPTH):
            x = x + FFN_HALF * swiglu(rmsnorm(x, g_ffn1[i]), w_ffn_gate[i, 0], w_ffn_up[i, 0], w_ffn_down[i, 0])
            h = rmsnorm(x, g_mix[i])
            kind, j = i % N_MIXERS, i // N_MIXERS
            if kind == 0:
                m, s = conformer_conv(h, buf_a[j], a_w_pw1[j], a_b_pw1[j], a_w_dw[j], a_b_dw[j],
                                      a_ln_g[j], a_ln_b[j], a_w_pw2[j])
                new_a.append(s)
            elif kind == 1:
                m, s = chunked_sgu(h, b_w_in[j], b_b_in[j], b_ln_g[j], b_ln_b[j], b_w_s[j], b_b_s[j], b_w_out[j])
                new_b.append(s)
            else:
                m, s = short_gated_conv(h, buf_c[j], c_w_in[j], c_w_conv[j], c_w_out[j])
                new_c.append(s)
            x = x + m
            x = x + FFN_HALF * swiglu(rmsnorm(x, g_ffn2[i]), w_ffn_gate[i, 1], w_ffn_up[i, 1], w_ffn_down[i, 1])
        return rmsnorm(x, g_final), jnp.stack(new_a), jnp.stack(new_b), jnp.stack(new_c)

    zeros_a = jnp.zeros((N_LAYERS_A, BATCH, CONV_A_WIDTH - 1, D_A), x_prompt.dtype)
    zeros_c = jnp.zeros((N_LAYERS_C, BATCH, CONV_C_WIDTH - 1, D_C), x_prompt.dtype)
    y_prompt, conv_a_p, sgu_v_p, conv_c_p = run(x_prompt, zeros_a, zeros_c)
    y_sample, conv_a_s, sgu_v_s, conv_c_s = run(x_sample, state_conv_a, state_conv_c)
    return (y_prompt, y_sample, conv_a_p, conv_a_s, sgu_v_p, sgu_v_s, conv_c_p, conv_c_s)
```

```python
import functools

import jax
import jax.numpy as jnp
from jax import lax
from jax.experimental import pallas as pl
from jax.experimental.pallas import tpu as pltpu

F32 = jnp.float32
BF16 = jnp.bfloat16

EPS = 1e-6
FFN_HALF = 0.5
CHUNK = 128
N_SGU_GROUPS = 8
INV_SQRT2 = 0.7071067811865476

V7X_VMEM_BYTES = 64 * 1024 * 1024
VMEM_LIMIT = V7X_VMEM_BYTES - 6 * 1024 * 1024


def _params(sem):
    return pltpu.CompilerParams(dimension_semantics=sem, vmem_limit_bytes=VMEM_LIMIT)


def _rms(x, g):
    ms = jnp.mean(x * x, axis=-1, keepdims=True)
    return (x * lax.rsqrt(ms + EPS)) * g


def _layernorm(y, g, b):
    mu = jnp.mean(y, axis=-1, keepdims=True)
    yc = y - mu
    var = jnp.mean(yc * yc, axis=-1, keepdims=True)
    return (yc * lax.rsqrt(var + EPS)) * g + b


def _silu(x):
    return x * jax.nn.sigmoid(x)


def _gelu(x):
    return 0.5 * x * (1.0 + lax.erf(x * INV_SQRT2))


def _dot(a, w):
    return jnp.dot(a, w, preferred_element_type=F32)


def _rms_rows(x_ref, g_ref, h_ref, copy_ref=None, rows_per_step=64):
    rows = x_ref.shape[0]
    step = rows_per_step
    assert rows % step == 0
    g = g_ref[...]

    def body(c, carry):
        r0 = pl.multiple_of(c * step, step)
        x = x_ref[pl.ds(r0, step), :]
        h_ref[pl.ds(r0, step), :] = _rms(x, g).astype(BF16)
        if copy_ref is not None:
            copy_ref[pl.ds(r0, step), :] = x
        return carry

    lax.fori_loop(0, rows // step, body, 0)


def _ffn_kernel(x_ref, g_ref, gf_ref, wg_ref, wu_ref, wd_ref, o_ref, h_ref, *, final_norm):
    f = pl.program_id(1)

    @pl.when(f == 0)
    def _():
        _rms_rows(x_ref, g_ref, h_ref, copy_ref=o_ref)

    h = h_ref[...]
    gate = _dot(h, wg_ref[...].astype(BF16))
    up = _dot(h, wu_ref[...].astype(BF16))
    act = (_silu(gate) * (up * FFN_HALF)).astype(BF16)
    o_ref[...] += _dot(act, wd_ref[...].astype(BF16))

    if final_norm:
        @pl.when(f == pl.num_programs(1) - 1)
        def _():
            rows, step = o_ref.shape[0], 64
            gf = gf_ref[...]

            def body(c, carry):
                r0 = pl.multiple_of(c * step, step)
                o_ref[pl.ds(r0, step), :] = _rms(o_ref[pl.ds(r0, step), :], gf)
                return carry

            lax.fori_loop(0, rows // step, body, 0)


def _ffn(x, g3, gf2, w_gate, w_up, w_down, layer, which, *, final_norm, tm, tf):
    m, d = x.shape
    dff = w_gate.shape[-1]
    grid = (m // tm, dff // tf)
    return pl.pallas_call(
        functools.partial(_ffn_kernel, final_norm=final_norm),
        grid=grid,
        in_specs=[
            pl.BlockSpec((tm, d), lambda i, f: (i, 0)),
            pl.BlockSpec((None, 1, d), lambda i, f: (layer, 0, 0)),
            pl.BlockSpec((1, d), lambda i, f: (0, 0)),
            pl.BlockSpec((None, None, d, tf), lambda i, f: (layer, which, 0, f)),
            pl.BlockSpec((None, None, d, tf), lambda i, f: (layer, which, 0, f)),
            pl.BlockSpec((None, None, tf, d), lambda i, f: (layer, which, f, 0)),
        ],
        out_specs=pl.BlockSpec((tm, d), lambda i, f: (i, 0)),
        out_shape=jax.ShapeDtypeStruct((m, d), F32),
        scratch_shapes=[pltpu.VMEM((tm, d), BF16)],
        compiler_params=_params(("parallel", "arbitrary")),
        name=f"ffn_{layer}_{which}",
    )(x, g3, gf2, w_gate, w_up, w_down)


def _a1_kernel(x_ref, g_ref, wa_ref, wg_ref, ba_ref, bg_ref, o_ref, h_ref):
    @pl.when(pl.program_id(1) == 0)
    def _():
        _rms_rows(x_ref, g_ref, h_ref)

    h = h_ref[...]
    a = _dot(h, wa_ref[...].astype(BF16)) + ba_ref[...]
    gt = _dot(h, wg_ref[...].astype(BF16)) + bg_ref[...]
    o_ref[...] = a * jax.nn.sigmoid(gt)


def _a1(x, g3, w1, b1, layer, j, *, tm, tn):
    m, d = x.shape
    d_a = w1.shape[-1] // 2
    nb = d_a // tn
    return pl.pallas_call(
        _a1_kernel,
        grid=(m // tm, nb),
        in_specs=[
            pl.BlockSpec((tm, d), lambda i, n: (i, 0)),
            pl.BlockSpec((None, 1, d), lambda i, n: (layer, 0, 0)),
            pl.BlockSpec((None, d, tn), lambda i, n: (j, 0, n)),
            pl.BlockSpec((None, d, tn), lambda i, n: (j, 0, n + nb)),
            pl.BlockSpec((None, 1, tn), lambda i, n: (j, 0, n)),
            pl.BlockSpec((None, 1, tn), lambda i, n: (j, 0, n + nb)),
        ],
        out_specs=pl.BlockSpec((tm, tn), lambda i, n: (i, n)),
        out_shape=jax.ShapeDtypeStruct((m, d_a), F32),
        scratch_shapes=[pltpu.VMEM((tm, d), BF16)],
        compiler_params=_params(("parallel", "arbitrary")),
        name=f"mix_a1_{j}",
    )(x, g3, w1, w1, b1, b1)


def _fill_xp(xp_ref, halo_ref, cur_ref, first_in_seq):
    hr = halo_ref.shape[0]
    halo = halo_ref[...]
    xp_ref[0:hr, :] = jnp.where(first_in_seq, jnp.zeros_like(halo), halo)
    rows, step = cur_ref.shape[0], 64

    def body(c, carry):
        r0 = pl.multiple_of(c * step, step)
        xp_ref[pl.ds(hr + r0, step), :] = cur_ref[pl.ds(r0, step), :]
        return carry

    lax.fori_loop(0, rows // step, body, 0)


def _dwconv_block(xp_ref, w_ref, r0, rows, c0, lanes, taps, halo):
    base = halo - (taps - 1) + r0
    cols = pl.ds(c0, lanes)
    acc = xp_ref[pl.ds(base, rows), cols] * w_ref[0:1, cols]
    for k in range(1, taps):
        acc = acc + xp_ref[pl.ds(base + k, rows), cols] * w_ref[k:k + 1, cols]
    return acc


def _a2p_kernel(x_ref, cur_ref, halo_ref, w_ref, b_ref, lg_ref, lb_ref, w2_ref, o_ref,
                xp_ref, y_ref, act_ref, *, taps, tiles_per_seq, row_block, lanes):
    i = pl.program_id(0)
    tm = cur_ref.shape[0]
    d = cur_ref.shape[1]
    halo = halo_ref.shape[0]

    @pl.when(pl.program_id(1) == 0)
    def _():
        _fill_xp(xp_ref, halo_ref, cur_ref, i % tiles_per_seq == 0)
        bias, lg, lb = b_ref[...], lg_ref[...], lb_ref[...]
        for rb in range(tm // row_block):
            def col_body(c, carry, rb=rb):
                c0 = pl.multiple_of(c * lanes, lanes)
                y_ref[:, pl.ds(c0, lanes)] = _dwconv_block(
                    xp_ref, w_ref, rb * row_block, row_block, c0, lanes, taps, halo)
                return carry

            lax.fori_loop(0, d // lanes, col_body, 0)
            step = 32
            for s in range(row_block // step):
                y = y_ref[s * step:(s + 1) * step, :] + bias
                r0 = rb * row_block + s * step
                act_ref[r0:r0 + step, :] = _silu(_layernorm(y, lg, lb)).astype(BF16)

    o_ref[...] = x_ref[...] + _dot(act_ref[...], w2_ref[...].astype(BF16))


def _a2_prompt(x, glu, w_dw, b_dw, ln_g, ln_b, w2, j, *, n_prompt, seq, tm, tn, halo):
    m, d = x.shape
    taps = w_dw.shape[1]
    hb = tm // halo
    kern = functools.partial(_a2p_kernel, taps=taps, tiles_per_seq=seq // tm,
                             row_block=256, lanes=128)
    return pl.pallas_call(
        kern,
        grid=(n_prompt // tm, d // tn),
        in_specs=[
            pl.BlockSpec((tm, tn), lambda i, n: (i, n)),
            pl.BlockSpec((tm, d), lambda i, n: (i, 0)),
            pl.BlockSpec((halo, d), lambda i, n: (jnp.maximum(i * hb - 1, 0), 0)),
            pl.BlockSpec((None, taps, d), lambda i, n: (j, 0, 0)),
            pl.BlockSpec((None, 1, d), lambda i, n: (j, 0, 0)),
            pl.BlockSpec((None, 1, d), lambda i, n: (j, 0, 0)),
            pl.BlockSpec((None, 1, d), lambda i, n: (j, 0, 0)),
            pl.BlockSpec((None, d, tn), lambda i, n: (j, 0, n)),
        ],
        out_specs=pl.BlockSpec((tm, tn), lambda i, n: (i, n)),
        out_shape=jax.ShapeDtypeStruct((m, d), F32),
        scratch_shapes=[
            pltpu.VMEM((halo + tm, d), F32),
            pltpu.VMEM((256, d), F32),
            pltpu.VMEM((tm, d), BF16),
        ],
        input_output_aliases={0: 0},
        compiler_params=_params(("parallel", "arbitrary")),
        name=f"mix_a2p_{j}",
    )(x, glu, glu, w_dw, b_dw, ln_g, ln_b, w2)


def _a_sconv_kernel(st_ref, glu_ref, w_ref, b_ref, y_ref, ns_ref):
    tb, past, _ = st_ref.shape
    w = w_ref[...]
    bias = b_ref[...]
    for b in range(tb):
        st = st_ref[b]
        g = glu_ref[b:b + 1, :]
        y = jnp.sum(st * w[0:past], axis=0, keepdims=True) + g * w[past:past + 1] + bias
        y_ref[b:b + 1, :] = y
        ns_ref[b, 0:past - 1, :] = st_ref[b, 1:past, :]
        ns_ref[b, past - 1:past, :] = g


def _a_sample_conv(state, glu, w_dw, b_dw, j, *, n_prompt, tb):
    _, nb, past, d = state.shape
    taps = w_dw.shape[1]
    off = n_prompt // tb
    return pl.pallas_call(
        _a_sconv_kernel,
        grid=(nb // tb,),
        in_specs=[
            pl.BlockSpec((None, tb, past, d), lambda b: (j, b, 0, 0)),
            pl.BlockSpec((tb, d), lambda b: (off + b, 0)),
            pl.BlockSpec((None, taps, d), lambda b: (j, 0, 0)),
            pl.BlockSpec((None, 1, d), lambda b: (j, 0, 0)),
        ],
        out_specs=[
            pl.BlockSpec((tb, d), lambda b: (b, 0)),
            pl.BlockSpec((tb, past, d), lambda b: (b, 0, 0)),
        ],
        out_shape=[
            jax.ShapeDtypeStruct((nb, d), F32),
            jax.ShapeDtypeStruct((nb, past, d), F32),
        ],
        compiler_params=_params(("parallel",)),
        name=f"mix_a_sconv_{j}",
    )(state, glu, w_dw, b_dw)


def _a2s_kernel(x_ref, y_ref, lg_ref, lb_ref, w2_ref, o_ref, act_ref):
    @pl.when(pl.program_id(0) == 0)
    def _():
        act_ref[...] = _silu(_layernorm(y_ref[...], lg_ref[...], lb_ref[...])).astype(BF16)

    o_ref[...] = x_ref[...] + _dot(act_ref[...], w2_ref[...].astype(BF16))


def _a2_sample(x, y, ln_g, ln_b, w2, j, *, n_prompt, tn):
    m, d = x.shape
    ns = y.shape[0]
    rb = n_prompt // ns
    return pl.pallas_call(
        _a2s_kernel,
        grid=(d // tn,),
        in_specs=[
            pl.BlockSpec((ns, tn), lambda n: (rb, n)),
            pl.BlockSpec((ns, d), lambda n: (0, 0)),
            pl.BlockSpec((None, 1, d), lambda n: (j, 0, 0)),
            pl.BlockSpec((None, 1, d), lambda n: (j, 0, 0)),
            pl.BlockSpec((None, d, tn), lambda n: (j, 0, n)),
        ],
        out_specs=pl.BlockSpec((ns, tn), lambda n: (rb, n)),
        out_shape=jax.ShapeDtypeStruct((m, d), F32),
        scratch_shapes=[pltpu.VMEM((ns, d), BF16)],
        input_output_aliases={0: 0},
        compiler_params=_params(("arbitrary",)),
        name=f"mix_a2s_{j}",
    )(x, y, ln_g, ln_b, w2)


def _b1_kernel(x_ref, g_ref, w_ref, b_ref, lg_ref, lb_ref, o_ref, h_ref):
    n = pl.program_id(1)
    tn = w_ref.shape[1]

    @pl.when(n == 0)
    def _():
        _rms_rows(x_ref, g_ref, h_ref)

    z = _dot(h_ref[...], w_ref[...].astype(BF16)) + b_ref[...]
    o_ref[:, pl.ds(pl.multiple_of(n * tn, tn), tn)] = _gelu(z)

    @pl.when(n == pl.num_programs(1) - 1)
    def _():
        rows, step = o_ref.shape[0], 32
        lg, lb = lg_ref[...], lb_ref[...]

        def body(c, carry):
            r0 = pl.multiple_of(c * step, step)
            o_ref[pl.ds(r0, step), :] = _layernorm(o_ref[pl.ds(r0, step), :], lg, lb)
            return carry

        lax.fori_loop(0, rows // step, body, 0)


def _b1(x, g3, w_in, b_in, ln_g, ln_b, layer, j, *, tm, tn):
    m, d = x.shape
    d_sgu = w_in.shape[-1] // 2
    nb = d_sgu // tn
    return pl.pallas_call(
        _b1_kernel,
        grid=(m // tm, nb),
        in_specs=[
            pl.BlockSpec((tm, d), lambda i, n: (i, 0)),
            pl.BlockSpec((None, 1, d), lambda i, n: (layer, 0, 0)),
            pl.BlockSpec((None, d, tn), lambda i, n: (j, 0, n + nb)),
            pl.BlockSpec((None, 1, tn), lambda i, n: (j, 0, n + nb)),
            pl.BlockSpec((None, 1, d_sgu), lambda i, n: (j, 0, 0)),
            pl.BlockSpec((None, 1, d_sgu), lambda i, n: (j, 0, 0)),
        ],
        out_specs=pl.BlockSpec((tm, d_sgu), lambda i, n: (i, 0)),
        out_shape=jax.ShapeDtypeStruct((m, d_sgu), F32),
        scratch_shapes=[pltpu.VMEM((tm, d), BF16)],
        compiler_params=_params(("parallel", "arbitrary")),
        name=f"mix_b1_{j}",
    )(x, g3, w_in, b_in, ln_g, ln_b)


def _b2_kernel(x_ref, g_ref, v_ref, wi_ref, bi_ref, ws_ref, bs_ref, wo_ref, o_ref, h_ref,
               *, n_prompt_chunks):
    i = pl.program_id(0)
    tm = x_ref.shape[0]

    @pl.when(pl.program_id(1) == 0)
    def _():
        _rms_rows(x_ref, g_ref, h_ref, copy_ref=o_ref)

    u = _gelu(_dot(h_ref[...], wi_ref[...].astype(BF16)) + bi_ref[...])
    ws = ws_ref[...]
    row = lax.broadcasted_iota(jnp.int32, ws.shape, 0)
    col = lax.broadcasted_iota(jnp.int32, ws.shape, 1)
    w_tril = jnp.where(col <= row, ws, 0.0).astype(BF16)
    bs = bs_ref[...]
    parts = []
    for c in range(tm // CHUNK):
        rows = slice(c * CHUNK, (c + 1) * CHUNK)
        v = v_ref[rows, :]
        s_prompt = _dot(w_tril, v.astype(BF16)) + bs
        s_sample = ws[0:1, 0:1] * v + bs[0:1, :]
        is_sample = i * (tm // CHUNK) + c >= n_prompt_chunks
        parts.append(u[rows, :] * jnp.where(is_sample, s_sample, s_prompt))
    y = jnp.concatenate(parts, axis=0).astype(BF16)
    o_ref[...] += _dot(y, wo_ref[...].astype(BF16))


def _b2(x, g3, vn, w_in, b_in, w_s, b_s3, w_out, layer, j, *, n_prompt, tm):
    m, d = x.shape
    d_sgu = vn.shape[1]
    gw = d_sgu // N_SGU_GROUPS
    kern = functools.partial(_b2_kernel, n_prompt_chunks=n_prompt // CHUNK)
    return pl.pallas_call(
        kern,
        grid=(m // tm, N_SGU_GROUPS),
        in_specs=[
            pl.BlockSpec((tm, d), lambda i, g: (i, 0)),
            pl.BlockSpec((None, 1, d), lambda i, g: (layer, 0, 0)),
            pl.BlockSpec((tm, gw), lambda i, g: (i, g)),
            pl.BlockSpec((None, d, gw), lambda i, g: (j, 0, g)),
            pl.BlockSpec((None, 1, gw), lambda i, g: (j, 0, g)),
            pl.BlockSpec((None, None, CHUNK, CHUNK), lambda i, g: (j, g, 0, 0)),
            pl.BlockSpec((None, None, CHUNK, 1), lambda i, g: (j, g, 0, 0)),
            pl.BlockSpec((None, gw, d), lambda i, g: (j, g, 0)),
        ],
        out_specs=pl.BlockSpec((tm, d), lambda i, g: (i, 0)),
        out_shape=jax.ShapeDtypeStruct((m, d), F32),
        scratch_shapes=[pltpu.VMEM((tm, d), BF16)],
        compiler_params=_params(("parallel", "arbitrary")),
        name=f"mix_b2_{j}",
    )(x, g3, vn, w_in, b_in, w_s, b_s3, w_out)


def _c1_kernel(x_ref, g_ref, wb_ref, wc_ref, wx_ref, bg_ref, cx_ref, h_ref):
    @pl.when(pl.program_id(1) == 0)
    def _():
        _rms_rows(x_ref, g_ref, h_ref)

    h = h_ref[...]
    bg_ref[...] = _dot(h, wb_ref[...].astype(BF16))
    cx_ref[...] = _dot(h, wc_ref[...].astype(BF16)) * _dot(h, wx_ref[...].astype(BF16))


def _c1(x, g3, w_in, layer, j, *, tm, tn):
    m, d = x.shape
    d_c = w_in.shape[-1] // 3
    nb = d_c // tn
    out = jax.ShapeDtypeStruct((m, d_c), F32)
    return pl.pallas_call(
        _c1_kernel,
        grid=(m // tm, nb),
        in_specs=[
            pl.BlockSpec((tm, d), lambda i, n: (i, 0)),
            pl.BlockSpec((None, 1, d), lambda i, n: (layer, 0, 0)),
            pl.BlockSpec((None, d, tn), lambda i, n: (j, 0, n)),
            pl.BlockSpec((None, d, tn), lambda i, n: (j, 0, n + nb)),
            pl.BlockSpec((None, d, tn), lambda i, n: (j, 0, n + 2 * nb)),
        ],
        out_specs=[pl.BlockSpec((tm, tn), lambda i, n: (i, n))] * 2,
        out_shape=[out, out],
        scratch_shapes=[pltpu.VMEM((tm, d), BF16)],
        compiler_params=_params(("parallel", "arbitrary")),
        name=f"mix_c1_{j}",
    )(x, g3, w_in, w_in, w_in)


def _c2p_kernel(x_ref, bg_ref, cur_ref, halo_ref, w_ref, wo_ref, o_ref, xp_ref, act_ref,
                *, taps, tiles_per_seq, row_block, lanes):
    i = pl.program_id(0)
    tm, d = cur_ref.shape
    halo = halo_ref.shape[0]

    @pl.when(pl.program_id(1) == 0)
    def _():
        _fill_xp(xp_ref, halo_ref, cur_ref, i % tiles_per_seq == 0)
        for rb in range(tm // row_block):
            rows = pl.ds(rb * row_block, row_block)

            def col_body(c, carry, rb=rb, rows=rows):
                c0 = pl.multiple_of(c * lanes, lanes)
                cols = pl.ds(c0, lanes)
                y = _dwconv_block(xp_ref, w_ref, rb * row_block, row_block, c0, lanes, taps, halo)
                act_ref[rows, cols] = (bg_ref[rows, cols] * y).astype(BF16)
                return carry

            lax.fori_loop(0, d // lanes, col_body, 0)

    o_ref[...] = x_ref[...] + _dot(act_ref[...], wo_ref[...].astype(BF16))


def _c2_prompt(x, bg, cx, w_conv, w_out, j, *, n_prompt, seq, tm, tn, halo):
    m, d = x.shape
    taps = w_conv.shape[1]
    hb = tm // halo
    kern = functools.partial(_c2p_kernel, taps=taps, tiles_per_seq=seq // tm,
                             row_block=256, lanes=128)
    return pl.pallas_call(
        kern,
        grid=(n_prompt // tm, d // tn),
        in_specs=[
            pl.BlockSpec((tm, tn), lambda i, n: (i, n)),
            pl.BlockSpec((tm, d), lambda i, n: (i, 0)),
            pl.BlockSpec((tm, d), lambda i, n: (i, 0)),
            pl.BlockSpec((halo, d), lambda i, n: (jnp.maximum(i * hb - 1, 0), 0)),
            pl.BlockSpec((None, taps, d), lambda i, n: (j, 0, 0)),
            pl.BlockSpec((None, d, tn), lambda i, n: (j, 0, n)),
        ],
        out_specs=pl.BlockSpec((tm, tn), lambda i, n: (i, n)),
        out_shape=jax.ShapeDtypeStruct((m, d), F32),
        scratch_shapes=[
            pltpu.VMEM((halo + tm, d), F32),
            pltpu.VMEM((tm, d), BF16),
        ],
        input_output_aliases={0: 0},
        compiler_params=_params(("parallel", "arbitrary")),
        name=f"mix_c2p_{j}",
    )(x, bg, cx, cx, w_conv, w_out)


def _c2s_kernel(x_ref, bg_ref, cx_ref, s0_ref, s1_ref, w_ref, wo_ref, o_ref, act_ref):
    @pl.when(pl.program_id(0) == 0)
    def _():
        w = w_ref[...]
        y = s0_ref[...] * w[0:1] + s1_ref[...] * w[1:2] + cx_ref[...] * w[2:3]
        act_ref[...] = (bg_ref[...] * y).astype(BF16)

    o_ref[...] = x_ref[...] + _dot(act_ref[...], wo_ref[...].astype(BF16))


def _c2_sample(x, bg, cx, s0, s1, w_conv, w_out, j, *, n_prompt, tn):
    m, d = x.shape
    ns = s0.shape[0]
    rb = n_prompt // ns
    taps = w_conv.shape[1]
    return pl.pallas_call(
        _c2s_kernel,
        grid=(d // tn,),
        in_specs=[
            pl.BlockSpec((ns, tn), lambda n: (rb, n)),
            pl.BlockSpec((ns, d), lambda n: (rb, 0)),
            pl.BlockSpec((ns, d), lambda n: (rb, 0)),
            pl.BlockSpec((ns, d), lambda n: (0, 0)),
            pl.BlockSpec((ns, d), lambda n: (0, 0)),
            pl.BlockSpec((None, taps, d), lambda n: (j, 0, 0)),
            pl.BlockSpec((None, d, tn), lambda n: (j, 0, n)),
        ],
        out_specs=pl.BlockSpec((ns, tn), lambda n: (rb, n)),
        out_shape=jax.ShapeDtypeStruct((m, d), F32),
        scratch_shapes=[pltpu.VMEM((ns, d), BF16)],
        input_output_aliases={0: 0},
        compiler_params=_params(("arbitrary",)),
        name=f"mix_c2s_{j}",
    )(x, bg, cx, s0, s1, w_conv, w_out)


def kernel(x_prompt, x_sample, state_conv_a, state_conv_c, g_ffn1, g_mix, g_ffn2, g_final,
           w_ffn_gate, w_ffn_up, w_ffn_down,
           a_w_pw1, a_b_pw1, a_w_dw, a_b_dw, a_ln_g, a_ln_b, a_w_pw2,
           b_w_in, b_b_in, b_ln_g, b_ln_b, b_w_s, b_b_s, b_w_out,
           c_w_in, c_w_conv, c_w_out):
    batch, seq, d = x_prompt.shape
    n_sample = x_sample.shape[0]
    assert x_sample.shape[1] == 1
    n_prompt = batch * seq
    depth = g_ffn1.shape[0]
    past_a = state_conv_a.shape[2]
    past_c = state_conv_c.shape[2]
    d_sgu = b_ln_g.shape[-1]

    x = jnp.concatenate([x_prompt.reshape(n_prompt, d), x_sample.reshape(n_sample, d)], axis=0)

    row3 = lambda a: a.reshape(a.shape[0], 1, a.shape[1])
    g1, gm, g2 = row3(g_ffn1), row3(g_mix), row3(g_ffn2)
    gf = g_final.reshape(1, d)
    a_b1, a_bd, a_lg, a_lb = row3(a_b_pw1), row3(a_b_dw), row3(a_ln_g), row3(a_ln_b)
    b_bi, b_lg, b_lb = row3(b_b_in), row3(b_ln_g), row3(b_ln_b)
    b_bs = b_b_s.reshape(*b_b_s.shape, 1)

    ffn = functools.partial(_ffn, tm=832, tf=256)
    new_a_p, new_a_s, new_b_p, new_b_s, new_c_p, new_c_s = [], [], [], [], [], []

    for i in range(depth):
        x = ffn(x, g1, gf, w_ffn_gate, w_ffn_up, w_ffn_down, i, 0, final_norm=False)
        kind, j = i % 3, i // 3
        if kind == 0:
            glu = _a1(x, gm, a_w_pw1, a_b1, i, j, tm=832, tn=256)
            y_s, st_s = _a_sample_conv(state_conv_a, glu, a_w_dw, a_bd, j, n_prompt=n_prompt, tb=8)
            x = _a2_prompt(x, glu, a_w_dw, a_bd, a_lg, a_lb, a_w_pw2, j,
                           n_prompt=n_prompt, seq=seq, tm=1024, tn=256, halo=32)
            x = _a2_sample(x, y_s, a_lg, a_lb, a_w_pw2, j, n_prompt=n_prompt, tn=256)
            new_a_p.append(glu[:n_prompt].reshape(batch, seq, d)[:, seq - past_a:])
            new_a_s.append(st_s)
        elif kind == 1:
            vn = _b1(x, gm, b_w_in, b_bi, b_lg, b_lb, i, j, tm=832, tn=256)
            x = _b2(x, gm, vn, b_w_in, b_bi, b_w_s, b_bs, b_w_out, i, j, n_prompt=n_prompt, tm=640)
            new_b_p.append(vn[:n_prompt].reshape(batch, seq, d_sgu)[:, seq - CHUNK:])
            new_b_s.append(vn[n_prompt:].reshape(n_sample, 1, d_sgu))
        else:
            bg, cx = _c1(x, gm, c_w_in, i, j, tm=832, tn=256)
            s0, s1 = state_conv_c[j, :, 0], state_conv_c[j, :, 1]
            cx_s = cx[n_prompt:]
            x = _c2_prompt(x, bg, cx, c_w_conv, c_w_out, j,
                           n_prompt=n_prompt, seq=seq, tm=1024, tn=256, halo=8)
            x = _c2_sample(x, bg, cx, s0, s1, c_w_conv, c_w_out, j, n_prompt=n_prompt, tn=256)
            new_c_p.append(cx[:n_prompt].reshape(batch, seq, d)[:, seq - past_c:])
            new_c_s.append(jnp.stack([s1, cx_s], axis=1))
        x = ffn(x, g2, gf, w_ffn_gate, w_ffn_up, w_ffn_down, i, 1, final_norm=(i == depth - 1))

    y_prompt = x[:n_prompt].reshape(batch, seq, d)
    y_sample = x[n_prompt:].reshape(n_sample, 1, d)
    return (y_prompt, y_sample,
            jnp.stack(new_a_p), jnp.stack(new_a_s),
            jnp.stack(new_b_p), jnp.stack(new_b_s),
            jnp.stack(new_c_p), jnp.stack(new_c_s))
```

```python
import functools

import jax
import jax.numpy as jnp
from jax import lax
from jax.experimental import pallas as pl
from jax.experimental.pallas import tpu as pltpu

F32 = jnp.float32
BF16 = jnp.bfloat16

EPS = 1e-6
FFN_HALF = 0.5
CHUNK = 128
N_SGU_GROUPS = 8
INV_SQRT2 = 0.7071067811865476
SUBLANES = 8
LANES = 128

V7X_VMEM_BYTES = 64 * 1024 * 1024
VMEM_LIMIT = V7X_VMEM_BYTES - 6 * 1024 * 1024


def _params(sem):
    return pltpu.CompilerParams(dimension_semantics=sem, vmem_limit_bytes=VMEM_LIMIT)


def _rms(x, g):
    ms = jnp.mean(x * x, axis=-1, keepdims=True)
    return (x * lax.rsqrt(ms + EPS)) * g


def _layernorm(y, g, b):
    mu = jnp.mean(y, axis=-1, keepdims=True)
    yc = y - mu
    var = jnp.mean(yc * yc, axis=-1, keepdims=True)
    return (yc * lax.rsqrt(var + EPS)) * g + b


def _silu(x):
    return x * jax.nn.sigmoid(x)


def _gelu(x):
    return 0.5 * x * (1.0 + lax.erf(x * INV_SQRT2))


def _dot(a, w):
    return jnp.dot(a, w, preferred_element_type=F32)


def _rms_rows(x_ref, g_ref, h_ref, copy_ref=None, rows_per_step=64, tail=None):
    rows = x_ref.shape[0]
    step = rows_per_step
    assert rows % step == 0
    g = g_ref[...]

    def body(c, carry):
        r0 = pl.multiple_of(c * step, step)
        x = x_ref[pl.ds(r0, step), :]
        if tail is not None:
            use_tail, first_chunk, tail_ref = tail
            ct = jnp.clip(c - first_chunk, 0, tail_ref.shape[0] // step - 1)
            xt = tail_ref[pl.ds(pl.multiple_of(ct * step, step), step), :]
            x = jnp.where(use_tail & (c >= first_chunk), xt, x)
        h_ref[pl.ds(r0, step), :] = _rms(x, g).astype(BF16)
        if copy_ref is not None:
            copy_ref[pl.ds(r0, step), :] = x
        return carry

    lax.fori_loop(0, rows // step, body, 0)


def _rms_rows_inplace(o_ref, g_ref, rows_per_step=64):
    rows, step = o_ref.shape[0], rows_per_step
    g = g_ref[...]

    def body(c, carry):
        r0 = pl.multiple_of(c * step, step)
        o_ref[pl.ds(r0, step), :] = _rms(o_ref[pl.ds(r0, step), :], g)
        return carry

    lax.fori_loop(0, rows // step, body, 0)


def _ffn_step(h_ref, wg, wu, wd, o_ref):
    h = h_ref[...]
    gate = _dot(h, wg)
    up = _dot(h, wu)
    act = (_silu(gate) * (up * FFN_HALF)).astype(BF16)
    o_ref[...] += _dot(act, wd)


def _ffn_head_kernel(x_ref, g_ref, gf_ref, wg_ref, wu_ref, wd_ref,
                     o_ref, wg16_ref, wu16_ref, wd16_ref, h_ref, *, final_norm):
    f = pl.program_id(0)

    @pl.when(f == 0)
    def _():
        _rms_rows(x_ref, g_ref, h_ref, copy_ref=o_ref)

    wg = wg_ref[...].astype(BF16)
    wu = wu_ref[...].astype(BF16)
    wd = wd_ref[...].astype(BF16)
    wg16_ref[...] = wg
    wu16_ref[...] = wu
    wd16_ref[...] = wd
    _ffn_step(h_ref, wg, wu, wd, o_ref)

    if final_norm:
        @pl.when(f == pl.num_programs(0) - 1)
        def _():
            _rms_rows_inplace(o_ref, gf_ref)


def _ffn_tail_kernel(*refs, first, last, split_row):
    refs = list(refs)
    x_ref, g_ref, gf_ref, wg_ref, wu_ref, wd_ref, _ = refs[:7]
    xs_ref = refs[7] if first else None
    n_in = 8 if first else 7
    o_ref = refs[n_in]
    h_ref = refs[-1]
    i, f = pl.program_id(0), pl.program_id(1)
    last_tile = i == pl.num_programs(0) - 1

    @pl.when(f == 0)
    def _():
        tail = (last_tile, split_row // 64, xs_ref) if first else None
        _rms_rows(x_ref, g_ref, h_ref, copy_ref=o_ref, tail=tail)

    _ffn_step(h_ref, wg_ref[...], wu_ref[...], wd_ref[...], o_ref)

    if last:
        @pl.when(f == pl.num_programs(1) - 1)
        def _():
            _rms_rows_inplace(o_ref, gf_ref)


def _ffn(x, xs, g3, gf2, w_gate, w_up, w_down, layer, which, *, n_prompt, n_sample,
         first, last, tm, tf_head, tf_tail):
    d = x.shape[1]
    dff = w_gate.shape[-1]
    m = n_prompt + n_sample
    assert m % tm == 0
    n_tiles = m // tm
    split_row = n_prompt - (n_tiles - 1) * tm
    assert 0 < split_row < tm and split_row % 64 == 0 and tm - split_row == n_sample
    out_rows = m
    name = f"ffn_{layer}_{which}"

    head_out, wg16, wu16, wd16 = pl.pallas_call(
        functools.partial(_ffn_head_kernel, final_norm=last),
        grid=(dff // tf_head,),
        in_specs=[
            pl.BlockSpec((tm, d), lambda f: (0, 0)),
            pl.BlockSpec((None, 1, d), lambda f: (layer, 0, 0)),
            pl.BlockSpec((1, d), lambda f: (0, 0)),
            pl.BlockSpec((None, None, d, tf_head), lambda f: (layer, which, 0, f)),
            pl.BlockSpec((None, None, d, tf_head), lambda f: (layer, which, 0, f)),
            pl.BlockSpec((None, None, tf_head, d), lambda f: (layer, which, f, 0)),
        ],
        out_specs=[
            pl.BlockSpec((tm, d), lambda f: (0, 0)),
            pl.BlockSpec((d, tf_head), lambda f: (0, f)),
            pl.BlockSpec((d, tf_head), lambda f: (0, f)),
            pl.BlockSpec((tf_head, d), lambda f: (f, 0)),
        ],
        out_shape=[
            jax.ShapeDtypeStruct((out_rows, d), F32),
            jax.ShapeDtypeStruct((d, dff), BF16),
            jax.ShapeDtypeStruct((d, dff), BF16),
            jax.ShapeDtypeStruct((dff, d), BF16),
        ],
        scratch_shapes=[pltpu.VMEM((tm, d), BF16)],
        compiler_params=_params(("arbitrary",)),
        name=name + "_head",
    )(x, g3, gf2, w_gate, w_up, w_down)

    in_specs = [
        pl.BlockSpec((tm, d), lambda i, f: (i + 1, 0)),
        pl.BlockSpec((None, 1, d), lambda i, f: (layer, 0, 0)),
        pl.BlockSpec((1, d), lambda i, f: (0, 0)),
        pl.BlockSpec((d, tf_tail), lambda i, f: (0, f)),
        pl.BlockSpec((d, tf_tail), lambda i, f: (0, f)),
        pl.BlockSpec((tf_tail, d), lambda i, f: (f, 0)),
        pl.BlockSpec(memory_space=pl.ANY),
    ]
    args = [x, g3, gf2, wg16, wu16, wd16, head_out]
    if first:
        in_specs.append(pl.BlockSpec((n_sample, d), lambda i, f: (0, 0)))
        args.append(xs)
    return pl.pallas_call(
        functools.partial(_ffn_tail_kernel, first=first, last=last, split_row=split_row),
        grid=(n_tiles - 1, dff // tf_tail),
        in_specs=in_specs,
        out_specs=pl.BlockSpec((tm, d), lambda i, f: (i + 1, 0)),
        out_shape=jax.ShapeDtypeStruct((out_rows, d), F32),
        scratch_shapes=[pltpu.VMEM((tm, d), BF16)],
        input_output_aliases={6: 0},
        compiler_params=_params(("arbitrary", "arbitrary")),
        name=name + "_tail",
    )(*args)


def _a1_kernel(x_ref, g_ref, wa_ref, wg_ref, ba_ref, bg_ref, o_ref, h_ref):
    @pl.when(pl.program_id(1) == 0)
    def _():
        _rms_rows(x_ref, g_ref, h_ref)

    h = h_ref[...]
    a = _dot(h, wa_ref[...].astype(BF16)) + ba_ref[...]
    gt = _dot(h, wg_ref[...].astype(BF16)) + bg_ref[...]
    o_ref[...] = a * jax.nn.sigmoid(gt)


def _a1(x, g3, w1, b1, layer, j, *, tm, tn):
    m, d = x.shape
    d_a = w1.shape[-1] // 2
    nb = d_a // tn
    return pl.pallas_call(
        _a1_kernel,
        grid=(m // tm, nb),
        in_specs=[
            pl.BlockSpec((tm, d), lambda i, n: (i, 0)),
            pl.BlockSpec((None, 1, d), lambda i, n: (layer, 0, 0)),
            pl.BlockSpec((None, d, tn), lambda i, n: (j, 0, n)),
            pl.BlockSpec((None, d, tn), lambda i, n: (j, 0, n + nb)),
            pl.BlockSpec((None, 1, tn), lambda i, n: (j, 0, n)),
            pl.BlockSpec((None, 1, tn), lambda i, n: (j, 0, n + nb)),
        ],
        out_specs=pl.BlockSpec((tm, tn), lambda i, n: (i, n)),
        out_shape=jax.ShapeDtypeStruct((m, d_a), F32),
        scratch_shapes=[pltpu.VMEM((tm, d), BF16)],
        compiler_params=_params(("parallel", "arbitrary")),
        name=f"mix_a1_{j}",
    )(x, g3, w1, w1, b1, b1)


def _fill_xp(xp_ref, halo_ref, cur_ref, first_in_seq):
    hr = halo_ref.shape[0]
    halo = halo_ref[...]
    xp_ref[0:hr, :] = jnp.where(first_in_seq, jnp.zeros_like(halo), halo)
    rows, step = cur_ref.shape[0], 64

    def body(c, carry):
        r0 = pl.multiple_of(c * step, step)
        xp_ref[pl.ds(hr + r0, step), :] = cur_ref[pl.ds(r0, step), :]
        return carry

    lax.fori_loop(0, rows // step, body, 0)


def _dwconv_block(xp_ref, w_ref, r0, rows, cols, taps, halo):
    lead = halo - (taps - 1)
    assert halo % SUBLANES == 0 and lead >= 0
    acc = None
    for s in range(SUBLANES):
        group = [k for k in range(taps) if (k + lead) % SUBLANES == s]
        if not group:
            continue
        ext = rows + (SUBLANES if s else 0)
        part = None
        for k in group:
            base = r0 + ((k + lead) // SUBLANES) * SUBLANES
            term = xp_ref[pl.ds(base, ext), cols] * w_ref[k:k + 1, cols]
            part = term if part is None else part + term
        if s:
            part = part[s:s + rows]
        acc = part if acc is None else acc + part
    return acc


def _a2p_kernel(x_ref, cur_ref, halo_ref, w_ref, b_ref, lg_ref, lb_ref, w2_ref, o_ref,
                xp_ref, y_ref, act_ref, *, taps, tiles_per_seq, conv_rows):
    i = pl.program_id(0)
    tm, d = cur_ref.shape
    halo = halo_ref.shape[0]
    ln_rows = y_ref.shape[0]

    @pl.when(pl.program_id(1) == 0)
    def _():
        _fill_xp(xp_ref, halo_ref, cur_ref, i % tiles_per_seq == 0)
        bias, lg, lb = b_ref[...], lg_ref[...], lb_ref[...]
        for rb in range(tm // ln_rows):
            def col_body(c, carry, rb=rb):
                cols = pl.ds(pl.multiple_of(c * LANES, LANES), LANES)
                for sub in range(ln_rows // conv_rows):
                    y_ref[sub * conv_rows:(sub + 1) * conv_rows, cols] = _dwconv_block(
                        xp_ref, w_ref, rb * ln_rows + sub * conv_rows, conv_rows, cols, taps, halo)
                return carry

            lax.fori_loop(0, d // LANES, col_body, 0)
            step = 32
            for s in range(ln_rows // step):
                y = y_ref[s * step:(s + 1) * step, :] + bias
                r0 = rb * ln_rows + s * step
                act_ref[r0:r0 + step, :] = _silu(_layernorm(y, lg, lb)).astype(BF16)

    o_ref[...] = x_ref[...] + _dot(act_ref[...], w2_ref[...].astype(BF16))


def _a2_prompt(x, glu, w_dw, b_dw, ln_g, ln_b, w2, j, *, n_prompt, seq, tm, tn, halo):
    m, d = x.shape
    taps = w_dw.shape[1]
    hb = tm // halo
    kern = functools.partial(_a2p_kernel, taps=taps, tiles_per_seq=seq // tm, conv_rows=128)
    return pl.pallas_call(
        kern,
        grid=(n_prompt // tm, d // tn),
        in_specs=[
            pl.BlockSpec((tm, tn), lambda i, n: (i, n)),
            pl.BlockSpec((tm, d), lambda i, n: (i, 0)),
            pl.BlockSpec((halo, d), lambda i, n: (jnp.maximum(i * hb - 1, 0), 0)),
            pl.BlockSpec((None, taps, d), lambda i, n: (j, 0, 0)),
            pl.BlockSpec((None, 1, d), lambda i, n: (j, 0, 0)),
            pl.BlockSpec((None, 1, d), lambda i, n: (j, 0, 0)),
            pl.BlockSpec((None, 1, d), lambda i, n: (j, 0, 0)),
            pl.BlockSpec((None, d, tn), lambda i, n: (j, 0, n)),
        ],
        out_specs=pl.BlockSpec((tm, tn), lambda i, n: (i, n)),
        out_shape=jax.ShapeDtypeStruct((m, d), F32),
        scratch_shapes=[
            pltpu.VMEM((halo + tm, d), F32),
            pltpu.VMEM((256, d), F32),
            pltpu.VMEM((tm, d), BF16),
        ],
        input_output_aliases={0: 0},
        compiler_params=_params(("parallel", "arbitrary")),
        name=f"mix_a2p_{j}",
    )(x, glu, glu, w_dw, b_dw, ln_g, ln_b, w2)


def _a_sconv_kernel(*refs, aliased):
    st_ref, glu_ref, w_ref, b_ref = refs[:4]
    y_ref, ns_ref = refs[-2:]
    past = st_ref.shape[0]
    acc = st_ref[0] * w_ref[0:1, :]
    for k in range(1, past):
        acc = acc + st_ref[k] * w_ref[k:k + 1, :]
        ns_ref[k - 1] = st_ref[k]
    g = glu_ref[...]
    ns_ref[past - 1] = g
    y_ref[...] = acc + g * w_ref[past:past + 1, :] + b_ref[...]


def _a_sample_conv(state_t, glu, w_dw, b_dw, prev, j, *, n_prompt, tn):
    n_layers, past, nb, d = state_t.shape
    taps = w_dw.shape[1]
    assert taps == past + 1
    rb = n_prompt // nb
    in_specs = [
        pl.BlockSpec((None, past, nb, tn), lambda n: (j, 0, 0, n)),
        pl.BlockSpec((nb, tn), lambda n: (rb, n)),
        pl.BlockSpec((None, taps, tn), lambda n: (j, 0, n)),
        pl.BlockSpec((None, 1, tn), lambda n: (j, 0, n)),
    ]
    args = [state_t, glu, w_dw, b_dw]
    aliases = {}
    if prev is not None:
        in_specs.append(pl.BlockSpec(memory_space=pl.ANY))
        args.append(prev)
        aliases = {4: 1}
    return pl.pallas_call(
        functools.partial(_a_sconv_kernel, aliased=prev is not None),
        grid=(d // tn,),
        in_specs=in_specs,
        out_specs=[
            pl.BlockSpec((nb, tn), lambda n: (0, n)),
            pl.BlockSpec((None, past, nb, tn), lambda n: (j, 0, 0, n)),
        ],
        out_shape=[
            jax.ShapeDtypeStruct((nb, d), F32),
            jax.ShapeDtypeStruct((n_layers, past, nb, d), F32),
        ],
        input_output_aliases=aliases,
        compiler_params=_params(("parallel",)),
        name=f"mix_a_sconv_{j}",
    )(*args)


def _a2s_kernel(x_ref, y_ref, lg_ref, lb_ref, w2_ref, o_ref, act_ref):
    @pl.when(pl.program_id(0) == 0)
    def _():
        act_ref[...] = _silu(_layernorm(y_ref[...], lg_ref[...], lb_ref[...])).astype(BF16)

    o_ref[...] = x_ref[...] + _dot(act_ref[...], w2_ref[...].astype(BF16))


def _a2_sample(x, y, ln_g, ln_b, w2, j, *, n_prompt, tn):
    m, d = x.shape
    ns = y.shape[0]
    rb = n_prompt // ns
    return pl.pallas_call(
        _a2s_kernel,
        grid=(d // tn,),
        in_specs=[
            pl.BlockSpec((ns, tn), lambda n: (rb, n)),
            pl.BlockSpec((ns, d), lambda n: (0, 0)),
            pl.BlockSpec((None, 1, d), lambda n: (j, 0, 0)),
            pl.BlockSpec((None, 1, d), lambda n: (j, 0, 0)),
            pl.BlockSpec((None, d, tn), lambda n: (j, 0, n)),
        ],
        out_specs=pl.BlockSpec((ns, tn), lambda n: (rb, n)),
        out_shape=jax.ShapeDtypeStruct((m, d), F32),
        scratch_shapes=[pltpu.VMEM((ns, d), BF16)],
        input_output_aliases={0: 0},
        compiler_params=_params(("arbitrary",)),
        name=f"mix_a2s_{j}",
    )(x, y, ln_g, ln_b, w2)


def _b1_kernel(x_ref, g_ref, w_ref, b_ref, lg_ref, lb_ref, o_ref, h_ref):
    n = pl.program_id(1)
    tn = w_ref.shape[1]

    @pl.when(n == 0)
    def _():
        _rms_rows(x_ref, g_ref, h_ref)

    z = _dot(h_ref[...], w_ref[...].astype(BF16)) + b_ref[...]
    o_ref[:, pl.ds(pl.multiple_of(n * tn, tn), tn)] = _gelu(z)

    @pl.when(n == pl.num_programs(1) - 1)
    def _():
        rows, step = o_ref.shape[0], 32
        lg, lb = lg_ref[...], lb_ref[...]

        def body(c, carry):
            r0 = pl.multiple_of(c * step, step)
            o_ref[pl.ds(r0, step), :] = _layernorm(o_ref[pl.ds(r0, step), :], lg, lb)
            return carry

        lax.fori_loop(0, rows // step, body, 0)


def _b1(x, g3, w_in, b_in, ln_g, ln_b, layer, j, *, tm, tn):
    m, d = x.shape
    d_sgu = w_in.shape[-1] // 2
    nb = d_sgu // tn
    return pl.pallas_call(
        _b1_kernel,
        grid=(m // tm, nb),
        in_specs=[
            pl.BlockSpec((tm, d), lambda i, n: (i, 0)),
            pl.BlockSpec((None, 1, d), lambda i, n: (layer, 0, 0)),
            pl.BlockSpec((None, d, tn), lambda i, n: (j, 0, n + nb)),
            pl.BlockSpec((None, 1, tn), lambda i, n: (j, 0, n + nb)),
            pl.BlockSpec((None, 1, d_sgu), lambda i, n: (j, 0, 0)),
            pl.BlockSpec((None, 1, d_sgu), lambda i, n: (j, 0, 0)),
        ],
        out_specs=pl.BlockSpec((tm, d_sgu), lambda i, n: (i, 0)),
        out_shape=jax.ShapeDtypeStruct((m, d_sgu), F32),
        scratch_shapes=[pltpu.VMEM((tm, d), BF16)],
        compiler_params=_params(("parallel", "arbitrary")),
        name=f"mix_b1_{j}",
    )(x, g3, w_in, b_in, ln_g, ln_b)


def _b2_kernel(x_ref, g_ref, v_ref, wi_ref, bi_ref, ws_ref, bs_ref, wo_ref, o_ref, h_ref,
               *, n_prompt_chunks):
    i = pl.program_id(0)
    tm = x_ref.shape[0]

    @pl.when(pl.program_id(1) == 0)
    def _():
        _rms_rows(x_ref, g_ref, h_ref, copy_ref=o_ref)

    u = _gelu(_dot(h_ref[...], wi_ref[...].astype(BF16)) + bi_ref[...])
    ws = ws_ref[...]
    row = lax.broadcasted_iota(jnp.int32, ws.shape, 0)
    col = lax.broadcasted_iota(jnp.int32, ws.shape, 1)
    w_tril = jnp.where(col <= row, ws, 0.0).astype(BF16)
    bs = bs_ref[...]
    parts = []
    for c in range(tm // CHUNK):
        rows = slice(c * CHUNK, (c + 1) * CHUNK)
        v = v_ref[rows, :]
        s_prompt = _dot(w_tril, v.astype(BF16)) + bs
        s_sample = ws[0:1, 0:1] * v + bs[0:1, :]
        is_sample = i * (tm // CHUNK) + c >= n_prompt_chunks
        parts.append(u[rows, :] * jnp.where(is_sample, s_sample, s_prompt))
    y = jnp.concatenate(parts, axis=0).astype(BF16)
    o_ref[...] += _dot(y, wo_ref[...].astype(BF16))


def _b2(x, g3, vn, w_in, b_in, w_s, b_s3, w_out, layer, j, *, n_prompt, tm):
    m, d = x.shape
    d_sgu = vn.shape[1]
    gw = d_sgu // N_SGU_GROUPS
    kern = functools.partial(_b2_kernel, n_prompt_chunks=n_prompt // CHUNK)
    return pl.pallas_call(
        kern,
        grid=(m // tm, N_SGU_GROUPS),
        in_specs=[
            pl.BlockSpec((tm, d), lambda i, g: (i, 0)),
            pl.BlockSpec((None, 1, d), lambda i, g: (layer, 0, 0)),
            pl.BlockSpec((tm, gw), lambda i, g: (i, g)),
            pl.BlockSpec((None, d, gw), lambda i, g: (j, 0, g)),
            pl.BlockSpec((None, 1, gw), lambda i, g: (j, 0, g)),
            pl.BlockSpec((None, None, CHUNK, CHUNK), lambda i, g: (j, g, 0, 0)),
            pl.BlockSpec((None, None, CHUNK, 1), lambda i, g: (j, g, 0, 0)),
            pl.BlockSpec((None, gw, d), lambda i, g: (j, g, 0)),
        ],
        out_specs=pl.BlockSpec((tm, d), lambda i, g: (i, 0)),
        out_shape=jax.ShapeDtypeStruct((m, d), F32),
        scratch_shapes=[pltpu.VMEM((tm, d), BF16)],
        compiler_params=_params(("parallel", "arbitrary")),
        name=f"mix_b2_{j}",
    )(x, g3, vn, w_in, b_in, w_s, b_s3, w_out)


def _c1_kernel(x_ref, g_ref, wb_ref, wc_ref, wx_ref, bg_ref, cx_ref, h_ref):
    @pl.when(pl.program_id(1) == 0)
    def _():
        _rms_rows(x_ref, g_ref, h_ref)

    h = h_ref[...]
    bg_ref[...] = _dot(h, wb_ref[...].astype(BF16))
    cx_ref[...] = _dot(h, wc_ref[...].astype(BF16)) * _dot(h, wx_ref[...].astype(BF16))


def _c1(x, g3, w_in, layer, j, *, tm, tn):
    m, d = x.shape
    d_c = w_in.shape[-1] // 3
    nb = d_c // tn
    out = jax.ShapeDtypeStruct((m, d_c), F32)
    return pl.pallas_call(
        _c1_kernel,
        grid=(m // tm, nb),
        in_specs=[
            pl.BlockSpec((tm, d), lambda i, n: (i, 0)),
            pl.BlockSpec((None, 1, d), lambda i, n: (layer, 0, 0)),
            pl.BlockSpec((None, d, tn), lambda i, n: (j, 0, n)),
            pl.BlockSpec((None, d, tn), lambda i, n: (j, 0, n + nb)),
            pl.BlockSpec((None, d, tn), lambda i, n: (j, 0, n + 2 * nb)),
        ],
        out_specs=[pl.BlockSpec((tm, tn), lambda i, n: (i, n))] * 2,
        out_shape=[out, out],
        scratch_shapes=[pltpu.VMEM((tm, d), BF16)],
        compiler_params=_params(("parallel", "arbitrary")),
        name=f"mix_c1_{j}",
    )(x, g3, w_in, w_in, w_in)


def _c2p_kernel(x_ref, bg_ref, cur_ref, halo_ref, w_ref, wo_ref, o_ref, xp_ref, act_ref,
                *, taps, tiles_per_seq, conv_rows):
    i = pl.program_id(0)
    tm, d = cur_ref.shape
    halo = halo_ref.shape[0]

    @pl.when(pl.program_id(1) == 0)
    def _():
        _fill_xp(xp_ref, halo_ref, cur_ref, i % tiles_per_seq == 0)

        def col_body(c, carry):
            cols = pl.ds(pl.multiple_of(c * LANES, LANES), LANES)
            for rb in range(tm // conv_rows):
                rows = pl.ds(rb * conv_rows, conv_rows)
                y = _dwconv_block(xp_ref, w_ref, rb * conv_rows, conv_rows, cols, taps, halo)
                act_ref[rows, cols] = (bg_ref[rows, cols] * y).astype(BF16)
            return carry

        lax.fori_loop(0, d // LANES, col_body, 0)

    o_ref[...] = x_ref[...] + _dot(act_ref[...], wo_ref[...].astype(BF16))


def _c2_prompt(x, bg, cx, w_conv, w_out, j, *, n_prompt, seq, tm, tn, halo):
    m, d = x.shape
    taps = w_conv.shape[1]
    hb = tm // halo
    kern = functools.partial(_c2p_kernel, taps=taps, tiles_per_seq=seq // tm, conv_rows=128)
    return pl.pallas_call(
        kern,
        grid=(n_prompt // tm, d // tn),
        in_specs=[
            pl.BlockSpec((tm, tn), lambda i, n: (i, n)),
            pl.BlockSpec((tm, d), lambda i, n: (i, 0)),
            pl.BlockSpec((tm, d), lambda i, n: (i, 0)),
            pl.BlockSpec((halo, d), lambda i, n: (jnp.maximum(i * hb - 1, 0), 0)),
            pl.BlockSpec((None, taps, d), lambda i, n: (j, 0, 0)),
            pl.BlockSpec((None, d, tn), lambda i, n: (j, 0, n)),
        ],
        out_specs=pl.BlockSpec((tm, tn), lambda i, n: (i, n)),
        out_shape=jax.ShapeDtypeStruct((m, d), F32),
        scratch_shapes=[
            pltpu.VMEM((halo + tm, d), F32),
            pltpu.VMEM((tm, d), BF16),
        ],
        input_output_aliases={0: 0},
        compiler_params=_params(("parallel", "arbitrary")),
        name=f"mix_c2p_{j}",
    )(x, bg, cx, cx, w_conv, w_out)


def _c2s_kernel(x_ref, bg_ref, cx_ref, s0_ref, s1_ref, w_ref, wo_ref, o_ref, act_ref):
    @pl.when(pl.program_id(0) == 0)
    def _():
        w = w_ref[...]
        y = s0_ref[...] * w[0:1] + s1_ref[...] * w[1:2] + cx_ref[...] * w[2:3]
        act_ref[...] = (bg_ref[...] * y).astype(BF16)

    o_ref[...] = x_ref[...] + _dot(act_ref[...], wo_ref[...].astype(BF16))


def _c2_sample(x, bg, cx, s0, s1, w_conv, w_out, j, *, n_prompt, tn):
    m, d = x.shape
    ns = s0.shape[0]
    rb = n_prompt // ns
    taps = w_conv.shape[1]
    assert taps == 3
    return pl.pallas_call(
        _c2s_kernel,
        grid=(d // tn,),
        in_specs=[
            pl.BlockSpec((ns, tn), lambda n: (rb, n)),
            pl.BlockSpec((ns, d), lambda n: (rb, 0)),
            pl.BlockSpec((ns, d), lambda n: (rb, 0)),
            pl.BlockSpec((ns, d), lambda n: (0, 0)),
            pl.BlockSpec((ns, d), lambda n: (0, 0)),
            pl.BlockSpec((None, taps, d), lambda n: (j, 0, 0)),
            pl.BlockSpec((None, d, tn), lambda n: (j, 0, n)),
        ],
        out_specs=pl.BlockSpec((ns, tn), lambda n: (rb, n)),
        out_shape=jax.ShapeDtypeStruct((m, d), F32),
        scratch_shapes=[pltpu.VMEM((ns, d), BF16)],
        input_output_aliases={0: 0},
        compiler_params=_params(("arbitrary",)),
        name=f"mix_c2s_{j}",
    )(x, bg, cx, s0, s1, w_conv, w_out)


def _last_rows(a, batch, seq, rows):
    return jnp.stack([a[(b + 1) * seq - rows:(b + 1) * seq] for b in range(batch)])


def kernel(x_prompt, x_sample, state_conv_a, state_conv_c, g_ffn1, g_mix, g_ffn2, g_final,
           w_ffn_gate, w_ffn_up, w_ffn_down,
           a_w_pw1, a_b_pw1, a_w_dw, a_b_dw, a_ln_g, a_ln_b, a_w_pw2,
           b_w_in, b_b_in, b_ln_g, b_ln_b, b_w_s, b_b_s, b_w_out,
           c_w_in, c_w_conv, c_w_out):
    batch, seq, d = x_prompt.shape
    n_sample = x_sample.shape[0]
    assert x_sample.shape[1] == 1
    n_prompt = batch * seq
    depth = g_ffn1.shape[0]
    past_a = state_conv_a.shape[2]
    past_c = state_conv_c.shape[2]
    assert past_c == 2
    d_sgu = b_ln_g.shape[-1]

    row3 = lambda a: a.reshape(a.shape[0], 1, a.shape[1])
    g1, gm, g2 = row3(g_ffn1), row3(g_mix), row3(g_ffn2)
    gf = g_final.reshape(1, d)
    a_b1, a_bd, a_lg, a_lb = row3(a_b_pw1), row3(a_b_dw), row3(a_ln_g), row3(a_ln_b)
    b_bi, b_lg, b_lb = row3(b_b_in), row3(b_ln_g), row3(b_ln_b)
    b_bs = b_b_s.reshape(*b_b_s.shape, 1)
    state_a_t = jnp.transpose(state_conv_a, (0, 2, 1, 3))

    ffn = functools.partial(_ffn, n_prompt=n_prompt, n_sample=n_sample,
                            tm=832, tf_head=256, tf_tail=512)
    new_a_p, new_b_p, new_b_s, new_c_p, new_c_s = [], [], [], [], []
    new_a_s_t = None

    x = x_prompt.reshape(n_prompt, d)
    xs = x_sample.reshape(n_sample, d)
    for i in range(depth):
        x = ffn(x, xs, g1, gf, w_ffn_gate, w_ffn_up, w_ffn_down, i, 0, first=(i == 0), last=False)
        kind, j = i % 3, i // 3
        if kind == 0:
            glu = _a1(x, gm, a_w_pw1, a_b1, i, j, tm=832, tn=256)
            y_s, new_a_s_t = _a_sample_conv(state_a_t, glu, a_w_dw, a_bd, new_a_s_t, j,
                                            n_prompt=n_prompt, tn=256)
            x = _a2_prompt(x, glu, a_w_dw, a_bd, a_lg, a_lb, a_w_pw2, j,
                           n_prompt=n_prompt, seq=seq, tm=1024, tn=256, halo=32)
            x = _a2_sample(x, y_s, a_lg, a_lb, a_w_pw2, j, n_prompt=n_prompt, tn=256)
            new_a_p.append(_last_rows(glu, batch, seq, past_a))
        elif kind == 1:
            vn = _b1(x, gm, b_w_in, b_bi, b_lg, b_lb, i, j, tm=832, tn=256)
            x = _b2(x, gm, vn, b_w_in, b_bi, b_w_s, b_bs, b_w_out, i, j, n_prompt=n_prompt, tm=640)
            new_b_p.append(_last_rows(vn, batch, seq, CHUNK))
            new_b_s.append(vn[n_prompt:].reshape(n_sample, 1, d_sgu))
        else:
            bg, cx = _c1(x, gm, c_w_in, i, j, tm=832, tn=256)
            s0, s1 = state_conv_c[j, :, 0], state_conv_c[j, :, 1]
            x = _c2_prompt(x, bg, cx, c_w_conv, c_w_out, j,
                           n_prompt=n_prompt, seq=seq, tm=1024, tn=256, halo=8)
            x = _c2_sample(x, bg, cx, s0, s1, c_w_conv, c_w_out, j, n_prompt=n_prompt, tn=256)
            new_c_p.append(_last_rows(cx, batch, seq, past_c))
            new_c_s.append(jnp.stack([s1, cx[n_prompt:]], axis=1))
        last = i == depth - 1
        x = ffn(x, xs, g2, gf, w_ffn_gate, w_ffn_up, w_ffn_down, i, 1, first=False, last=last)

    return (x[:n_prompt].reshape(batch, seq, d), x[n_prompt:].reshape(n_sample, 1, d),
            jnp.stack(new_a_p), jnp.transpose(new_a_s_t, (0, 2, 1, 3)),
            jnp.stack(new_b_p), jnp.stack(new_b_s),
            jnp.stack(new_c_p), jnp.stack(new_c_s))
```

```python
import functools

import jax
import jax.numpy as jnp
from jax import lax
from jax.experimental import pallas as pl
from jax.experimental.pallas import tpu as pltpu

F32 = jnp.float32
BF16 = jnp.bfloat16

EPS = 1e-6
FFN_HALF = 0.5
CHUNK = 128
N_SGU_GROUPS = 8
INV_SQRT2 = 0.7071067811865476
SUBLANES = 8
LANES = 128

V7X_VMEM_BYTES = 64 * 1024 * 1024
VMEM_LIMIT = V7X_VMEM_BYTES - 6 * 1024 * 1024


def _params(sem):
    return pltpu.CompilerParams(dimension_semantics=sem, vmem_limit_bytes=VMEM_LIMIT)


def _rms(x, g):
    ms = jnp.mean(x * x, axis=-1, keepdims=True)
    return (x * lax.rsqrt(ms + EPS)) * g


def _layernorm(y, g, b):
    mu = jnp.mean(y, axis=-1, keepdims=True)
    yc = y - mu
    var = jnp.mean(yc * yc, axis=-1, keepdims=True)
    return (yc * lax.rsqrt(var + EPS)) * g + b


def _silu(x):
    return x * jax.nn.sigmoid(x)


def _gelu(x):
    return 0.5 * x * (1.0 + lax.erf(x * INV_SQRT2))


def _dot(a, w):
    return jnp.dot(a, w, preferred_element_type=F32)


def _rms_rows(x_ref, g_ref, h_ref, copy_ref=None, rows_per_step=64, tail=None):
    rows = x_ref.shape[0]
    step = rows_per_step
    assert rows % step == 0
    g = g_ref[...]

    def body(c, carry):
        r0 = pl.multiple_of(c * step, step)
        x = x_ref[pl.ds(r0, step), :]
        if tail is not None:
            use_tail, first_chunk, tail_ref = tail
            ct = jnp.clip(c - first_chunk, 0, tail_ref.shape[0] // step - 1)
            xt = tail_ref[pl.ds(pl.multiple_of(ct * step, step), step), :]
            x = jnp.where(use_tail & (c >= first_chunk), xt, x)
        h_ref[pl.ds(r0, step), :] = _rms(x, g).astype(BF16)
        if copy_ref is not None:
            copy_ref[pl.ds(r0, step), :] = x
        return carry

    lax.fori_loop(0, rows // step, body, 0)


def _rms_rows_inplace(o_ref, g_ref, rows_per_step=64):
    rows, step = o_ref.shape[0], rows_per_step
    g = g_ref[...]

    def body(c, carry):
        r0 = pl.multiple_of(c * step, step)
        o_ref[pl.ds(r0, step), :] = _rms(o_ref[pl.ds(r0, step), :], g)
        return carry

    lax.fori_loop(0, rows // step, body, 0)


def _ffn_step(h_ref, wg, wu, wd, o_ref):
    h = h_ref[...]
    gate = _dot(h, wg)
    up = _dot(h, wu)
    act = (_silu(gate) * (up * FFN_HALF)).astype(BF16)
    o_ref[...] += _dot(act, wd)


def _ffn_head_kernel(x_ref, g_ref, gf_ref, wg_ref, wu_ref, wd_ref,
                     o_ref, wg16_ref, wu16_ref, wd16_ref, h_ref, *, final_norm):
    f = pl.program_id(0)

    @pl.when(f == 0)
    def _():
        _rms_rows(x_ref, g_ref, h_ref, copy_ref=o_ref)

    wg = wg_ref[...].astype(BF16)
    wu = wu_ref[...].astype(BF16)
    wd = wd_ref[...].astype(BF16)
    wg16_ref[...] = wg
    wu16_ref[...] = wu
    wd16_ref[...] = wd
    _ffn_step(h_ref, wg, wu, wd, o_ref)

    if final_norm:
        @pl.when(f == pl.num_programs(0) - 1)
        def _():
            _rms_rows_inplace(o_ref, gf_ref)


def _ffn_tail_kernel(*refs, first, last, split_row):
    refs = list(refs)
    x_ref, g_ref, gf_ref, wg_ref, wu_ref, wd_ref, _ = refs[:7]
    xs_ref = refs[7] if first else None
    n_in = 8 if first else 7
    o_ref = refs[n_in]
    h_ref = refs[-1]
    i, f = pl.program_id(0), pl.program_id(1)
    last_tile = i == pl.num_programs(0) - 1

    @pl.when(f == 0)
    def _():
        tail = (last_tile, split_row // 64, xs_ref) if first else None
        _rms_rows(x_ref, g_ref, h_ref, copy_ref=o_ref, tail=tail)

    _ffn_step(h_ref, wg_ref[...], wu_ref[...], wd_ref[...], o_ref)

    if last:
        @pl.when(f == pl.num_programs(1) - 1)
        def _():
            _rms_rows_inplace(o_ref, gf_ref)


def _ffn(x, xs, g3, gf2, w_gate, w_up, w_down, layer, which, *, n_prompt, n_sample,
         first, last, tm, tf_head, tf_tail):
    d = x.shape[1]
    dff = w_gate.shape[-1]
    m = n_prompt + n_sample
    assert m % tm == 0
    n_tiles = m // tm
    split_row = n_prompt - (n_tiles - 1) * tm
    assert 0 < split_row < tm and split_row % 64 == 0 and tm - split_row == n_sample
    out_rows = m
    name = f"ffn_{layer}_{which}"

    head_out, wg16, wu16, wd16 = pl.pallas_call(
        functools.partial(_ffn_head_kernel, final_norm=last),
        grid=(dff // tf_head,),
        in_specs=[
            pl.BlockSpec((tm, d), lambda f: (0, 0)),
            pl.BlockSpec((None, 1, d), lambda f: (layer, 0, 0)),
            pl.BlockSpec((1, d), lambda f: (0, 0)),
            pl.BlockSpec((None, None, d, tf_head), lambda f: (layer, which, 0, f)),
            pl.BlockSpec((None, None, d, tf_head), lambda f: (layer, which, 0, f)),
            pl.BlockSpec((None, None, tf_head, d), lambda f: (layer, which, f, 0)),
        ],
        out_specs=[
            pl.BlockSpec((tm, d), lambda f: (0, 0)),
            pl.BlockSpec((d, tf_head), lambda f: (0, f)),
            pl.BlockSpec((d, tf_head), lambda f: (0, f)),
            pl.BlockSpec((tf_head, d), lambda f: (f, 0)),
        ],
        out_shape=[
            jax.ShapeDtypeStruct((out_rows, d), F32),
            jax.ShapeDtypeStruct((d, dff), BF16),
            jax.ShapeDtypeStruct((d, dff), BF16),
            jax.ShapeDtypeStruct((dff, d), BF16),
        ],
        scratch_shapes=[pltpu.VMEM((tm, d), BF16)],
        compiler_params=_params(("arbitrary",)),
        name=name + "_head",
    )(x, g3, gf2, w_gate, w_up, w_down)

    in_specs = [
        pl.BlockSpec((tm, d), lambda i, f: (i + 1, 0)),
        pl.BlockSpec((None, 1, d), lambda i, f: (layer, 0, 0)),
        pl.BlockSpec((1, d), lambda i, f: (0, 0)),
        pl.BlockSpec((d, tf_tail), lambda i, f: (0, f)),
        pl.BlockSpec((d, tf_tail), lambda i, f: (0, f)),
        pl.BlockSpec((tf_tail, d), lambda i, f: (f, 0)),
        pl.BlockSpec(memory_space=pl.ANY),
    ]
    args = [x, g3, gf2, wg16, wu16, wd16, head_out]
    if first:
        in_specs.append(pl.BlockSpec((n_sample, d), lambda i, f: (0, 0)))
        args.append(xs)
    return pl.pallas_call(
        functools.partial(_ffn_tail_kernel, first=first, last=last, split_row=split_row),
        grid=(n_tiles - 1, dff // tf_tail),
        in_specs=in_specs,
        out_specs=pl.BlockSpec((tm, d), lambda i, f: (i + 1, 0)),
        out_shape=jax.ShapeDtypeStruct((out_rows, d), F32),
        scratch_shapes=[pltpu.VMEM((tm, d), BF16)],
        input_output_aliases={6: 0},
        compiler_params=_params(("arbitrary", "arbitrary")),
        name=name + "_tail",
    )(*args)


def _a1_kernel(x_ref, g_ref, wa_ref, wg_ref, ba_ref, bg_ref, o_ref, h_ref):
    @pl.when(pl.program_id(1) == 0)
    def _():
        _rms_rows(x_ref, g_ref, h_ref)

    h = h_ref[...]
    a = _dot(h, wa_ref[...].astype(BF16)) + ba_ref[...]
    gt = _dot(h, wg_ref[...].astype(BF16)) + bg_ref[...]
    o_ref[...] = a * jax.nn.sigmoid(gt)


def _a1(x, g3, w1, b1, layer, j, *, row0, n_rows, tm, tn):
    d = x.shape[1]
    d_a = w1.shape[-1] // 2
    nb = d_a // tn
    rb0 = row0 // tm
    return pl.pallas_call(
        _a1_kernel,
        grid=(n_rows // tm, nb),
        in_specs=[
            pl.BlockSpec((tm, d), lambda i, n: (rb0 + i, 0)),
            pl.BlockSpec((None, 1, d), lambda i, n: (layer, 0, 0)),
            pl.BlockSpec((None, d, tn), lambda i, n: (j, 0, n)),
            pl.BlockSpec((None, d, tn), lambda i, n: (j, 0, n + nb)),
            pl.BlockSpec((None, 1, tn), lambda i, n: (j, 0, n)),
            pl.BlockSpec((None, 1, tn), lambda i, n: (j, 0, n + nb)),
        ],
        out_specs=pl.BlockSpec((tm, tn), lambda i, n: (i, n)),
        out_shape=jax.ShapeDtypeStruct((n_rows, d_a), F32),
        scratch_shapes=[pltpu.VMEM((tm, d), BF16)],
        compiler_params=_params(("parallel", "arbitrary")),
        name=f"mix_a1s_{j}",
    )(x, g3, w1, w1, b1, b1)


def _dwconv_block(xp_ref, w_ref, r0, rows, cols, taps, halo):
    lead = halo - (taps - 1)
    assert halo % SUBLANES == 0 and lead >= 0
    acc = None
    for s in range(SUBLANES):
        group = [k for k in range(taps) if (k + lead) % SUBLANES == s]
        if not group:
            continue
        ext = rows + (SUBLANES if s else 0)
        part = None
        for k in group:
            base = r0 + ((k + lead) // SUBLANES) * SUBLANES
            term = xp_ref[pl.ds(base, ext), cols] * w_ref[k:k + 1, cols]
            part = term if part is None else part + term
        if s:
            part = part[s:s + rows]
        acc = part if acc is None else acc + part
    return acc


def _start_tile(x_ref, g_ref, h_ref, hal_ref, first_in_seq):
    _rms_rows(x_ref, g_ref, h_ref)

    @pl.when(first_in_seq)
    def _():
        hal_ref[...] = jnp.zeros(hal_ref.shape, hal_ref.dtype)


def _af_kernel(x_ref, g_ref, wa_ref, wg_ref, ba_ref, bgt_ref, wdw_ref, bdw_ref, lg_ref, lb_ref,
               w2_ref, o_ref, st_ref, h_ref, gbuf_ref, wbuf_ref, hal_ref, y_ref, act_ref,
               *, taps, tiles_per_seq, nb, rc):
    i, n = pl.program_id(0), pl.program_id(1)
    tm = x_ref.shape[0]
    tn = wa_ref.shape[1]
    halo = hal_ref.shape[0]
    cb = 128

    @pl.when(n == 0)
    def _():
        _start_tile(x_ref, g_ref, h_ref, hal_ref, i % tiles_per_seq == 0)

    @pl.when(n < nb)
    def _():
        col0 = pl.multiple_of(n * tn, tn)
        gbuf_ref[0:halo, :] = hal_ref[:, pl.ds(col0, tn)]
        wbuf_ref[0:taps, :] = wdw_ref[:, pl.ds(col0, tn)]
        wa = wa_ref[...].astype(BF16)
        wg = wg_ref[...].astype(BF16)
        ba, bgt = ba_ref[...], bgt_ref[...]
        for r in range(tm // rc):
            hc = h_ref[r * rc:(r + 1) * rc, :]
            glu = (_dot(hc, wa) + ba) * jax.nn.sigmoid(_dot(hc, wg) + bgt)
            gbuf_ref[halo + r * rc:halo + (r + 1) * rc, :] = glu
            for sub in range(rc // cb):
                r0 = r * rc + sub * cb
                for strip in range(tn // LANES):
                    cols = slice(strip * LANES, (strip + 1) * LANES)
                    y = _dwconv_block(gbuf_ref, wbuf_ref, r0, cb, cols, taps, halo)
                    y_ref[r0:r0 + cb, pl.ds(pl.multiple_of(col0 + strip * LANES, LANES), LANES)] = y
        hal_ref[:, pl.ds(col0, tn)] = gbuf_ref[tm:tm + halo, :]

    @pl.when(n == nb)
    def _():
        w2 = w2_ref[...].astype(BF16)
        bias, lg, lb = bdw_ref[...], lg_ref[...], lb_ref[...]
        step = 32
        for r in range(tm // rc):
            for s in range(rc // step):
                r0 = r * rc + s * step
                y = y_ref[r0:r0 + step, :] + bias
                act_ref[r0:r0 + step, :] = _silu(_layernorm(y, lg, lb)).astype(BF16)
            rows = slice(r * rc, (r + 1) * rc)
            o_ref[rows, :] = x_ref[rows, 0:tn] + _dot(act_ref[rows, :], w2)

    @pl.when(n > nb)
    def _():
        col0 = pl.multiple_of((n - nb) * tn, tn)
        o_ref[...] = x_ref[:, pl.ds(col0, tn)] + _dot(act_ref[...], w2_ref[...].astype(BF16))

    @pl.when((n == 2 * nb - 1) & (i % tiles_per_seq == tiles_per_seq - 1))
    def _():
        st_ref[...] = hal_ref[halo - (taps - 1):halo, :]


def _a_fused(x, g3, w1, b1, w_dw, b_dw, ln_g, ln_b, w2, layer, j, *, n_prompt, seq, tm, tn):
    m, d = x.shape
    taps = w_dw.shape[1]
    nb = d // tn
    halo = 32
    assert w1.shape[-1] == 2 * d and taps - 1 <= halo and seq % tm == 0
    kern = functools.partial(_af_kernel, taps=taps, tiles_per_seq=seq // tm, nb=nb, rc=256)
    taps_pad = -(-taps // SUBLANES) * SUBLANES
    w1_blk = lambda i, n: (j, 0, jnp.minimum(n, nb - 1))
    w1g_blk = lambda i, n: (j, 0, jnp.minimum(n, nb - 1) + nb)
    out_blk = lambda i, n: (i, jnp.maximum(n - nb, 0))
    row = lambda i, n: (j, 0, 0)
    return pl.pallas_call(
        kern,
        grid=(n_prompt // tm, 2 * nb),
        in_specs=[
            pl.BlockSpec((tm, d), lambda i, n: (i, 0)),
            pl.BlockSpec((None, 1, d), lambda i, n: (layer, 0, 0)),
            pl.BlockSpec((None, d, tn), w1_blk),
            pl.BlockSpec((None, d, tn), w1g_blk),
            pl.BlockSpec((None, 1, tn), w1_blk),
            pl.BlockSpec((None, 1, tn), w1g_blk),
            pl.BlockSpec((None, taps, d), row),
            pl.BlockSpec((None, 1, d), row),
            pl.BlockSpec((None, 1, d), row),
            pl.BlockSpec((None, 1, d), row),
            pl.BlockSpec((None, d, tn), lambda i, n: (j, 0, jnp.maximum(n - nb, 0))),
        ],
        out_specs=[
            pl.BlockSpec((tm, tn), out_blk),
            pl.BlockSpec((None, taps - 1, d), lambda i, n: (i // (seq // tm), 0, 0)),
        ],
        out_shape=[
            jax.ShapeDtypeStruct((m, d), F32),
            jax.ShapeDtypeStruct((n_prompt // seq, taps - 1, d), F32),
        ],
        scratch_shapes=[
            pltpu.VMEM((tm, d), BF16),
            pltpu.VMEM((halo + tm, tn), F32),
            pltpu.VMEM((taps_pad, tn), F32),
            pltpu.VMEM((halo, d), F32),
            pltpu.VMEM((tm, d), F32),
            pltpu.VMEM((tm, d), BF16),
        ],
        input_output_aliases={0: 0},
        compiler_params=_params(("arbitrary", "arbitrary")),
        name=f"mix_a_{j}",
    )(x, g3, w1, w1, b1, b1, w_dw, b_dw, ln_g, ln_b, w2)


def _a_sconv_kernel(*refs):
    st_ref, glu_ref, w_ref, b_ref = refs[:4]
    y_ref, ns_ref = refs[-2:]
    past = st_ref.shape[0]
    acc = st_ref[0] * w_ref[0:1, :]
    for k in range(1, past):
        acc = acc + st_ref[k] * w_ref[k:k + 1, :]
        ns_ref[k - 1] = st_ref[k]
    g = glu_ref[...]
    ns_ref[past - 1] = g
    y_ref[...] = acc + g * w_ref[past:past + 1, :] + b_ref[...]


def _a_sample_conv(state_t, glu, w_dw, b_dw, prev, j, *, tn):
    n_layers, past, nb, d = state_t.shape
    taps = w_dw.shape[1]
    assert taps == past + 1 and glu.shape == (nb, d)
    in_specs = [
        pl.BlockSpec((None, past, nb, tn), lambda n: (j, 0, 0, n)),
        pl.BlockSpec((nb, tn), lambda n: (0, n)),
        pl.BlockSpec((None, taps, tn), lambda n: (j, 0, n)),
        pl.BlockSpec((None, 1, tn), lambda n: (j, 0, n)),
    ]
    args = [state_t, glu, w_dw, b_dw]
    aliases = {}
    if prev is not None:
        in_specs.append(pl.BlockSpec(memory_space=pl.ANY))
        args.append(prev)
        aliases = {4: 1}
    return pl.pallas_call(
        _a_sconv_kernel,
        grid=(d // tn,),
        in_specs=in_specs,
        out_specs=[
            pl.BlockSpec((nb, tn), lambda n: (0, n)),
            pl.BlockSpec((None, past, nb, tn), lambda n: (j, 0, 0, n)),
        ],
        out_shape=[
            jax.ShapeDtypeStruct((nb, d), F32),
            jax.ShapeDtypeStruct((n_layers, past, nb, d), F32),
        ],
        input_output_aliases=aliases,
        compiler_params=_params(("parallel",)),
        name=f"mix_a_sconv_{j}",
    )(*args)


def _a2s_kernel(x_ref, y_ref, lg_ref, lb_ref, w2_ref, o_ref, act_ref):
    @pl.when(pl.program_id(0) == 0)
    def _():
        act_ref[...] = _silu(_layernorm(y_ref[...], lg_ref[...], lb_ref[...])).astype(BF16)

    o_ref[...] = x_ref[...] + _dot(act_ref[...], w2_ref[...].astype(BF16))


def _a2_sample(x, y, ln_g, ln_b, w2, j, *, n_prompt, tn):
    m, d = x.shape
    ns = y.shape[0]
    rb = n_prompt // ns
    return pl.pallas_call(
        _a2s_kernel,
        grid=(d // tn,),
        in_specs=[
            pl.BlockSpec((ns, tn), lambda n: (rb, n)),
            pl.BlockSpec((ns, d), lambda n: (0, 0)),
            pl.BlockSpec((None, 1, d), lambda n: (j, 0, 0)),
            pl.BlockSpec((None, 1, d), lambda n: (j, 0, 0)),
            pl.BlockSpec((None, d, tn), lambda n: (j, 0, n)),
        ],
        out_specs=pl.BlockSpec((ns, tn), lambda n: (rb, n)),
        out_shape=jax.ShapeDtypeStruct((m, d), F32),
        scratch_shapes=[pltpu.VMEM((ns, d), BF16)],
        input_output_aliases={0: 0},
        compiler_params=_params(("arbitrary",)),
        name=f"mix_a2s_{j}",
    )(x, y, ln_g, ln_b, w2)


def _b1_kernel(x_ref, g_ref, w_ref, b_ref, lg_ref, lb_ref, o_ref, h_ref):
    n = pl.program_id(1)
    tn = w_ref.shape[1]

    @pl.when(n == 0)
    def _():
        _rms_rows(x_ref, g_ref, h_ref)

    z = _dot(h_ref[...], w_ref[...].astype(BF16)) + b_ref[...]
    o_ref[:, pl.ds(pl.multiple_of(n * tn, tn), tn)] = _gelu(z)

    @pl.when(n == pl.num_programs(1) - 1)
    def _():
        rows, step = o_ref.shape[0], 32
        lg, lb = lg_ref[...], lb_ref[...]

        def body(c, carry):
            r0 = pl.multiple_of(c * step, step)
            o_ref[pl.ds(r0, step), :] = _layernorm(o_ref[pl.ds(r0, step), :], lg, lb)
            return carry

        lax.fori_loop(0, rows // step, body, 0)


def _b1(x, g3, w_in, b_in, ln_g, ln_b, layer, j, *, tm, tn):
    m, d = x.shape
    d_sgu = w_in.shape[-1] // 2
    nb = d_sgu // tn
    return pl.pallas_call(
        _b1_kernel,
        grid=(m // tm, nb),
        in_specs=[
            pl.BlockSpec((tm, d), lambda i, n: (i, 0)),
            pl.BlockSpec((None, 1, d), lambda i, n: (layer, 0, 0)),
            pl.BlockSpec((None, d, tn), lambda i, n: (j, 0, n + nb)),
            pl.BlockSpec((None, 1, tn), lambda i, n: (j, 0, n + nb)),
            pl.BlockSpec((None, 1, d_sgu), lambda i, n: (j, 0, 0)),
            pl.BlockSpec((None, 1, d_sgu), lambda i, n: (j, 0, 0)),
        ],
        out_specs=pl.BlockSpec((tm, d_sgu), lambda i, n: (i, 0)),
        out_shape=jax.ShapeDtypeStruct((m, d_sgu), F32),
        scratch_shapes=[pltpu.VMEM((tm, d), BF16)],
        compiler_params=_params(("parallel", "arbitrary")),
        name=f"mix_b1_{j}",
    )(x, g3, w_in, b_in, ln_g, ln_b)


def _b2_kernel(x_ref, g_ref, v_ref, wi_ref, bi_ref, ws_ref, bs_ref, wo_ref, o_ref, h_ref,
               *, n_prompt_chunks):
    i = pl.program_id(0)
    tm = x_ref.shape[0]

    @pl.when(pl.program_id(1) == 0)
    def _():
        _rms_rows(x_ref, g_ref, h_ref, copy_ref=o_ref)

    u = _gelu(_dot(h_ref[...], wi_ref[...].astype(BF16)) + bi_ref[...])
    ws = ws_ref[...]
    row = lax.broadcasted_iota(jnp.int32, ws.shape, 0)
    col = lax.broadcasted_iota(jnp.int32, ws.shape, 1)
    w_tril = jnp.where(col <= row, ws, 0.0).astype(BF16)
    bs = bs_ref[...]
    parts = []
    for c in range(tm // CHUNK):
        rows = slice(c * CHUNK, (c + 1) * CHUNK)
        v = v_ref[rows, :]
        s_prompt = _dot(w_tril, v.astype(BF16)) + bs
        s_sample = ws[0:1, 0:1] * v + bs[0:1, :]
        is_sample = i * (tm // CHUNK) + c >= n_prompt_chunks
        parts.append(u[rows, :] * jnp.where(is_sample, s_sample, s_prompt))
    y = jnp.concatenate(parts, axis=0).astype(BF16)
    o_ref[...] += _dot(y, wo_ref[...].astype(BF16))


def _b2(x, g3, vn, w_in, b_in, w_s, b_s3, w_out, layer, j, *, n_prompt, tm):
    m, d = x.shape
    d_sgu = vn.shape[1]
    gw = d_sgu // N_SGU_GROUPS
    kern = functools.partial(_b2_kernel, n_prompt_chunks=n_prompt // CHUNK)
    return pl.pallas_call(
        kern,
        grid=(m // tm, N_SGU_GROUPS),
        in_specs=[
            pl.BlockSpec((tm, d), lambda i, g: (i, 0)),
            pl.BlockSpec((None, 1, d), lambda i, g: (layer, 0, 0)),
            pl.BlockSpec((tm, gw), lambda i, g: (i, g)),
            pl.BlockSpec((None, d, gw), lambda i, g: (j, 0, g)),
            pl.BlockSpec((None, 1, gw), lambda i, g: (j, 0, g)),
            pl.BlockSpec((None, None, CHUNK, CHUNK), lambda i, g: (j, g, 0, 0)),
            pl.BlockSpec((None, None, CHUNK, 1), lambda i, g: (j, g, 0, 0)),
            pl.BlockSpec((None, gw, d), lambda i, g: (j, g, 0)),
        ],
        out_specs=pl.BlockSpec((tm, d), lambda i, g: (i, 0)),
        out_shape=jax.ShapeDtypeStruct((m, d), F32),
        scratch_shapes=[pltpu.VMEM((tm, d), BF16)],
        compiler_params=_params(("parallel", "arbitrary")),
        name=f"mix_b2_{j}",
    )(x, g3, vn, w_in, b_in, w_s, b_s3, w_out)


def _c1_kernel(x_ref, g_ref, wb_ref, wc_ref, wx_ref, bg_ref, cx_ref, h_ref):
    @pl.when(pl.program_id(1) == 0)
    def _():
        _rms_rows(x_ref, g_ref, h_ref)

    h = h_ref[...]
    bg_ref[...] = _dot(h, wb_ref[...].astype(BF16))
    cx_ref[...] = _dot(h, wc_ref[...].astype(BF16)) * _dot(h, wx_ref[...].astype(BF16))


def _c1(x, g3, w_in, layer, j, *, row0, n_rows, tm, tn):
    d = x.shape[1]
    d_c = w_in.shape[-1] // 3
    nb = d_c // tn
    rb0 = row0 // tm
    out = jax.ShapeDtypeStruct((n_rows, d_c), F32)
    return pl.pallas_call(
        _c1_kernel,
        grid=(n_rows // tm, nb),
        in_specs=[
            pl.BlockSpec((tm, d), lambda i, n: (rb0 + i, 0)),
            pl.BlockSpec((None, 1, d), lambda i, n: (layer, 0, 0)),
            pl.BlockSpec((None, d, tn), lambda i, n: (j, 0, n)),
            pl.BlockSpec((None, d, tn), lambda i, n: (j, 0, n + nb)),
            pl.BlockSpec((None, d, tn), lambda i, n: (j, 0, n + 2 * nb)),
        ],
        out_specs=[pl.BlockSpec((tm, tn), lambda i, n: (i, n))] * 2,
        out_shape=[out, out],
        scratch_shapes=[pltpu.VMEM((tm, d), BF16)],
        compiler_params=_params(("parallel", "arbitrary")),
        name=f"mix_c1s_{j}",
    )(x, g3, w_in, w_in, w_in)


def _cf_kernel(x_ref, g_ref, wb_ref, wc_ref, wx_ref, wcv_ref, wo_ref, o_ref, st_ref,
               h_ref, gbuf_ref, wbuf_ref, hal_ref, act_ref, *, taps, tiles_per_seq, nb, rc):
    i, n = pl.program_id(0), pl.program_id(1)
    tm = x_ref.shape[0]
    tn = wb_ref.shape[1]
    halo = hal_ref.shape[0]
    cb = 128

    @pl.when(n == 0)
    def _():
        _start_tile(x_ref, g_ref, h_ref, hal_ref, i % tiles_per_seq == 0)

    @pl.when(n < nb)
    def _():
        col0 = pl.multiple_of(n * tn, tn)
        gbuf_ref[0:halo, :] = hal_ref[:, pl.ds(col0, tn)]
        wbuf_ref[0:taps, :] = wcv_ref[:, pl.ds(col0, tn)]
        wb = wb_ref[...].astype(BF16)
        wc = wc_ref[...].astype(BF16)
        wx = wx_ref[...].astype(BF16)
        for r in range(tm // rc):
            hc = h_ref[r * rc:(r + 1) * rc, :]
            bg = _dot(hc, wb)
            gbuf_ref[halo + r * rc:halo + (r + 1) * rc, :] = _dot(hc, wc) * _dot(hc, wx)
            for sub in range(rc // cb):
                r0 = r * rc + sub * cb
                for strip in range(tn // LANES):
                    cols = slice(strip * LANES, (strip + 1) * LANES)
                    y = _dwconv_block(gbuf_ref, wbuf_ref, r0, cb, cols, taps, halo)
                    gate = bg[sub * cb:(sub + 1) * cb, cols]
                    act_ref[r0:r0 + cb, pl.ds(pl.multiple_of(col0 + strip * LANES, LANES), LANES)] = (
                        gate * y).astype(BF16)
        hal_ref[:, pl.ds(col0, tn)] = gbuf_ref[tm:tm + halo, :]

    @pl.when(n >= nb)
    def _():
        col0 = pl.multiple_of((n - nb) * tn, tn)
        o_ref[...] = x_ref[:, pl.ds(col0, tn)] + _dot(act_ref[...], wo_ref[...].astype(BF16))

    @pl.when((n == 2 * nb - 1) & (i % tiles_per_seq == tiles_per_seq - 1))
    def _():
        st_ref[...] = hal_ref[halo - (taps - 1):halo, :]


def _c_fused(x, g3, w_in, w_conv, w_out, layer, j, *, n_prompt, seq, tm, tn):
    m, d = x.shape
    taps = w_conv.shape[1]
    nb = d // tn
    halo = SUBLANES
    assert w_in.shape[-1] == 3 * d and taps - 1 <= halo and seq % tm == 0
    kern = functools.partial(_cf_kernel, taps=taps, tiles_per_seq=seq // tm, nb=nb, rc=256)
    taps_pad = -(-taps // SUBLANES) * SUBLANES
    blk = lambda part: (lambda i, n: (j, 0, jnp.minimum(n, nb - 1) + part * nb))
    return pl.pallas_call(
        kern,
        grid=(n_prompt // tm, 2 * nb),
        in_specs=[
            pl.BlockSpec((tm, d), lambda i, n: (i, 0)),
            pl.BlockSpec((None, 1, d), lambda i, n: (layer, 0, 0)),
            pl.BlockSpec((None, d, tn), blk(0)),
            pl.BlockSpec((None, d, tn), blk(1)),
            pl.BlockSpec((None, d, tn), blk(2)),
            pl.BlockSpec((None, taps, d), lambda i, n: (j, 0, 0)),
            pl.BlockSpec((None, d, tn), lambda i, n: (j, 0, jnp.maximum(n - nb, 0))),
        ],
        out_specs=[
            pl.BlockSpec((tm, tn), lambda i, n: (i, jnp.maximum(n - nb, 0))),
            pl.BlockSpec((None, taps - 1, d), lambda i, n: (i // (seq // tm), 0, 0)),
        ],
        out_shape=[
            jax.ShapeDtypeStruct((m, d), F32),
            jax.ShapeDtypeStruct((n_prompt // seq, taps - 1, d), F32),
        ],
        scratch_shapes=[
            pltpu.VMEM((tm, d), BF16),
            pltpu.VMEM((halo + tm, tn), F32),
            pltpu.VMEM((taps_pad, tn), F32),
            pltpu.VMEM((halo, d), F32),
            pltpu.VMEM((tm, d), BF16),
        ],
        input_output_aliases={0: 0},
        compiler_params=_params(("arbitrary", "arbitrary")),
        name=f"mix_c_{j}",
    )(x, g3, w_in, w_in, w_in, w_conv, w_out)


def _c2s_kernel(x_ref, bg_ref, cx_ref, s0_ref, s1_ref, w_ref, wo_ref, o_ref, act_ref):
    @pl.when(pl.program_id(0) == 0)
    def _():
        w = w_ref[...]
        y = s0_ref[...] * w[0:1] + s1_ref[...] * w[1:2] + cx_ref[...] * w[2:3]
        act_ref[...] = (bg_ref[...] * y).astype(BF16)

    o_ref[...] = x_ref[...] + _dot(act_ref[...], wo_ref[...].astype(BF16))


def _c2_sample(x, bg, cx, s0, s1, w_conv, w_out, j, *, n_prompt, tn):
    m, d = x.shape
    ns = s0.shape[0]
    rb = n_prompt // ns
    taps = w_conv.shape[1]
    assert taps == 3
    return pl.pallas_call(
        _c2s_kernel,
        grid=(d // tn,),
        in_specs=[
            pl.BlockSpec((ns, tn), lambda n: (rb, n)),
            pl.BlockSpec((ns, d), lambda n: (0, 0)),
            pl.BlockSpec((ns, d), lambda n: (0, 0)),
            pl.BlockSpec((ns, d), lambda n: (0, 0)),
            pl.BlockSpec((ns, d), lambda n: (0, 0)),
            pl.BlockSpec((None, taps, d), lambda n: (j, 0, 0)),
            pl.BlockSpec((None, d, tn), lambda n: (j, 0, n)),
        ],
        out_specs=pl.BlockSpec((ns, tn), lambda n: (rb, n)),
        out_shape=jax.ShapeDtypeStruct((m, d), F32),
        scratch_shapes=[pltpu.VMEM((ns, d), BF16)],
        input_output_aliases={0: 0},
        compiler_params=_params(("arbitrary",)),
        name=f"mix_c2s_{j}",
    )(x, bg, cx, s0, s1, w_conv, w_out)


def _last_rows(a, batch, seq, rows):
    return jnp.stack([a[(b + 1) * seq - rows:(b + 1) * seq] for b in range(batch)])


def kernel(x_prompt, x_sample, state_conv_a, state_conv_c, g_ffn1, g_mix, g_ffn2, g_final,
           w_ffn_gate, w_ffn_up, w_ffn_down,
           a_w_pw1, a_b_pw1, a_w_dw, a_b_dw, a_ln_g, a_ln_b, a_w_pw2,
           b_w_in, b_b_in, b_ln_g, b_ln_b, b_w_s, b_b_s, b_w_out,
           c_w_in, c_w_conv, c_w_out):
    batch, seq, d = x_prompt.shape
    n_sample = x_sample.shape[0]
    assert x_sample.shape[1] == 1
    n_prompt = batch * seq
    depth = g_ffn1.shape[0]
    past_c = state_conv_c.shape[2]
    assert past_c == 2
    d_sgu = b_ln_g.shape[-1]

    row3 = lambda a: a.reshape(a.shape[0], 1, a.shape[1])
    g1, gm, g2 = row3(g_ffn1), row3(g_mix), row3(g_ffn2)
    gf = g_final.reshape(1, d)
    a_b1, a_bd, a_lg, a_lb = row3(a_b_pw1), row3(a_b_dw), row3(a_ln_g), row3(a_ln_b)
    b_bi, b_lg, b_lb = row3(b_b_in), row3(b_ln_g), row3(b_ln_b)
    b_bs = b_b_s.reshape(*b_b_s.shape, 1)
    state_a_t = jnp.transpose(state_conv_a, (0, 2, 1, 3))

    ffn = functools.partial(_ffn, n_prompt=n_prompt, n_sample=n_sample,
                            tm=832, tf_head=256, tf_tail=512)
    new_a_p, new_b_p, new_b_s, new_c_p, new_c_s = [], [], [], [], []
    new_a_s_t = None

    x = x_prompt.reshape(n_prompt, d)
    xs = x_sample.reshape(n_sample, d)
    for i in range(depth):
        x = ffn(x, xs, g1, gf, w_ffn_gate, w_ffn_up, w_ffn_down, i, 0, first=(i == 0), last=False)
        kind, j = i % 3, i // 3
        if kind == 0:
            glu_s = _a1(x, gm, a_w_pw1, a_b1, i, j, row0=n_prompt, n_rows=n_sample,
                        tm=n_sample, tn=256)
            y_s, new_a_s_t = _a_sample_conv(state_a_t, glu_s, a_w_dw, a_bd, new_a_s_t, j, tn=256)
            x, st_p = _a_fused(x, gm, a_w_pw1, a_b1, a_w_dw, a_bd, a_lg, a_lb, a_w_pw2, i, j,
                               n_prompt=n_prompt, seq=seq, tm=1024, tn=256)
            x = _a2_sample(x, y_s, a_lg, a_lb, a_w_pw2, j, n_prompt=n_prompt, tn=256)
            new_a_p.append(st_p)
        elif kind == 1:
            vn = _b1(x, gm, b_w_in, b_bi, b_lg, b_lb, i, j, tm=832, tn=256)
            x = _b2(x, gm, vn, b_w_in, b_bi, b_w_s, b_bs, b_w_out, i, j, n_prompt=n_prompt, tm=640)
            new_b_p.append(_last_rows(vn, batch, seq, CHUNK))
            new_b_s.append(vn[n_prompt:].reshape(n_sample, 1, d_sgu))
        else:
            bg_s, cx_s = _c1(x, gm, c_w_in, i, j, row0=n_prompt, n_rows=n_sample,
                             tm=n_sample, tn=256)
            s0, s1 = state_conv_c[j, :, 0], state_conv_c[j, :, 1]
            x, st_p = _c_fused(x, gm, c_w_in, c_w_conv, c_w_out, i, j,
                               n_prompt=n_prompt, seq=seq, tm=1024, tn=256)
            x = _c2_sample(x, bg_s, cx_s, s0, s1, c_w_conv, c_w_out, j, n_prompt=n_prompt, tn=256)
            new_c_p.append(st_p)
            new_c_s.append(jnp.stack([s1, cx_s], axis=1))
        last = i == depth - 1
        x = ffn(x, xs, g2, gf, w_ffn_gate, w_ffn_up, w_ffn_down, i, 1, first=False, last=last)

    return (x[:n_prompt].reshape(batch, seq, d), x[n_prompt:].reshape(n_sample, 1, d),
            jnp.stack(new_a_p), jnp.transpose(new_a_s_t, (0, 2, 1, 3)),
            jnp.stack(new_b_p), jnp.stack(new_b_s),
            jnp.stack(new_c_p), jnp.stack(new_c_s))
```

```python
import functools

import jax
import jax.numpy as jnp
from jax import lax
from jax.experimental import pallas as pl
from jax.experimental.pallas import tpu as pltpu

F32 = jnp.float32
BF16 = jnp.bfloat16

EPS = 1e-6
FFN_HALF = 0.5
CHUNK = 128
N_SGU_GROUPS = 8
INV_SQRT2 = 0.7071067811865476
SUBLANES = 8
LANES = 128

V7X_VMEM_BYTES = 64 * 1024 * 1024
VMEM_LIMIT = V7X_VMEM_BYTES - 6 * 1024 * 1024


def _params(sem):
    return pltpu.CompilerParams(dimension_semantics=sem, vmem_limit_bytes=VMEM_LIMIT)


def _rms(x, g):
    ms = jnp.mean(x * x, axis=-1, keepdims=True)
    return (x * lax.rsqrt(ms + EPS)) * g


def _layernorm(y, g, b):
    mu = jnp.mean(y, axis=-1, keepdims=True)
    yc = y - mu
    var = jnp.mean(yc * yc, axis=-1, keepdims=True)
    return (yc * lax.rsqrt(var + EPS)) * g + b


def _silu(x):
    return x * jax.nn.sigmoid(x)


def _gelu(x):
    return 0.5 * x * (1.0 + lax.erf(x * INV_SQRT2))


def _dot(a, w):
    return jnp.dot(a, w, preferred_element_type=F32)


def _rms_rows(x_ref, g_ref, h_ref, copy_ref=None, rows_per_step=64, tail=None):
    rows = x_ref.shape[0]
    step = rows_per_step
    assert rows % step == 0
    g = g_ref[...]

    def body(c, carry):
        r0 = pl.multiple_of(c * step, step)
        x = x_ref[pl.ds(r0, step), :]
        if tail is not None:
            use_tail, first_chunk, tail_ref = tail
            ct = jnp.clip(c - first_chunk, 0, tail_ref.shape[0] // step - 1)
            xt = tail_ref[pl.ds(pl.multiple_of(ct * step, step), step), :]
            x = jnp.where(use_tail & (c >= first_chunk), xt, x)
        h_ref[pl.ds(r0, step), :] = _rms(x, g).astype(BF16)
        if copy_ref is not None:
            copy_ref[pl.ds(r0, step), :] = x
        return carry

    lax.fori_loop(0, rows // step, body, 0)


def _rms_rows_inplace(o_ref, g_ref, rows_per_step=64):
    rows, step = o_ref.shape[0], rows_per_step
    g = g_ref[...]

    def body(c, carry):
        r0 = pl.multiple_of(c * step, step)
        o_ref[pl.ds(r0, step), :] = _rms(o_ref[pl.ds(r0, step), :], g)
        return carry

    lax.fori_loop(0, rows // step, body, 0)


def _ffn_step(h_ref, wg, wu, wd, o_ref):
    h = h_ref[...]
    gate = _dot(h, wg)
    up = _dot(h, wu)
    act = (_silu(gate) * (up * FFN_HALF)).astype(BF16)
    o_ref[...] += _dot(act, wd)


def _ffn_head_kernel(x_ref, g_ref, gf_ref, wg_ref, wu_ref, wd_ref,
                     o_ref, wg16_ref, wu16_ref, wd16_ref, h_ref, *, final_norm):
    f = pl.program_id(0)

    @pl.when(f == 0)
    def _():
        _rms_rows(x_ref, g_ref, h_ref, copy_ref=o_ref)

    wg = wg_ref[...].astype(BF16)
    wu = wu_ref[...].astype(BF16)
    wd = wd_ref[...].astype(BF16)
    wg16_ref[...] = wg
    wu16_ref[...] = wu
    wd16_ref[...] = wd
    _ffn_step(h_ref, wg, wu, wd, o_ref)

    if final_norm:
        @pl.when(f == pl.num_programs(0) - 1)
        def _():
            _rms_rows_inplace(o_ref, gf_ref)


def _ffn_tail_kernel(*refs, first, last, split_row, n_casts):
    refs = list(refs)
    x_ref, g_ref, gf_ref, wg_ref, wu_ref, wd_ref, _ = refs[:7]
    xs_ref = refs[7] if first else None
    n_in = 8 if first else 7
    cast_src = refs[n_in:n_in + n_casts]
    o_ref = refs[n_in + n_casts]
    cast_dst = refs[n_in + n_casts + 1:n_in + 2 * n_casts + 1]
    h_ref = refs[-1]
    i, f = pl.program_id(0), pl.program_id(1)
    last_tile = i == pl.num_programs(0) - 1

    @pl.when(f == 0)
    def _():
        tail = (last_tile, split_row // 64, xs_ref) if first else None
        _rms_rows(x_ref, g_ref, h_ref, copy_ref=o_ref, tail=tail)

    for src, dst in zip(cast_src, cast_dst):
        tn = dst.shape[2]
        for c in range(dst.shape[0]):
            dst[c] = src[:, c * tn:(c + 1) * tn].astype(BF16)

    _ffn_step(h_ref, wg_ref[...], wu_ref[...], wd_ref[...], o_ref)

    if last:
        @pl.when(f == pl.num_programs(1) - 1)
        def _():
            _rms_rows_inplace(o_ref, gf_ref)


def _ffn(x, xs, g3, gf2, w_gate, w_up, w_down, layer, which, casts, *, n_prompt, n_sample,
         first, last, tm, tf_head, tf_tail):
    d = x.shape[1]
    dff = w_gate.shape[-1]
    m = n_prompt + n_sample
    assert m % tm == 0
    n_tiles = m // tm
    split_row = n_prompt - (n_tiles - 1) * tm
    assert 0 < split_row < tm and split_row % 64 == 0 and tm - split_row == n_sample
    out_rows = m
    name = f"ffn_{layer}_{which}"

    head_out, wg16, wu16, wd16 = pl.pallas_call(
        functools.partial(_ffn_head_kernel, final_norm=last),
        grid=(dff // tf_head,),
        in_specs=[
            pl.BlockSpec((tm, d), lambda f: (0, 0)),
            pl.BlockSpec((None, 1, d), lambda f: (layer, 0, 0)),
            pl.BlockSpec((1, d), lambda f: (0, 0)),
            pl.BlockSpec((None, None, d, tf_head), lambda f: (layer, which, 0, f)),
            pl.BlockSpec((None, None, d, tf_head), lambda f: (layer, which, 0, f)),
            pl.BlockSpec((None, None, tf_head, d), lambda f: (layer, which, f, 0)),
        ],
        out_specs=[
            pl.BlockSpec((tm, d), lambda f: (0, 0)),
            pl.BlockSpec((d, tf_head), lambda f: (0, f)),
            pl.BlockSpec((d, tf_head), lambda f: (0, f)),
            pl.BlockSpec((tf_head, d), lambda f: (f, 0)),
        ],
        out_shape=[
            jax.ShapeDtypeStruct((out_rows, d), F32),
            jax.ShapeDtypeStruct((d, dff), BF16),
            jax.ShapeDtypeStruct((d, dff), BF16),
            jax.ShapeDtypeStruct((dff, d), BF16),
        ],
        scratch_shapes=[pltpu.VMEM((tm, d), BF16)],
        compiler_params=_params(("arbitrary",)),
        name=name + "_head",
    )(x, g3, gf2, w_gate, w_up, w_down)

    in_specs = [
        pl.BlockSpec((tm, d), lambda i, f: (i + 1, 0)),
        pl.BlockSpec((None, 1, d), lambda i, f: (layer, 0, 0)),
        pl.BlockSpec((1, d), lambda i, f: (0, 0)),
        pl.BlockSpec((d, tf_tail), lambda i, f: (0, f)),
        pl.BlockSpec((d, tf_tail), lambda i, f: (0, f)),
        pl.BlockSpec((tf_tail, d), lambda i, f: (f, 0)),
        pl.BlockSpec(memory_space=pl.ANY),
    ]
    args = [x, g3, gf2, wg16, wu16, wd16, head_out]
    if first:
        in_specs.append(pl.BlockSpec((n_sample, d), lambda i, f: (0, 0)))
        args.append(xs)
    nf = dff // tf_tail
    out_specs = [pl.BlockSpec((tm, d), lambda i, f: (i + 1, 0))]
    out_shape = [jax.ShapeDtypeStruct((out_rows, d), F32)]
    for w, lj, rb, tn in casts:
        _, rows, cols = w.shape
        n_blk = rows // rb
        assert rows % rb == 0 and cols % tn == 0 and n_blk <= (n_tiles - 1) * nf
        blk = lambda i, f, n_blk=n_blk: jnp.minimum(i * nf + f, n_blk - 1)
        in_specs.append(pl.BlockSpec((None, rb, cols), lambda i, f, lj=lj, blk=blk: (lj, blk(i, f), 0)))
        args.append(w)
        out_specs.append(pl.BlockSpec((cols // tn, rb, tn), lambda i, f, blk=blk: (0, blk(i, f), 0)))
        out_shape.append(jax.ShapeDtypeStruct((cols // tn, rows, tn), BF16))
    outs = pl.pallas_call(
        functools.partial(_ffn_tail_kernel, first=first, last=last, split_row=split_row,
                          n_casts=len(casts)),
        grid=(n_tiles - 1, nf),
        in_specs=in_specs,
        out_specs=out_specs,
        out_shape=out_shape,
        scratch_shapes=[pltpu.VMEM((tm, d), BF16)],
        input_output_aliases={6: 0},
        compiler_params=_params(("arbitrary", "arbitrary")),
        name=name + "_tail",
    )(*args)
    return outs[0], list(outs[1:])


def _a1_kernel(x_ref, g_ref, wa_ref, wg_ref, ba_ref, bg_ref, o_ref, h_ref):
    @pl.when(pl.program_id(1) == 0)
    def _():
        _rms_rows(x_ref, g_ref, h_ref)

    h = h_ref[...]
    a = _dot(h, wa_ref[...]) + ba_ref[...]
    gt = _dot(h, wg_ref[...]) + bg_ref[...]
    o_ref[...] = a * jax.nn.sigmoid(gt)


def _a1(x, g3, w1, b1, layer, j, *, row0, n_rows, tm):
    d = x.shape[1]
    nb, tn = w1.shape[0] // 2, w1.shape[2]
    d_a = nb * tn
    rb0 = row0 // tm
    return pl.pallas_call(
        _a1_kernel,
        grid=(n_rows // tm, nb),
        in_specs=[
            pl.BlockSpec((tm, d), lambda i, n: (rb0 + i, 0)),
            pl.BlockSpec((None, 1, d), lambda i, n: (layer, 0, 0)),
            pl.BlockSpec((None, d, tn), lambda i, n: (n, 0, 0)),
            pl.BlockSpec((None, d, tn), lambda i, n: (n + nb, 0, 0)),
            pl.BlockSpec((None, 1, tn), lambda i, n: (j, 0, n)),
            pl.BlockSpec((None, 1, tn), lambda i, n: (j, 0, n + nb)),
        ],
        out_specs=pl.BlockSpec((tm, tn), lambda i, n: (i, n)),
        out_shape=jax.ShapeDtypeStruct((n_rows, d_a), F32),
        scratch_shapes=[pltpu.VMEM((tm, d), BF16)],
        compiler_params=_params(("parallel", "arbitrary")),
        name=f"mix_a1s_{j}",
    )(x, g3, w1, w1, b1, b1)


def _dwconv_block(xp_ref, w_ref, r0, rows, cols, taps, halo):
    lead = halo - (taps - 1)
    assert halo % SUBLANES == 0 and lead >= 0
    acc = None
    for s in range(SUBLANES):
        group = [k for k in range(taps) if (k + lead) % SUBLANES == s]
        if not group:
            continue
        ext = rows + (SUBLANES if s else 0)
        part = None
        for k in group:
            base = r0 + ((k + lead) // SUBLANES) * SUBLANES
            term = xp_ref[pl.ds(base, ext), cols] * w_ref[k:k + 1, cols]
            part = term if part is None else part + term
        if s:
            part = part[s:s + rows]
        acc = part if acc is None else acc + part
    return acc


def _start_tile(x_ref, g_ref, h_ref, hal_ref, first_in_seq):
    _rms_rows(x_ref, g_ref, h_ref)

    @pl.when(first_in_seq)
    def _():
        hal_ref[...] = jnp.zeros(hal_ref.shape, hal_ref.dtype)


def _af_kernel(x_ref, g_ref, wa_ref, wg_ref, ba_ref, bgt_ref, wdw_ref, bdw_ref, lg_ref, lb_ref,
               w2_ref, o_ref, st_ref, h_ref, gbuf_ref, wbuf_ref, hal_ref, y_ref, act_ref,
               *, taps, tiles_per_seq, nb, rc):
    i, n = pl.program_id(0), pl.program_id(1)
    tm = x_ref.shape[0]
    tn = wa_ref.shape[1]
    halo = hal_ref.shape[0]
    cb = 128

    @pl.when(n == 0)
    def _():
        _start_tile(x_ref, g_ref, h_ref, hal_ref, i % tiles_per_seq == 0)

    @pl.when(n < nb)
    def _():
        col0 = pl.multiple_of(n * tn, tn)
        gbuf_ref[0:halo, :] = hal_ref[:, pl.ds(col0, tn)]
        wbuf_ref[0:taps, :] = wdw_ref[:, pl.ds(col0, tn)]
        wa, wg = wa_ref[...], wg_ref[...]
        ba, bgt = ba_ref[...], bgt_ref[...]
        for r in range(tm // rc):
            hc = h_ref[r * rc:(r + 1) * rc, :]
            glu = (_dot(hc, wa) + ba) * jax.nn.sigmoid(_dot(hc, wg) + bgt)
            gbuf_ref[halo + r * rc:halo + (r + 1) * rc, :] = glu
            for sub in range(rc // cb):
                r0 = r * rc + sub * cb
                for strip in range(tn // LANES):
                    cols = slice(strip * LANES, (strip + 1) * LANES)
                    y = _dwconv_block(gbuf_ref, wbuf_ref, r0, cb, cols, taps, halo)
                    y_ref[r0:r0 + cb, pl.ds(pl.multiple_of(col0 + strip * LANES, LANES), LANES)] = y
        hal_ref[:, pl.ds(col0, tn)] = gbuf_ref[tm:tm + halo, :]

    @pl.when(n == nb)
    def _():
        w2 = w2_ref[...]
        bias, lg, lb = bdw_ref[...], lg_ref[...], lb_ref[...]
        step = 32
        for r in range(tm // rc):
            for s in range(rc // step):
                r0 = r * rc + s * step
                y = y_ref[r0:r0 + step, :] + bias
                act_ref[r0:r0 + step, :] = _silu(_layernorm(y, lg, lb)).astype(BF16)
            rows = slice(r * rc, (r + 1) * rc)
            o_ref[rows, :] = x_ref[rows, 0:tn] + _dot(act_ref[rows, :], w2)

    @pl.when(n > nb)
    def _():
        col0 = pl.multiple_of((n - nb) * tn, tn)
        o_ref[...] = x_ref[:, pl.ds(col0, tn)] + _dot(act_ref[...], w2_ref[...])

    @pl.when((n == 2 * nb - 1) & (i % tiles_per_seq == tiles_per_seq - 1))
    def _():
        st_ref[...] = hal_ref[halo - (taps - 1):halo, :]


def _a_fused(x, g3, w1, b1, w_dw, b_dw, ln_g, ln_b, w2, layer, j, *, n_prompt, seq, tm):
    m, d = x.shape
    taps = w_dw.shape[1]
    nb, tn = w2.shape[0], w2.shape[2]
    halo = 32
    assert w1.shape == (2 * nb, d, tn) and nb * tn == d and taps - 1 <= halo and seq % tm == 0
    kern = functools.partial(_af_kernel, taps=taps, tiles_per_seq=seq // tm, nb=nb, rc=256)
    taps_pad = -(-taps // SUBLANES) * SUBLANES
    w1_blk = lambda i, n: (jnp.minimum(n, nb - 1), 0, 0)
    w1g_blk = lambda i, n: (jnp.minimum(n, nb - 1) + nb, 0, 0)
    b1_blk = lambda i, n: (j, 0, jnp.minimum(n, nb - 1))
    b1g_blk = lambda i, n: (j, 0, jnp.minimum(n, nb - 1) + nb)
    out_blk = lambda i, n: (i, jnp.maximum(n - nb, 0))
    row = lambda i, n: (j, 0, 0)
    return pl.pallas_call(
        kern,
        grid=(n_prompt // tm, 2 * nb),
        in_specs=[
            pl.BlockSpec((tm, d), lambda i, n: (i, 0)),
            pl.BlockSpec((None, 1, d), lambda i, n: (layer, 0, 0)),
            pl.BlockSpec((None, d, tn), w1_blk),
            pl.BlockSpec((None, d, tn), w1g_blk),
            pl.BlockSpec((None, 1, tn), b1_blk),
            pl.BlockSpec((None, 1, tn), b1g_blk),
            pl.BlockSpec((None, taps, d), row),
            pl.BlockSpec((None, 1, d), row),
            pl.BlockSpec((None, 1, d), row),
            pl.BlockSpec((None, 1, d), row),
            pl.BlockSpec((None, d, tn), lambda i, n: (jnp.maximum(n - nb, 0), 0, 0)),
        ],
        out_specs=[
            pl.BlockSpec((tm, tn), out_blk),
            pl.BlockSpec((None, taps - 1, d), lambda i, n: (i // (seq // tm), 0, 0)),
        ],
        out_shape=[
            jax.ShapeDtypeStruct((m, d), F32),
            jax.ShapeDtypeStruct((n_prompt // seq, taps - 1, d), F32),
        ],
        scratch_shapes=[
            pltpu.VMEM((tm, d), BF16),
            pltpu.VMEM((halo + tm, tn), F32),
            pltpu.VMEM((taps_pad, tn), F32),
            pltpu.VMEM((halo, d), F32),
            pltpu.VMEM((tm, d), F32),
            pltpu.VMEM((tm, d), BF16),
        ],
        input_output_aliases={0: 0},
        compiler_params=_params(("arbitrary", "arbitrary")),
        name=f"mix_a_{j}",
    )(x, g3, w1, w1, b1, b1, w_dw, b_dw, ln_g, ln_b, w2)


def _a_sconv_kernel(*refs):
    st_ref, glu_ref, w_ref, b_ref = refs[:4]
    y_ref, ns_ref = refs[-2:]
    past = st_ref.shape[0]
    acc = st_ref[0] * w_ref[0:1, :]
    for k in range(1, past):
        acc = acc + st_ref[k] * w_ref[k:k + 1, :]
        ns_ref[k - 1] = st_ref[k]
    g = glu_ref[...]
    ns_ref[past - 1] = g
    y_ref[...] = acc + g * w_ref[past:past + 1, :] + b_ref[...]


def _a_sample_conv(state_t, glu, w_dw, b_dw, prev, j, *, tn):
    n_layers, past, nb, d = state_t.shape
    taps = w_dw.shape[1]
    assert taps == past + 1 and glu.shape == (nb, d)
    in_specs = [
        pl.BlockSpec((None, past, nb, tn), lambda n: (j, 0, 0, n)),
        pl.BlockSpec((nb, tn), lambda n: (0, n)),
        pl.BlockSpec((None, taps, tn), lambda n: (j, 0, n)),
        pl.BlockSpec((None, 1, tn), lambda n: (j, 0, n)),
    ]
    args = [state_t, glu, w_dw, b_dw]
    aliases = {}
    if prev is not None:
        in_specs.append(pl.BlockSpec(memory_space=pl.ANY))
        args.append(prev)
        aliases = {4: 1}
    return pl.pallas_call(
        _a_sconv_kernel,
        grid=(d // tn,),
        in_specs=in_specs,
        out_specs=[
            pl.BlockSpec((nb, tn), lambda n: (0, n)),
            pl.BlockSpec((None, past, nb, tn), lambda n: (j, 0, 0, n)),
        ],
        out_shape=[
            jax.ShapeDtypeStruct((nb, d), F32),
            jax.ShapeDtypeStruct((n_layers, past, nb, d), F32),
        ],
        input_output_aliases=aliases,
        compiler_params=_params(("parallel",)),
        name=f"mix_a_sconv_{j}",
    )(*args)


def _a2s_kernel(x_ref, y_ref, lg_ref, lb_ref, w2_ref, o_ref, act_ref):
    @pl.when(pl.program_id(0) == 0)
    def _():
        act_ref[...] = _silu(_layernorm(y_ref[...], lg_ref[...], lb_ref[...])).astype(BF16)

    o_ref[...] = x_ref[...] + _dot(act_ref[...], w2_ref[...])


def _a2_sample(x, y, ln_g, ln_b, w2, j, *, n_prompt):
    m, d = x.shape
    ns = y.shape[0]
    tn = w2.shape[2]
    rb = n_prompt // ns
    return pl.pallas_call(
        _a2s_kernel,
        grid=(d // tn,),
        in_specs=[
            pl.BlockSpec((ns, tn), lambda n: (rb, n)),
            pl.BlockSpec((ns, d), lambda n: (0, 0)),
            pl.BlockSpec((None, 1, d), lambda n: (j, 0, 0)),
            pl.BlockSpec((None, 1, d), lambda n: (j, 0, 0)),
            pl.BlockSpec((None, d, tn), lambda n: (n, 0, 0)),
        ],
        out_specs=pl.BlockSpec((ns, tn), lambda n: (rb, n)),
        out_shape=jax.ShapeDtypeStruct((m, d), F32),
        scratch_shapes=[pltpu.VMEM((ns, d), BF16)],
        input_output_aliases={0: 0},
        compiler_params=_params(("arbitrary",)),
        name=f"mix_a2s_{j}",
    )(x, y, ln_g, ln_b, w2)


def _b1_kernel(x_ref, g_ref, w_ref, b_ref, lg_ref, lb_ref, o_ref, h_ref):
    n = pl.program_id(1)
    tn = w_ref.shape[1]

    @pl.when(n == 0)
    def _():
        _rms_rows(x_ref, g_ref, h_ref)

    z = _dot(h_ref[...], w_ref[...]) + b_ref[...]
    o_ref[:, pl.ds(pl.multiple_of(n * tn, tn), tn)] = _gelu(z)

    @pl.when(n == pl.num_programs(1) - 1)
    def _():
        rows, step = o_ref.shape[0], 32
        lg, lb = lg_ref[...], lb_ref[...]

        def body(c, carry):
            r0 = pl.multiple_of(c * step, step)
            o_ref[pl.ds(r0, step), :] = _layernorm(o_ref[pl.ds(r0, step), :], lg, lb)
            return carry

        lax.fori_loop(0, rows // step, body, 0)


def _b1(x, g3, w_in, b_in, ln_g, ln_b, layer, j, *, tm):
    m, d = x.shape
    nb, tn = w_in.shape[0] // 2, w_in.shape[2]
    d_sgu = nb * tn
    return pl.pallas_call(
        _b1_kernel,
        grid=(m // tm, nb),
        in_specs=[
            pl.BlockSpec((tm, d), lambda i, n: (i, 0)),
            pl.BlockSpec((None, 1, d), lambda i, n: (layer, 0, 0)),
            pl.BlockSpec((None, d, tn), lambda i, n: (n + nb, 0, 0)),
            pl.BlockSpec((None, 1, tn), lambda i, n: (j, 0, n + nb)),
            pl.BlockSpec((None, 1, d_sgu), lambda i, n: (j, 0, 0)),
            pl.BlockSpec((None, 1, d_sgu), lambda i, n: (j, 0, 0)),
        ],
        out_specs=pl.BlockSpec((tm, d_sgu), lambda i, n: (i, 0)),
        out_shape=jax.ShapeDtypeStruct((m, d_sgu), F32),
        scratch_shapes=[pltpu.VMEM((tm, d), BF16)],
        compiler_params=_params(("parallel", "arbitrary")),
        name=f"mix_b1_{j}",
    )(x, g3, w_in, b_in, ln_g, ln_b)


def _b2_kernel(x_ref, g_ref, v_ref, wi_ref, bi_ref, ws_ref, bs_ref, wo_ref, o_ref, h_ref,
               *, n_prompt_chunks):
    i = pl.program_id(0)
    tm = x_ref.shape[0]

    @pl.when(pl.program_id(1) == 0)
    def _():
        _rms_rows(x_ref, g_ref, h_ref, copy_ref=o_ref)

    u = _gelu(_dot(h_ref[...], wi_ref[...]) + bi_ref[...])
    ws = ws_ref[...]
    row = lax.broadcasted_iota(jnp.int32, ws.shape, 0)
    col = lax.broadcasted_iota(jnp.int32, ws.shape, 1)
    w_tril = jnp.where(col <= row, ws, 0.0).astype(BF16)
    bs = bs_ref[...]
    parts = []
    for c in range(tm // CHUNK):
        rows = slice(c * CHUNK, (c + 1) * CHUNK)
        v = v_ref[rows, :]
        s_prompt = _dot(w_tril, v.astype(BF16)) + bs
        s_sample = ws[0:1, 0:1] * v + bs[0:1, :]
        is_sample = i * (tm // CHUNK) + c >= n_prompt_chunks
        parts.append(u[rows, :] * jnp.where(is_sample, s_sample, s_prompt))
    y = jnp.concatenate(parts, axis=0).astype(BF16)
    o_ref[...] += _dot(y, wo_ref[...])


def _b2(x, g3, vn, w_in, b_in, w_s, b_s3, w_out, layer, j, *, n_prompt, tm):
    m, d = x.shape
    d_sgu = vn.shape[1]
    gw = d_sgu // N_SGU_GROUPS
    assert w_in.shape == (2 * N_SGU_GROUPS, d, gw) and w_out.shape == (1, d_sgu, d)
    kern = functools.partial(_b2_kernel, n_prompt_chunks=n_prompt // CHUNK)
    return pl.pallas_call(
        kern,
        grid=(m // tm, N_SGU_GROUPS),
        in_specs=[
            pl.BlockSpec((tm, d), lambda i, g: (i, 0)),
            pl.BlockSpec((None, 1, d), lambda i, g: (layer, 0, 0)),
            pl.BlockSpec((tm, gw), lambda i, g: (i, g)),
            pl.BlockSpec((None, d, gw), lambda i, g: (g, 0, 0)),
            pl.BlockSpec((None, 1, gw), lambda i, g: (j, 0, g)),
            pl.BlockSpec((None, None, CHUNK, CHUNK), lambda i, g: (j, g, 0, 0)),
            pl.BlockSpec((None, None, CHUNK, 1), lambda i, g: (j, g, 0, 0)),
            pl.BlockSpec((None, gw, d), lambda i, g: (0, g, 0)),
        ],
        out_specs=pl.BlockSpec((tm, d), lambda i, g: (i, 0)),
        out_shape=jax.ShapeDtypeStruct((m, d), F32),
        scratch_shapes=[pltpu.VMEM((tm, d), BF16)],
        compiler_params=_params(("parallel", "arbitrary")),
        name=f"mix_b2_{j}",
    )(x, g3, vn, w_in, b_in, w_s, b_s3, w_out)


def _c1_kernel(x_ref, g_ref, wb_ref, wc_ref, wx_ref, bg_ref, cx_ref, h_ref):
    @pl.when(pl.program_id(1) == 0)
    def _():
        _rms_rows(x_ref, g_ref, h_ref)

    h = h_ref[...]
    bg_ref[...] = _dot(h, wb_ref[...])
    cx_ref[...] = _dot(h, wc_ref[...]) * _dot(h, wx_ref[...])


def _c1(x, g3, w_in, layer, j, *, row0, n_rows, tm):
    d = x.shape[1]
    nb, tn = w_in.shape[0] // 3, w_in.shape[2]
    d_c = nb * tn
    rb0 = row0 // tm
    out = jax.ShapeDtypeStruct((n_rows, d_c), F32)
    return pl.pallas_call(
        _c1_kernel,
        grid=(n_rows // tm, nb),
        in_specs=[
            pl.BlockSpec((tm, d), lambda i, n: (rb0 + i, 0)),
            pl.BlockSpec((None, 1, d), lambda i, n: (layer, 0, 0)),
            pl.BlockSpec((None, d, tn), lambda i, n: (n, 0, 0)),
            pl.BlockSpec((None, d, tn), lambda i, n: (n + nb, 0, 0)),
            pl.BlockSpec((None, d, tn), lambda i, n: (n + 2 * nb, 0, 0)),
        ],
        out_specs=[pl.BlockSpec((tm, tn), lambda i, n: (i, n))] * 2,
        out_shape=[out, out],
        scratch_shapes=[pltpu.VMEM((tm, d), BF16)],
        compiler_params=_params(("parallel", "arbitrary")),
        name=f"mix_c1s_{j}",
    )(x, g3, w_in, w_in, w_in)


def _cf_kernel(x_ref, g_ref, wb_ref, wc_ref, wx_ref, wcv_ref, wo_ref, o_ref, st_ref,
               h_ref, gbuf_ref, wbuf_ref, hal_ref, act_ref, *, taps, tiles_per_seq, nb, rc):
    i, n = pl.program_id(0), pl.program_id(1)
    tm = x_ref.shape[0]
    tn = wb_ref.shape[1]
    halo = hal_ref.shape[0]
    cb = 128

    @pl.when(n == 0)
    def _():
        _start_tile(x_ref, g_ref, h_ref, hal_ref, i % tiles_per_seq == 0)

    @pl.when(n < nb)
    def _():
        col0 = pl.multiple_of(n * tn, tn)
        gbuf_ref[0:halo, :] = hal_ref[:, pl.ds(col0, tn)]
        wbuf_ref[0:taps, :] = wcv_ref[:, pl.ds(col0, tn)]
        wb, wc, wx = wb_ref[...], wc_ref[...], wx_ref[...]
        for r in range(tm // rc):
            hc = h_ref[r * rc:(r + 1) * rc, :]
            bg = _dot(hc, wb)
            gbuf_ref[halo + r * rc:halo + (r + 1) * rc, :] = _dot(hc, wc) * _dot(hc, wx)
            for sub in range(rc // cb):
                r0 = r * rc + sub * cb
                for strip in range(tn // LANES):
                    cols = slice(strip * LANES, (strip + 1) * LANES)
                    y = _dwconv_block(gbuf_ref, wbuf_ref, r0, cb, cols, taps, halo)
                    gate = bg[sub * cb:(sub + 1) * cb, cols]
                    act_ref[r0:r0 + cb, pl.ds(pl.multiple_of(col0 + strip * LANES, LANES), LANES)] = (
                        gate * y).astype(BF16)
        hal_ref[:, pl.ds(col0, tn)] = gbuf_ref[tm:tm + halo, :]

    @pl.when(n >= nb)
    def _():
        col0 = pl.multiple_of((n - nb) * tn, tn)
        o_ref[...] = x_ref[:, pl.ds(col0, tn)] + _dot(act_ref[...], wo_ref[...])

    @pl.when((n == 2 * nb - 1) & (i % tiles_per_seq == tiles_per_seq - 1))
    def _():
        st_ref[...] = hal_ref[halo - (taps - 1):halo, :]


def _c_fused(x, g3, w_in, w_conv, w_out, layer, j, *, n_prompt, seq, tm):
    m, d = x.shape
    taps = w_conv.shape[1]
    nb, tn = w_out.shape[0], w_out.shape[2]
    halo = SUBLANES
    assert w_in.shape == (3 * nb, d, tn) and nb * tn == d and taps - 1 <= halo and seq % tm == 0
    kern = functools.partial(_cf_kernel, taps=taps, tiles_per_seq=seq // tm, nb=nb, rc=256)
    taps_pad = -(-taps // SUBLANES) * SUBLANES
    blk = lambda part: (lambda i, n: (jnp.minimum(n, nb - 1) + part * nb, 0, 0))
    return pl.pallas_call(
        kern,
        grid=(n_prompt // tm, 2 * nb),
        in_specs=[
            pl.BlockSpec((tm, d), lambda i, n: (i, 0)),
            pl.BlockSpec((None, 1, d), lambda i, n: (layer, 0, 0)),
            pl.BlockSpec((None, d, tn), blk(0)),
            pl.BlockSpec((None, d, tn), blk(1)),
            pl.BlockSpec((None, d, tn), blk(2)),
            pl.BlockSpec((None, taps, d), lambda i, n: (j, 0, 0)),
            pl.BlockSpec((None, d, tn), lambda i, n: (jnp.maximum(n - nb, 0), 0, 0)),
        ],
        out_specs=[
            pl.BlockSpec((tm, tn), lambda i, n: (i, jnp.maximum(n - nb, 0))),
            pl.BlockSpec((None, taps - 1, d), lambda i, n: (i // (seq // tm), 0, 0)),
        ],
        out_shape=[
            jax.ShapeDtypeStruct((m, d), F32),
            jax.ShapeDtypeStruct((n_prompt // seq, taps - 1, d), F32),
        ],
        scratch_shapes=[
            pltpu.VMEM((tm, d), BF16),
            pltpu.VMEM((halo + tm, tn), F32),
            pltpu.VMEM((taps_pad, tn), F32),
            pltpu.VMEM((halo, d), F32),
            pltpu.VMEM((tm, d), BF16),
        ],
        input_output_aliases={0: 0},
        compiler_params=_params(("arbitrary", "arbitrary")),
        name=f"mix_c_{j}",
    )(x, g3, w_in, w_in, w_in, w_conv, w_out)


def _c2s_kernel(x_ref, bg_ref, cx_ref, s0_ref, s1_ref, w_ref, wo_ref, o_ref, act_ref):
    @pl.when(pl.program_id(0) == 0)
    def _():
        w = w_ref[...]
        y = s0_ref[...] * w[0:1] + s1_ref[...] * w[1:2] + cx_ref[...] * w[2:3]
        act_ref[...] = (bg_ref[...] * y).astype(BF16)

    o_ref[...] = x_ref[...] + _dot(act_ref[...], wo_ref[...])


def _c2_sample(x, bg, cx, s0, s1, w_conv, w_out, j, *, n_prompt):
    m, d = x.shape
    ns = s0.shape[0]
    tn = w_out.shape[2]
    rb = n_prompt // ns
    taps = w_conv.shape[1]
    assert taps == 3
    return pl.pallas_call(
        _c2s_kernel,
        grid=(d // tn,),
        in_specs=[
            pl.BlockSpec((ns, tn), lambda n: (rb, n)),
            pl.BlockSpec((ns, d), lambda n: (0, 0)),
            pl.BlockSpec((ns, d), lambda n: (0, 0)),
            pl.BlockSpec((ns, d), lambda n: (0, 0)),
            pl.BlockSpec((ns, d), lambda n: (0, 0)),
            pl.BlockSpec((None, taps, d), lambda n: (j, 0, 0)),
            pl.BlockSpec((None, d, tn), lambda n: (n, 0, 0)),
        ],
        out_specs=pl.BlockSpec((ns, tn), lambda n: (rb, n)),
        out_shape=jax.ShapeDtypeStruct((m, d), F32),
        scratch_shapes=[pltpu.VMEM((ns, d), BF16)],
        input_output_aliases={0: 0},
        compiler_params=_params(("arbitrary",)),
        name=f"mix_c2s_{j}",
    )(x, bg, cx, s0, s1, w_conv, w_out)


def _last_rows(a, batch, seq, rows):
    return jnp.stack([a[(b + 1) * seq - rows:(b + 1) * seq] for b in range(batch)])


def kernel(x_prompt, x_sample, state_conv_a, state_conv_c, g_ffn1, g_mix, g_ffn2, g_final,
           w_ffn_gate, w_ffn_up, w_ffn_down,
           a_w_pw1, a_b_pw1, a_w_dw, a_b_dw, a_ln_g, a_ln_b, a_w_pw2,
           b_w_in, b_b_in, b_ln_g, b_ln_b, b_w_s, b_b_s, b_w_out,
           c_w_in, c_w_conv, c_w_out):
    batch, seq, d = x_prompt.shape
    n_sample = x_sample.shape[0]
    assert x_sample.shape[1] == 1
    n_prompt = batch * seq
    depth = g_ffn1.shape[0]
    past_c = state_conv_c.shape[2]
    assert past_c == 2
    d_sgu = b_ln_g.shape[-1]

    row3 = lambda a: a.reshape(a.shape[0], 1, a.shape[1])
    g1, gm, g2 = row3(g_ffn1), row3(g_mix), row3(g_ffn2)
    gf = g_final.reshape(1, d)
    a_b1, a_bd, a_lg, a_lb = row3(a_b_pw1), row3(a_b_dw), row3(a_ln_g), row3(a_ln_b)
    b_bi, b_lg, b_lb = row3(b_b_in), row3(b_ln_g), row3(b_ln_b)
    b_bs = b_b_s.reshape(*b_b_s.shape, 1)
    state_a_t = jnp.transpose(state_conv_a, (0, 2, 1, 3))

    ffn = functools.partial(_ffn, n_prompt=n_prompt, n_sample=n_sample,
                            tm=832, tf_head=256, tf_tail=512)
    new_a_p, new_b_p, new_b_s, new_c_p, new_c_s = [], [], [], [], []
    new_a_s_t = None

    x = x_prompt.reshape(n_prompt, d)
    xs = x_sample.reshape(n_sample, d)
    gw = d_sgu // N_SGU_GROUPS
    for i in range(depth):
        kind, j = i % 3, i // 3
        if kind == 0:
            casts = [(a_w_pw1, j, 32, 256), (a_w_pw2, j, 32, 256)]
        elif kind == 1:
            casts = [(b_w_in, j, 32, gw), (b_w_out, j, 64, d)]
        else:
            casts = [(c_w_in, j, 32, 256), (c_w_out, j, 32, 256)]
        x, (w_a16, w_b16) = ffn(x, xs, g1, gf, w_ffn_gate, w_ffn_up, w_ffn_down, i, 0, casts,
                                first=(i == 0), last=False)
        if kind == 0:
            glu_s = _a1(x, gm, w_a16, a_b1, i, j, row0=n_prompt, n_rows=n_sample, tm=n_sample)
            y_s, new_a_s_t = _a_sample_conv(state_a_t, glu_s, a_w_dw, a_bd, new_a_s_t, j, tn=256)
            x, st_p = _a_fused(x, gm, w_a16, a_b1, a_w_dw, a_bd, a_lg, a_lb, w_b16, i, j,
                               n_prompt=n_prompt, seq=seq, tm=1024)
            x = _a2_sample(x, y_s, a_lg, a_lb, w_b16, j, n_prompt=n_prompt)
            new_a_p.append(st_p)
        elif kind == 1:
            vn = _b1(x, gm, w_a16, b_bi, b_lg, b_lb, i, j, tm=832)
            x = _b2(x, gm, vn, w_a16, b_bi, b_w_s, b_bs, w_b16, i, j, n_prompt=n_prompt, tm=640)
            new_b_p.append(_last_rows(vn, batch, seq, CHUNK))
            new_b_s.append(vn[n_prompt:].reshape(n_sample, 1, d_sgu))
        else:
            bg_s, cx_s = _c1(x, gm, w_a16, i, j, row0=n_prompt, n_rows=n_sample, tm=n_sample)
            s0, s1 = state_conv_c[j, :, 0], state_conv_c[j, :, 1]
            x, st_p = _c_fused(x, gm, w_a16, c_w_conv, w_b16, i, j,
                               n_prompt=n_prompt, seq=seq, tm=1024)
            x = _c2_sample(x, bg_s, cx_s, s0, s1, c_w_conv, w_b16, j, n_prompt=n_prompt)
            new_c_p.append(st_p)
            new_c_s.append(jnp.stack([s1, cx_s], axis=1))
        last = i == depth - 1
        x, _ = ffn(x, xs, g2, gf, w_ffn_gate, w_ffn_up, w_ffn_down, i, 1, [],
                   first=False, last=last)

    return (x[:n_prompt].reshape(batch, seq, d), x[n_prompt:].reshape(n_sample, 1, d),
            jnp.stack(new_a_p), jnp.transpose(new_a_s_t, (0, 2, 1, 3)),
            jnp.stack(new_b_p), jnp.stack(new_b_s),
            jnp.stack(new_c_p), jnp.stack(new_c_s))
```

```python
import functools

import jax
import jax.numpy as jnp
from jax import lax
from jax.experimental import pallas as pl
from jax.experimental.pallas import tpu as pltpu

F32 = jnp.float32
BF16 = jnp.bfloat16

EPS = 1e-6
FFN_HALF = 0.5
CHUNK = 128
N_SGU_GROUPS = 8
INV_SQRT2 = 0.7071067811865476
SUBLANES = 8
LANES = 128

V7X_VMEM_BYTES = 64 * 1024 * 1024
VMEM_LIMIT = V7X_VMEM_BYTES - 6 * 1024 * 1024


def _params(sem):
    return pltpu.CompilerParams(dimension_semantics=sem, vmem_limit_bytes=VMEM_LIMIT)


def _rms(x, g):
    ms = jnp.mean(x * x, axis=-1, keepdims=True)
    return (x * lax.rsqrt(ms + EPS)) * g


def _layernorm(y, g, b):
    mu = jnp.mean(y, axis=-1, keepdims=True)
    yc = y - mu
    var = jnp.mean(yc * yc, axis=-1, keepdims=True)
    return (yc * lax.rsqrt(var + EPS)) * g + b


def _layernorm_sweeps(load, store, width, g_ref, b_ref, cw=512):
    def lane_sum(v):
        acc = v[:, 0:LANES]
        for t in range(1, cw // LANES):
            acc = acc + v[:, t * LANES:(t + 1) * LANES]
        return acc

    n = width // cw
    acc = lane_sum(load(0))
    for c in range(1, n):
        acc = acc + lane_sum(load(c))
    mu = jnp.sum(acc, axis=-1, keepdims=True) * (1.0 / width)
    acc = None
    for c in range(n):
        dev = load(c) - mu
        sq = lane_sum(dev * dev)
        acc = sq if acc is None else acc + sq
    var = jnp.sum(acc, axis=-1, keepdims=True) * (1.0 / width)
    scale = lax.rsqrt(var + EPS)
    for c in range(n):
        cols = slice(c * cw, (c + 1) * cw)
        store(c, ((load(c) - mu) * scale) * g_ref[:, cols] + b_ref[:, cols])


def _silu(x):
    return x * jax.nn.sigmoid(x)


def _gelu(x):
    return 0.5 * x * (1.0 + lax.erf(x * INV_SQRT2))


def _dot(a, w):
    return jnp.dot(a, w, preferred_element_type=F32)


def _rms_rows(x_ref, g_ref, h_ref, copy_ref=None, rows_per_step=64):
    rows = x_ref.shape[0]
    step = rows_per_step
    assert rows % step == 0
    g = g_ref[...]

    def body(c, carry):
        r0 = pl.multiple_of(c * step, step)
        x = x_ref[pl.ds(r0, step), :]
        h_ref[pl.ds(r0, step), :] = _rms(x, g).astype(BF16)
        if copy_ref is not None:
            copy_ref[pl.ds(r0, step), :] = x
        return carry

    lax.fori_loop(0, rows // step, body, 0)


def _rms_rows_inplace(o_ref, g_ref, rows_per_step=64):
    rows, step = o_ref.shape[0], rows_per_step
    g = g_ref[...]

    def body(c, carry):
        r0 = pl.multiple_of(c * step, step)
        o_ref[pl.ds(r0, step), :] = _rms(o_ref[pl.ds(r0, step), :], g)
        return carry

    lax.fori_loop(0, rows // step, body, 0)


def _ffn_rows(h, wg, wu, wd):
    gate = _dot(h, wg)
    up = _dot(h, wu)
    act = (_silu(gate) * (up * FFN_HALF)).astype(BF16)
    return _dot(act, wd)


def _ffn_step(h_ref, wg, wu, wd, o_ref):
    o_ref[...] += _ffn_rows(h_ref[...], wg, wu, wd)


def _ffn_first_step(x_ref, g_ref, h_ref, wg, wu, wd, o_ref, tail, rc, sub=16):
    tm = x_ref.shape[0]
    g = g_ref[...]
    for r in range(tm // rc):
        for s in range(rc // sub):
            r0 = r * rc + s * sub
            x = x_ref[r0:r0 + sub, :]
            if tail is not None and r0 >= tail[1]:
                use_tail, split_row, tail_ref = tail
                x = jnp.where(use_tail, tail_ref[r0 - split_row:r0 - split_row + sub, :], x)
            h_ref[r0:r0 + sub, :] = _rms(x, g).astype(BF16)
            o_ref[r0:r0 + sub, :] = x
        rows = slice(r * rc, (r + 1) * rc)
        o_ref[rows, :] += _ffn_rows(h_ref[rows, :], wg, wu, wd)


def _ffn_head_kernel(x_ref, g_ref, gf_ref, wg_ref, wu_ref, wd_ref,
                     o_ref, wg16_ref, wu16_ref, wd16_ref, h_ref, *, final_norm):
    f = pl.program_id(0)

    @pl.when(f == 0)
    def _():
        _rms_rows(x_ref, g_ref, h_ref, copy_ref=o_ref)

    wg = wg_ref[...].astype(BF16)
    wu = wu_ref[...].astype(BF16)
    wd = wd_ref[...].astype(BF16)
    wg16_ref[...] = wg
    wu16_ref[...] = wu
    wd16_ref[...] = wd
    _ffn_step(h_ref, wg, wu, wd, o_ref)

    if final_norm:
        @pl.when(f == pl.num_programs(0) - 1)
        def _():
            _rms_rows_inplace(o_ref, gf_ref)


def _ffn_tail_kernel(*refs, first, last, split_row, n_casts):
    refs = list(refs)
    x_ref, g_ref, gf_ref, wg_ref, wu_ref, wd_ref, _ = refs[:7]
    xs_ref = refs[7] if first else None
    n_in = 8 if first else 7
    cast_src = refs[n_in:n_in + n_casts]
    o_ref = refs[n_in + n_casts]
    cast_dst = refs[n_in + n_casts + 1:n_in + 2 * n_casts + 1]
    h_ref = refs[-1]
    i, f = pl.program_id(0), pl.program_id(1)
    last_tile = i == pl.num_programs(0) - 1

    for src, dst in zip(cast_src, cast_dst):
        tn = dst.shape[2]
        for c in range(dst.shape[0]):
            dst[c] = src[:, c * tn:(c + 1) * tn].astype(BF16)

    @pl.when(f == 0)
    def _():
        tail = (last_tile, split_row, xs_ref) if first else None
        _ffn_first_step(x_ref, g_ref, h_ref, wg_ref[...], wu_ref[...], wd_ref[...], o_ref,
                        tail, rc=x_ref.shape[0] // 4)

    @pl.when(f > 0)
    def _():
        _ffn_step(h_ref, wg_ref[...], wu_ref[...], wd_ref[...], o_ref)

    if last:
        @pl.when(f == pl.num_programs(1) - 1)
        def _():
            _rms_rows_inplace(o_ref, gf_ref)


def _ffn(x, xs, g3, gf2, w_gate, w_up, w_down, layer, which, casts, *, n_prompt, n_sample,
         first, last, tm, tf_head, tf_tail):
    d = x.shape[1]
    dff = w_gate.shape[-1]
    m = n_prompt + n_sample
    assert m % tm == 0
    n_tiles = m // tm
    split_row = n_prompt - (n_tiles - 1) * tm
    assert 0 < split_row < tm and split_row % 64 == 0 and tm - split_row == n_sample
    out_rows = m
    name = f"ffn_{layer}_{which}"

    once = pl.Buffered(1)
    nf_head = dff // tf_head
    assert tf_head == tf_tail
    head_out, wg16, wu16, wd16 = pl.pallas_call(
        functools.partial(_ffn_head_kernel, final_norm=last),
        grid=(nf_head,),
        in_specs=[
            pl.BlockSpec((tm, d), lambda f: (0, 0), pipeline_mode=once),
            pl.BlockSpec((None, 1, d), lambda f: (layer, 0, 0)),
            pl.BlockSpec((1, d), lambda f: (0, 0)),
            pl.BlockSpec((None, None, d, tf_head), lambda f: (layer, which, 0, f)),
            pl.BlockSpec((None, None, d, tf_head), lambda f: (layer, which, 0, f)),
            pl.BlockSpec((None, None, tf_head, d), lambda f: (layer, which, f, 0)),
        ],
        out_specs=[
            pl.BlockSpec((tm, d), lambda f: (0, 0), pipeline_mode=once),
            pl.BlockSpec((None, d, tf_head), lambda f: (f, 0, 0)),
            pl.BlockSpec((None, d, tf_head), lambda f: (f, 0, 0)),
            pl.BlockSpec((None, tf_head, d), lambda f: (f, 0, 0)),
        ],
        out_shape=[
            jax.ShapeDtypeStruct((out_rows, d), F32),
            jax.ShapeDtypeStruct((nf_head, d, tf_head), BF16),
            jax.ShapeDtypeStruct((nf_head, d, tf_head), BF16),
            jax.ShapeDtypeStruct((nf_head, tf_head, d), BF16),
        ],
        scratch_shapes=[pltpu.VMEM((tm, d), BF16)],
        compiler_params=_params(("arbitrary",)),
        name=name + "_head",
    )(x, g3, gf2, w_gate, w_up, w_down)

    in_specs = [
        pl.BlockSpec((tm, d), lambda i, f: (i + 1, 0)),
        pl.BlockSpec((None, 1, d), lambda i, f: (layer, 0, 0)),
        pl.BlockSpec((1, d), lambda i, f: (0, 0)),
        pl.BlockSpec((None, d, tf_tail), lambda i, f: (f, 0, 0)),
        pl.BlockSpec((None, d, tf_tail), lambda i, f: (f, 0, 0)),
        pl.BlockSpec((None, tf_tail, d), lambda i, f: (f, 0, 0)),
        pl.BlockSpec(memory_space=pl.ANY),
    ]
    args = [x, g3, gf2, wg16, wu16, wd16, head_out]
    if first:
        in_specs.append(pl.BlockSpec((n_sample, d), lambda i, f: (0, 0)))
        args.append(xs)
    nf = dff // tf_tail
    out_specs = [pl.BlockSpec((tm, d), lambda i, f: (i + 1, 0))]
    out_shape = [jax.ShapeDtypeStruct((out_rows, d), F32)]
    for w, lj, rb, tn in casts:
        _, rows, cols = w.shape
        n_blk = rows // rb
        assert rows % rb == 0 and cols % tn == 0 and n_blk <= (n_tiles - 1) * nf
        blk = lambda i, f, n_blk=n_blk: jnp.minimum(i * nf + f, n_blk - 1)
        in_specs.append(pl.BlockSpec((None, rb, cols), lambda i, f, lj=lj, blk=blk: (lj, blk(i, f), 0)))
        args.append(w)
        out_specs.append(pl.BlockSpec((cols // tn, rb, tn), lambda i, f, blk=blk: (0, blk(i, f), 0)))
        out_shape.append(jax.ShapeDtypeStruct((cols // tn, rows, tn), BF16))
    outs = pl.pallas_call(
        functools.partial(_ffn_tail_kernel, first=first, last=last, split_row=split_row,
                          n_casts=len(casts)),
        grid=(n_tiles - 1, nf),
        in_specs=in_specs,
        out_specs=out_specs,
        out_shape=out_shape,
        scratch_shapes=[pltpu.VMEM((tm, d), BF16)],
        input_output_aliases={6: 0},
        compiler_params=_params(("arbitrary", "arbitrary")),
        name=name + "_tail",
    )(*args)
    return outs[0], list(outs[1:])


def _a1_kernel(x_ref, g_ref, wa_ref, wg_ref, ba_ref, bg_ref, o_ref, h_ref):
    @pl.when(pl.program_id(1) == 0)
    def _():
        _rms_rows(x_ref, g_ref, h_ref)

    h = h_ref[...]
    a = _dot(h, wa_ref[...]) + ba_ref[...]
    gt = _dot(h, wg_ref[...]) + bg_ref[...]
    o_ref[...] = a * jax.nn.sigmoid(gt)


def _a1(x, g3, w1, b1, layer, j, *, row0, n_rows, tm):
    d = x.shape[1]
    nb, tn = w1.shape[0] // 2, w1.shape[2]
    d_a = nb * tn
    rb0 = row0 // tm
    return pl.pallas_call(
        _a1_kernel,
        grid=(n_rows // tm, nb),
        in_specs=[
            pl.BlockSpec((tm, d), lambda i, n: (rb0 + i, 0)),
            pl.BlockSpec((None, 1, d), lambda i, n: (layer, 0, 0)),
            pl.BlockSpec((None, d, tn), lambda i, n: (n, 0, 0)),
            pl.BlockSpec((None, d, tn), lambda i, n: (n + nb, 0, 0)),
            pl.BlockSpec((None, 1, tn), lambda i, n: (j, 0, n)),
            pl.BlockSpec((None, 1, tn), lambda i, n: (j, 0, n + nb)),
        ],
        out_specs=pl.BlockSpec((tm, tn), lambda i, n: (i, n)),
        out_shape=jax.ShapeDtypeStruct((n_rows, d_a), F32),
        scratch_shapes=[pltpu.VMEM((tm, d), BF16)],
        compiler_params=_params(("parallel", "arbitrary")),
        name=f"mix_a1s_{j}",
    )(x, g3, w1, w1, b1, b1)


def _dwconv_block(xp_ref, w_ref, r0, rows, cols, taps, halo):
    lead = halo - (taps - 1)
    assert halo % SUBLANES == 0 and lead >= 0
    acc = None
    for s in range(SUBLANES):
        group = [k for k in range(taps) if (k + lead) % SUBLANES == s]
        if not group:
            continue
        ext = rows + (SUBLANES if s else 0)
        part = None
        for k in group:
            base = r0 + ((k + lead) // SUBLANES) * SUBLANES
            term = xp_ref[pl.ds(base, ext), cols] * w_ref[k:k + 1, cols]
            part = term if part is None else part + term
        if s:
            part = part[s:s + rows]
        acc = part if acc is None else acc + part
    return acc


def _start_tile(x_ref, g_ref, h_ref, hal_ref, first_in_seq):
    _rms_rows(x_ref, g_ref, h_ref)

    @pl.when(first_in_seq)
    def _():
        hal_ref[...] = jnp.zeros(hal_ref.shape, hal_ref.dtype)


def _af_kernel(x_ref, g_ref, wa_ref, wg_ref, ba_ref, bgt_ref, wdw_ref, bdw_ref, lg_ref, lb_ref,
               w2_ref, o_ref, st_ref, h_ref, gbuf_ref, wbuf_ref, hal_ref, y_ref, act_ref,
               *, taps, tiles_per_seq, nb, rc):
    i, n = pl.program_id(0), pl.program_id(1)
    tm = x_ref.shape[0]
    tn = wa_ref.shape[1]
    halo = hal_ref.shape[0]
    cb = 128

    @pl.when(n == 0)
    def _():
        _start_tile(x_ref, g_ref, h_ref, hal_ref, i % tiles_per_seq == 0)

    @pl.when(n < nb)
    def _():
        col0 = pl.multiple_of(n * tn, tn)
        gbuf_ref[0:halo, :] = hal_ref[:, pl.ds(col0, tn)]
        wbuf_ref[0:taps, :] = wdw_ref[:, pl.ds(col0, tn)]
        wa, wg = wa_ref[...], wg_ref[...]
        ba, bgt = ba_ref[...], bgt_ref[...]
        for r in range(tm // rc):
            hc = h_ref[r * rc:(r + 1) * rc, :]
            glu = (_dot(hc, wa) + ba) * jax.nn.sigmoid(_dot(hc, wg) + bgt)
            gbuf_ref[halo + r * rc:halo + (r + 1) * rc, :] = glu
            for sub in range(rc // cb):
                r0 = r * rc + sub * cb
                for strip in range(tn // LANES):
                    cols = slice(strip * LANES, (strip + 1) * LANES)
                    y = _dwconv_block(gbuf_ref, wbuf_ref, r0, cb, cols, taps, halo)
                    y_ref[r0:r0 + cb, pl.ds(pl.multiple_of(col0 + strip * LANES, LANES), LANES)] = y
        hal_ref[:, pl.ds(col0, tn)] = gbuf_ref[tm:tm + halo, :]

    @pl.when(n == nb)
    def _():
        w2 = w2_ref[...]
        step, cw = 32, 512
        for r in range(tm // rc):
            for s in range(rc // step):
                blk = slice(r * rc + s * step, r * rc + (s + 1) * step)

                def load(c, blk=blk):
                    cols = slice(c * cw, (c + 1) * cw)
                    return y_ref[blk, cols] + bdw_ref[:, cols]

                def store(c, v, blk=blk):
                    act_ref[blk, c * cw:(c + 1) * cw] = _silu(v).astype(BF16)

                _layernorm_sweeps(load, store, y_ref.shape[1], lg_ref, lb_ref, cw)
            rows = slice(r * rc, (r + 1) * rc)
            o_ref[rows, :] = x_ref[rows, 0:tn] + _dot(act_ref[rows, :], w2)

    @pl.when(n > nb)
    def _():
        col0 = pl.multiple_of((n - nb) * tn, tn)
        o_ref[...] = x_ref[:, pl.ds(col0, tn)] + _dot(act_ref[...], w2_ref[...])

    @pl.when((n == 2 * nb - 1) & (i % tiles_per_seq == tiles_per_seq - 1))
    def _():
        st_ref[...] = hal_ref[halo - (taps - 1):halo, :]


def _a_fused(x, g3, w1, b1, w_dw, b_dw, ln_g, ln_b, w2, layer, j, *, n_prompt, seq, tm):
    m, d = x.shape
    taps = w_dw.shape[1]
    nb, tn = w2.shape[0], w2.shape[2]
    halo = 32
    assert w1.shape == (2 * nb, d, tn) and nb * tn == d and taps - 1 <= halo and seq % tm == 0
    kern = functools.partial(_af_kernel, taps=taps, tiles_per_seq=seq // tm, nb=nb, rc=256)
    taps_pad = -(-taps // SUBLANES) * SUBLANES
    w1_blk = lambda i, n: (jnp.minimum(n, nb - 1), 0, 0)
    w1g_blk = lambda i, n: (jnp.minimum(n, nb - 1) + nb, 0, 0)
    b1_blk = lambda i, n: (j, 0, jnp.minimum(n, nb - 1))
    b1g_blk = lambda i, n: (j, 0, jnp.minimum(n, nb - 1) + nb)
    out_blk = lambda i, n: (i, jnp.maximum(n - nb, 0))
    row = lambda i, n: (j, 0, 0)
    return pl.pallas_call(
        kern,
        grid=(n_prompt // tm, 2 * nb),
        in_specs=[
            pl.BlockSpec((tm, d), lambda i, n: (i, 0)),
            pl.BlockSpec((None, 1, d), lambda i, n: (layer, 0, 0)),
            pl.BlockSpec((None, d, tn), w1_blk),
            pl.BlockSpec((None, d, tn), w1g_blk),
            pl.BlockSpec((None, 1, tn), b1_blk),
            pl.BlockSpec((None, 1, tn), b1g_blk),
            pl.BlockSpec((None, taps, d), row),
            pl.BlockSpec((None, 1, d), row),
            pl.BlockSpec((None, 1, d), row),
            pl.BlockSpec((None, 1, d), row),
            pl.BlockSpec((None, d, tn), lambda i, n: (jnp.maximum(n - nb, 0), 0, 0)),
        ],
        out_specs=[
            pl.BlockSpec((tm, tn), out_blk),
            pl.BlockSpec((None, taps - 1, d), lambda i, n: (i // (seq // tm), 0, 0)),
        ],
        out_shape=[
            jax.ShapeDtypeStruct((m, d), F32),
            jax.ShapeDtypeStruct((n_prompt // seq, taps - 1, d), F32),
        ],
        scratch_shapes=[
            pltpu.VMEM((tm, d), BF16),
            pltpu.VMEM((halo + tm, tn), F32),
            pltpu.VMEM((taps_pad, tn), F32),
            pltpu.VMEM((halo, d), F32),
            pltpu.VMEM((tm, d), F32),
            pltpu.VMEM((tm, d), BF16),
        ],
        input_output_aliases={0: 0},
        compiler_params=_params(("arbitrary", "arbitrary")),
        name=f"mix_a_{j}",
    )(x, g3, w1, w1, b1, b1, w_dw, b_dw, ln_g, ln_b, w2)


def _a_sconv_kernel(*refs):
    st_ref, glu_ref, w_ref, b_ref = refs[:4]
    y_ref, ns_ref = refs[-2:]
    past = st_ref.shape[0]
    acc = st_ref[0] * w_ref[0:1, :]
    for k in range(1, past):
        acc = acc + st_ref[k] * w_ref[k:k + 1, :]
        ns_ref[k - 1] = st_ref[k]
    g = glu_ref[...]
    ns_ref[past - 1] = g
    y_ref[...] = acc + g * w_ref[past:past + 1, :] + b_ref[...]


def _a_sample_conv(state_t, glu, w_dw, b_dw, prev, j, *, tn):
    n_layers, past, nb, d = state_t.shape
    taps = w_dw.shape[1]
    assert taps == past + 1 and glu.shape == (nb, d)
    in_specs = [
        pl.BlockSpec((None, past, nb, tn), lambda n: (j, 0, 0, n)),
        pl.BlockSpec((nb, tn), lambda n: (0, n)),
        pl.BlockSpec((None, taps, tn), lambda n: (j, 0, n)),
        pl.BlockSpec((None, 1, tn), lambda n: (j, 0, n)),
    ]
    args = [state_t, glu, w_dw, b_dw]
    aliases = {}
    if prev is not None:
        in_specs.append(pl.BlockSpec(memory_space=pl.ANY))
        args.append(prev)
        aliases = {4: 1}
    return pl.pallas_call(
        _a_sconv_kernel,
        grid=(d // tn,),
        in_specs=in_specs,
        out_specs=[
            pl.BlockSpec((nb, tn), lambda n: (0, n)),
            pl.BlockSpec((None, past, nb, tn), lambda n: (j, 0, 0, n)),
        ],
        out_shape=[
            jax.ShapeDtypeStruct((nb, d), F32),
            jax.ShapeDtypeStruct((n_layers, past, nb, d), F32),
        ],
        input_output_aliases=aliases,
        compiler_params=_params(("parallel",)),
        name=f"mix_a_sconv_{j}",
    )(*args)


def _a2s_kernel(x_ref, y_ref, lg_ref, lb_ref, w2_ref, o_ref, act_ref):
    @pl.when(pl.program_id(0) == 0)
    def _():
        act_ref[...] = _silu(_layernorm(y_ref[...], lg_ref[...], lb_ref[...])).astype(BF16)

    o_ref[...] = x_ref[...] + _dot(act_ref[...], w2_ref[...])


def _a2_sample(x, y, ln_g, ln_b, w2, j, *, n_prompt):
    m, d = x.shape
    ns = y.shape[0]
    tn = w2.shape[2]
    rb = n_prompt // ns
    return pl.pallas_call(
        _a2s_kernel,
        grid=(d // tn,),
        in_specs=[
            pl.BlockSpec((ns, tn), lambda n: (rb, n)),
            pl.BlockSpec((ns, d), lambda n: (0, 0)),
            pl.BlockSpec((None, 1, d), lambda n: (j, 0, 0)),
            pl.BlockSpec((None, 1, d), lambda n: (j, 0, 0)),
            pl.BlockSpec((None, d, tn), lambda n: (n, 0, 0)),
        ],
        out_specs=pl.BlockSpec((ns, tn), lambda n: (rb, n)),
        out_shape=jax.ShapeDtypeStruct((m, d), F32),
        scratch_shapes=[pltpu.VMEM((ns, d), BF16)],
        input_output_aliases={0: 0},
        compiler_params=_params(("arbitrary",)),
        name=f"mix_a2s_{j}",
    )(x, y, ln_g, ln_b, w2)


def _b1_kernel(x_ref, g_ref, w_ref, b_ref, lg_ref, lb_ref, o_ref, h_ref):
    n = pl.program_id(1)
    tn = w_ref.shape[1]

    @pl.when(n == 0)
    def _():
        _rms_rows(x_ref, g_ref, h_ref)

    z = _dot(h_ref[...], w_ref[...]) + b_ref[...]
    o_ref[:, pl.ds(pl.multiple_of(n * tn, tn), tn)] = _gelu(z)

    @pl.when(n == pl.num_programs(1) - 1)
    def _():
        rows, step, cw = o_ref.shape[0], 32, 512

        def body(c, carry):
            blk = pl.ds(pl.multiple_of(c * step, step), step)

            def load(k):
                return o_ref[blk, k * cw:(k + 1) * cw]

            def store(k, v):
                o_ref[blk, k * cw:(k + 1) * cw] = v

            _layernorm_sweeps(load, store, o_ref.shape[1], lg_ref, lb_ref, cw)
            return carry

        lax.fori_loop(0, rows // step, body, 0)


def _b1(x, g3, w_in, b_in, ln_g, ln_b, layer, j, *, tm):
    m, d = x.shape
    nb, tn = w_in.shape[0] // 2, w_in.shape[2]
    d_sgu = nb * tn
    return pl.pallas_call(
        _b1_kernel,
        grid=(m // tm, nb),
        in_specs=[
            pl.BlockSpec((tm, d), lambda i, n: (i, 0)),
            pl.BlockSpec((None, 1, d), lambda i, n: (layer, 0, 0)),
            pl.BlockSpec((None, d, tn), lambda i, n: (n + nb, 0, 0)),
            pl.BlockSpec((None, 1, tn), lambda i, n: (j, 0, n + nb)),
            pl.BlockSpec((None, 1, d_sgu), lambda i, n: (j, 0, 0)),
            pl.BlockSpec((None, 1, d_sgu), lambda i, n: (j, 0, 0)),
        ],
        out_specs=pl.BlockSpec((tm, d_sgu), lambda i, n: (i, 0)),
        out_shape=jax.ShapeDtypeStruct((m, d_sgu), F32),
        scratch_shapes=[pltpu.VMEM((tm, d), BF16)],
        compiler_params=_params(("parallel", "arbitrary")),
        name=f"mix_b1_{j}",
    )(x, g3, w_in, b_in, ln_g, ln_b)


def _b2_kernel(x_ref, g_ref, v_ref, wi_ref, bi_ref, ws_ref, bs_ref, wo_ref, o_ref, h_ref,
               *, n_prompt_chunks):
    i = pl.program_id(0)
    tm = x_ref.shape[0]

    @pl.when(pl.program_id(1) == 0)
    def _():
        _rms_rows(x_ref, g_ref, h_ref, copy_ref=o_ref)

    u = _gelu(_dot(h_ref[...], wi_ref[...]) + bi_ref[...])
    ws = ws_ref[...]
    row = lax.broadcasted_iota(jnp.int32, ws.shape, 0)
    col = lax.broadcasted_iota(jnp.int32, ws.shape, 1)
    w_tril = jnp.where(col <= row, ws, 0.0).astype(BF16)
    bs = bs_ref[...]
    parts = []
    for c in range(tm // CHUNK):
        rows = slice(c * CHUNK, (c + 1) * CHUNK)
        v = v_ref[rows, :]
        s_prompt = _dot(w_tril, v.astype(BF16)) + bs
        s_sample = ws[0:1, 0:1] * v + bs[0:1, :]
        is_sample = i * (tm // CHUNK) + c >= n_prompt_chunks
        parts.append(u[rows, :] * jnp.where(is_sample, s_sample, s_prompt))
    y = jnp.concatenate(parts, axis=0).astype(BF16)
    o_ref[...] += _dot(y, wo_ref[...])


def _b2(x, g3, vn, w_in, b_in, w_s, b_s3, w_out, layer, j, *, n_prompt, tm):
    m, d = x.shape
    d_sgu = vn.shape[1]
    gw = d_sgu // N_SGU_GROUPS
    assert w_in.shape == (2 * N_SGU_GROUPS, d, gw) and w_out.shape == (1, d_sgu, d)
    kern = functools.partial(_b2_kernel, n_prompt_chunks=n_prompt // CHUNK)
    return pl.pallas_call(
        kern,
        grid=(m // tm, N_SGU_GROUPS),
        in_specs=[
            pl.BlockSpec((tm, d), lambda i, g: (i, 0)),
            pl.BlockSpec((None, 1, d), lambda i, g: (layer, 0, 0)),
            pl.BlockSpec((tm, gw), lambda i, g: (i, g)),
            pl.BlockSpec((None, d, gw), lambda i, g: (g, 0, 0)),
            pl.BlockSpec((None, 1, gw), lambda i, g: (j, 0, g)),
            pl.BlockSpec((None, None, CHUNK, CHUNK), lambda i, g: (j, g, 0, 0)),
            pl.BlockSpec((None, None, CHUNK, 1), lambda i, g: (j, g, 0, 0)),
            pl.BlockSpec((None, gw, d), lambda i, g: (0, g, 0)),
        ],
        out_specs=pl.BlockSpec((tm, d), lambda i, g: (i, 0)),
        out_shape=jax.ShapeDtypeStruct((m, d), F32),
        scratch_shapes=[pltpu.VMEM((tm, d), BF16)],
        compiler_params=_params(("parallel", "arbitrary")),
        name=f"mix_b2_{j}",
    )(x, g3, vn, w_in, b_in, w_s, b_s3, w_out)


def _c1_kernel(x_ref, g_ref, wb_ref, wc_ref, wx_ref, bg_ref, cx_ref, h_ref):
    @pl.when(pl.program_id(1) == 0)
    def _():
        _rms_rows(x_ref, g_ref, h_ref)

    h = h_ref[...]
    bg_ref[...] = _dot(h, wb_ref[...])
    cx_ref[...] = _dot(h, wc_ref[...]) * _dot(h, wx_ref[...])


def _c1(x, g3, w_in, layer, j, *, row0, n_rows, tm):
    d = x.shape[1]
    nb, tn = w_in.shape[0] // 3, w_in.shape[2]
    d_c = nb * tn
    rb0 = row0 // tm
    out = jax.ShapeDtypeStruct((n_rows, d_c), F32)
    return pl.pallas_call(
        _c1_kernel,
        grid=(n_rows // tm, nb),
        in_specs=[
            pl.BlockSpec((tm, d), lambda i, n: (rb0 + i, 0)),
            pl.BlockSpec((None, 1, d), lambda i, n: (layer, 0, 0)),
            pl.BlockSpec((None, d, tn), lambda i, n: (n, 0, 0)),
            pl.BlockSpec((None, d, tn), lambda i, n: (n + nb, 0, 0)),
            pl.BlockSpec((None, d, tn), lambda i, n: (n + 2 * nb, 0, 0)),
        ],
        out_specs=[pl.BlockSpec((tm, tn), lambda i, n: (i, n))] * 2,
        out_shape=[out, out],
        scratch_shapes=[pltpu.VMEM((tm, d), BF16)],
        compiler_params=_params(("parallel", "arbitrary")),
        name=f"mix_c1s_{j}",
    )(x, g3, w_in, w_in, w_in)


def _cf_kernel(x_ref, g_ref, wb_ref, wc_ref, wx_ref, wcv_ref, wo_ref, o_ref, st_ref,
               h_ref, gbuf_ref, wbuf_ref, hal_ref, act_ref, *, taps, tiles_per_seq, nb, rc):
    i, n = pl.program_id(0), pl.program_id(1)
    tm = x_ref.shape[0]
    tn = wb_ref.shape[1]
    halo = hal_ref.shape[0]
    cb = 128

    @pl.when(n == 0)
    def _():
        _start_tile(x_ref, g_ref, h_ref, hal_ref, i % tiles_per_seq == 0)

    @pl.when(n < nb)
    def _():
        col0 = pl.multiple_of(n * tn, tn)
        gbuf_ref[0:halo, :] = hal_ref[:, pl.ds(col0, tn)]
        wbuf_ref[0:taps, :] = wcv_ref[:, pl.ds(col0, tn)]
        wb, wc, wx = wb_ref[...], wc_ref[...], wx_ref[...]
        for r in range(tm // rc):
            hc = h_ref[r * rc:(r + 1) * rc, :]
            bg = _dot(hc, wb)
            gbuf_ref[halo + r * rc:halo + (r + 1) * rc, :] = _dot(hc, wc) * _dot(hc, wx)
            for sub in range(rc // cb):
                r0 = r * rc + sub * cb
                for strip in range(tn // LANES):
                    cols = slice(strip * LANES, (strip + 1) * LANES)
                    y = _dwconv_block(gbuf_ref, wbuf_ref, r0, cb, cols, taps, halo)
                    gate = bg[sub * cb:(sub + 1) * cb, cols]
                    act_ref[r0:r0 + cb, pl.ds(pl.multiple_of(col0 + strip * LANES, LANES), LANES)] = (
                        gate * y).astype(BF16)
        hal_ref[:, pl.ds(col0, tn)] = gbuf_ref[tm:tm + halo, :]

    @pl.when(n >= nb)
    def _():
        col0 = pl.multiple_of((n - nb) * tn, tn)
        o_ref[...] = x_ref[:, pl.ds(col0, tn)] + _dot(act_ref[...], wo_ref[...])

    @pl.when((n == 2 * nb - 1) & (i % tiles_per_seq == tiles_per_seq - 1))
    def _():
        st_ref[...] = hal_ref[halo - (taps - 1):halo, :]


def _c_fused(x, g3, w_in, w_conv, w_out, layer, j, *, n_prompt, seq, tm):
    m, d = x.shape
    taps = w_conv.shape[1]
    nb, tn = w_out.shape[0], w_out.shape[2]
    halo = SUBLANES
    assert w_in.shape == (3 * nb, d, tn) and nb * tn == d and taps - 1 <= halo and seq % tm == 0
    kern = functools.partial(_cf_kernel, taps=taps, tiles_per_seq=seq // tm, nb=nb, rc=256)
    taps_pad = -(-taps // SUBLANES) * SUBLANES
    blk = lambda part: (lambda i, n: (jnp.minimum(n, nb - 1) + part * nb, 0, 0))
    return pl.pallas_call(
        kern,
        grid=(n_prompt // tm, 2 * nb),
        in_specs=[
            pl.BlockSpec((tm, d), lambda i, n: (i, 0)),
            pl.BlockSpec((None, 1, d), lambda i, n: (layer, 0, 0)),
            pl.BlockSpec((None, d, tn), blk(0)),
            pl.BlockSpec((None, d, tn), blk(1)),
            pl.BlockSpec((None, d, tn), blk(2)),
            pl.BlockSpec((None, taps, d), lambda i, n: (j, 0, 0)),
            pl.BlockSpec((None, d, tn), lambda i, n: (jnp.maximum(n - nb, 0), 0, 0)),
        ],
        out_specs=[
            pl.BlockSpec((tm, tn), lambda i, n: (i, jnp.maximum(n - nb, 0))),
            pl.BlockSpec((None, taps - 1, d), lambda i, n: (i // (seq // tm), 0, 0)),
        ],
        out_shape=[
            jax.ShapeDtypeStruct((m, d), F32),
            jax.ShapeDtypeStruct((n_prompt // seq, taps - 1, d), F32),
        ],
        scratch_shapes=[
            pltpu.VMEM((tm, d), BF16),
            pltpu.VMEM((halo + tm, tn), F32),
            pltpu.VMEM((taps_pad, tn), F32),
            pltpu.VMEM((halo, d), F32),
            pltpu.VMEM((tm, d), BF16),
        ],
        input_output_aliases={0: 0},
        compiler_params=_params(("arbitrary", "arbitrary")),
        name=f"mix_c_{j}",
    )(x, g3, w_in, w_in, w_in, w_conv, w_out)


def _c2s_kernel(x_ref, bg_ref, cx_ref, s0_ref, s1_ref, w_ref, wo_ref, o_ref, act_ref):
    @pl.when(pl.program_id(0) == 0)
    def _():
        w = w_ref[...]
        y = s0_ref[...] * w[0:1] + s1_ref[...] * w[1:2] + cx_ref[...] * w[2:3]
        act_ref[...] = (bg_ref[...] * y).astype(BF16)

    o_ref[...] = x_ref[...] + _dot(act_ref[...], wo_ref[...])


def _c2_sample(x, bg, cx, s0, s1, w_conv, w_out, j, *, n_prompt):
    m, d = x.shape
    ns = s0.shape[0]
    tn = w_out.shape[2]
    rb = n_prompt // ns
    taps = w_conv.shape[1]
    assert taps == 3
    return pl.pallas_call(
        _c2s_kernel,
        grid=(d // tn,),
        in_specs=[
            pl.BlockSpec((ns, tn), lambda n: (rb, n)),
            pl.BlockSpec((ns, d), lambda n: (0, 0)),
            pl.BlockSpec((ns, d), lambda n: (0, 0)),
            pl.BlockSpec((ns, d), lambda n: (0, 0)),
            pl.BlockSpec((ns, d), lambda n: (0, 0)),
            pl.BlockSpec((None, taps, d), lambda n: (j, 0, 0)),
            pl.BlockSpec((None, d, tn), lambda n: (n, 0, 0)),
        ],
        out_specs=pl.BlockSpec((ns, tn), lambda n: (rb, n)),
        out_shape=jax.ShapeDtypeStruct((m, d), F32),
        scratch_shapes=[pltpu.VMEM((ns, d), BF16)],
        input_output_aliases={0: 0},
        compiler_params=_params(("arbitrary",)),
        name=f"mix_c2s_{j}",
    )(x, bg, cx, s0, s1, w_conv, w_out)


def _last_rows(a, batch, seq, rows):
    return jnp.stack([a[(b + 1) * seq - rows:(b + 1) * seq] for b in range(batch)])


def kernel(x_prompt, x_sample, state_conv_a, state_conv_c, g_ffn1, g_mix, g_ffn2, g_final,
           w_ffn_gate, w_ffn_up, w_ffn_down,
           a_w_pw1, a_b_pw1, a_w_dw, a_b_dw, a_ln_g, a_ln_b, a_w_pw2,
           b_w_in, b_b_in, b_ln_g, b_ln_b, b_w_s, b_b_s, b_w_out,
           c_w_in, c_w_conv, c_w_out):
    batch, seq, d = x_prompt.shape
    n_sample = x_sample.shape[0]
    assert x_sample.shape[1] == 1
    n_prompt = batch * seq
    depth = g_ffn1.shape[0]
    past_c = state_conv_c.shape[2]
    assert past_c == 2
    d_sgu = b_ln_g.shape[-1]

    row3 = lambda a: a.reshape(a.shape[0], 1, a.shape[1])
    g1, gm, g2 = row3(g_ffn1), row3(g_mix), row3(g_ffn2)
    gf = g_final.reshape(1, d)
    a_b1, a_bd, a_lg, a_lb = row3(a_b_pw1), row3(a_b_dw), row3(a_ln_g), row3(a_ln_b)
    b_bi, b_lg, b_lb = row3(b_b_in), row3(b_ln_g), row3(b_ln_b)
    b_bs = b_b_s.reshape(*b_b_s.shape, 1)
    state_a_t = jnp.transpose(state_conv_a, (0, 2, 1, 3))

    ffn = functools.partial(_ffn, n_prompt=n_prompt, n_sample=n_sample,
                            tm=832, tf_head=512, tf_tail=512)
    new_a_p, new_b_p, new_b_s, new_c_p, new_c_s = [], [], [], [], []
    new_a_s_t = None

    x = x_prompt.reshape(n_prompt, d)
    xs = x_sample.reshape(n_sample, d)
    gw = d_sgu // N_SGU_GROUPS
    for i in range(depth):
        kind, j = i % 3, i // 3
        if kind == 0:
            casts = [(a_w_pw1, j, 32, 512), (a_w_pw2, j, 32, 512)]
        elif kind == 1:
            casts = [(b_w_in, j, 32, gw), (b_w_out, j, 64, d)]
        else:
            casts = [(c_w_in, j, 32, 512), (c_w_out, j, 32, 512)]
        x, (w_a16, w_b16) = ffn(x, xs, g1, gf, w_ffn_gate, w_ffn_up, w_ffn_down, i, 0, casts,
                                first=(i == 0), last=False)
        if kind == 0:
            glu_s = _a1(x, gm, w_a16, a_b1, i, j, row0=n_prompt, n_rows=n_sample, tm=n_sample)
            y_s, new_a_s_t = _a_sample_conv(state_a_t, glu_s, a_w_dw, a_bd, new_a_s_t, j, tn=256)
            x, st_p = _a_fused(x, gm, w_a16, a_b1, a_w_dw, a_bd, a_lg, a_lb, w_b16, i, j,
                               n_prompt=n_prompt, seq=seq, tm=1024)
            x = _a2_sample(x, y_s, a_lg, a_lb, w_b16, j, n_prompt=n_prompt)
            new_a_p.append(st_p)
        elif kind == 1:
            vn = _b1(x, gm, w_a16, b_bi, b_lg, b_lb, i, j, tm=832)
            x = _b2(x, gm, vn, w_a16, b_bi, b_w_s, b_bs, w_b16, i, j, n_prompt=n_prompt, tm=640)
            new_b_p.append(_last_rows(vn, batch, seq, CHUNK))
            new_b_s.append(vn[n_prompt:].reshape(n_sample, 1, d_sgu))
        else:
            bg_s, cx_s = _c1(x, gm, w_a16, i, j, row0=n_prompt, n_rows=n_sample, tm=n_sample)
            s0, s1 = state_conv_c[j, :, 0], state_conv_c[j, :, 1]
            x, st_p = _c_fused(x, gm, w_a16, c_w_conv, w_b16, i, j,
                               n_prompt=n_prompt, seq=seq, tm=1024)
            x = _c2_sample(x, bg_s, cx_s, s0, s1, c_w_conv, w_b16, j, n_prompt=n_prompt)
            new_c_p.append(st_p)
            new_c_s.append(jnp.stack([s1, cx_s], axis=1))
        last = i == depth - 1
        x, _ = ffn(x, xs, g2, gf, w_ffn_gate, w_ffn_up, w_ffn_down, i, 1, [],
                   first=False, last=last)

    return (x[:n_prompt].reshape(batch, seq, d), x[n_prompt:].reshape(n_sample, 1, d),
            jnp.stack(new_a_p), jnp.transpose(new_a_s_t, (0, 2, 1, 3)),
            jnp.stack(new_b_p), jnp.stack(new_b_s),
            jnp.stack(new_c_p), jnp.stack(new_c_s))
```

```python
import functools

import jax
import jax.numpy as jnp
from jax import lax
from jax.experimental import pallas as pl
from jax.experimental.pallas import tpu as pltpu

F32 = jnp.float32
BF16 = jnp.bfloat16

EPS = 1e-6
FFN_HALF = 0.5
CHUNK = 128
N_SGU_GROUPS = 8
INV_SQRT2 = 0.7071067811865476
SUBLANES = 8
LANES = 128

V7X_VMEM_BYTES = 64 * 1024 * 1024
VMEM_LIMIT = V7X_VMEM_BYTES - 6 * 1024 * 1024


def _params(sem):
    return pltpu.CompilerParams(dimension_semantics=sem, vmem_limit_bytes=VMEM_LIMIT)


def _read_once_rows(n_tiles, first_tile=0):
    return lambda i, s: (jnp.minimum(first_tile + i + jnp.minimum(s, 1), n_tiles - 1), 0)


def _rms(x, g):
    ms = jnp.mean(x * x, axis=-1, keepdims=True)
    return (x * lax.rsqrt(ms + EPS)) * g


def _layernorm(y, g, b):
    mu = jnp.mean(y, axis=-1, keepdims=True)
    yc = y - mu
    var = jnp.mean(yc * yc, axis=-1, keepdims=True)
    return (yc * lax.rsqrt(var + EPS)) * g + b


def _layernorm_sweeps(load, store, width, g_ref, b_ref, cw=512):
    def lane_sum(v):
        acc = v[:, 0:LANES]
        for t in range(1, cw // LANES):
            acc = acc + v[:, t * LANES:(t + 1) * LANES]
        return acc

    n = width // cw
    acc = lane_sum(load(0))
    for c in range(1, n):
        acc = acc + lane_sum(load(c))
    mu = jnp.sum(acc, axis=-1, keepdims=True) * (1.0 / width)
    acc = None
    for c in range(n):
        dev = load(c) - mu
        sq = lane_sum(dev * dev)
        acc = sq if acc is None else acc + sq
    var = jnp.sum(acc, axis=-1, keepdims=True) * (1.0 / width)
    scale = lax.rsqrt(var + EPS)
    for c in range(n):
        cols = slice(c * cw, (c + 1) * cw)
        store(c, ((load(c) - mu) * scale) * g_ref[:, cols] + b_ref[:, cols])


def _silu(x):
    return x * jax.nn.sigmoid(x)


def _gelu(x):
    return 0.5 * x * (1.0 + lax.erf(x * INV_SQRT2))


def _dot(a, w):
    return jnp.dot(a, w, preferred_element_type=F32)


def _rms_rows(x_ref, g_ref, h_ref, copy_ref=None, rows_per_step=64):
    rows = x_ref.shape[0]
    step = rows_per_step
    assert rows % step == 0
    g = g_ref[...]

    def body(c, carry):
        r0 = pl.multiple_of(c * step, step)
        x = x_ref[pl.ds(r0, step), :]
        h_ref[pl.ds(r0, step), :] = _rms(x, g).astype(BF16)
        if copy_ref is not None:
            copy_ref[pl.ds(r0, step), :] = x
        return carry

    lax.fori_loop(0, rows // step, body, 0)


def _rms_rows_inplace(o_ref, g_ref, rows_per_step=64):
    rows, step = o_ref.shape[0], rows_per_step
    g = g_ref[...]

    def body(c, carry):
        r0 = pl.multiple_of(c * step, step)
        o_ref[pl.ds(r0, step), :] = _rms(o_ref[pl.ds(r0, step), :], g)
        return carry

    lax.fori_loop(0, rows // step, body, 0)


def _ffn_rows(h, wgu, wd):
    tf = wgu.shape[1] // 2
    gu = _dot(h, wgu)
    gate, up = gu[:, :tf], gu[:, tf:]
    act = (_silu(gate) * (up * FFN_HALF)).astype(BF16)
    return _dot(act, wd)


def _ffn_step(h_ref, wgu, wd, o_ref):
    o_ref[...] += _ffn_rows(h_ref[...], wgu, wd)


def _ffn_first_step(x_ref, g_ref, h_ref, wgu, wd, o_ref, tail, rc, sub=16):
    tm = x_ref.shape[0]
    g = g_ref[...]
    for r in range(tm // rc):
        for s in range(rc // sub):
            r0 = r * rc + s * sub
            x = x_ref[r0:r0 + sub, :]
            if tail is not None and r0 >= tail[1]:
                use_tail, split_row, tail_ref = tail
                x = jnp.where(use_tail, tail_ref[r0 - split_row:r0 - split_row + sub, :], x)
            h_ref[r0:r0 + sub, :] = _rms(x, g).astype(BF16)
            o_ref[r0:r0 + sub, :] = x
        rows = slice(r * rc, (r + 1) * rc)
        o_ref[rows, :] += _ffn_rows(h_ref[rows, :], wgu, wd)


def _ffn_head_kernel(x_ref, g_ref, gf_ref, wg_ref, wu_ref, wd_ref,
                     o_ref, wgu16_ref, wd16_ref, h_ref, *, final_norm):
    f = pl.program_id(0)

    @pl.when(f == 0)
    def _():
        _rms_rows(x_ref, g_ref, h_ref, copy_ref=o_ref)

    wgu = jnp.concatenate([wg_ref[...].astype(BF16), wu_ref[...].astype(BF16)], axis=1)
    wd = wd_ref[...].astype(BF16)
    wgu16_ref[...] = wgu
    wd16_ref[...] = wd
    _ffn_step(h_ref, wgu, wd, o_ref)

    if final_norm:
        @pl.when(f == pl.num_programs(0) - 1)
        def _():
            _rms_rows_inplace(o_ref, gf_ref)


def _ffn_tail_kernel(*refs, first, last, split_row, n_casts):
    refs = list(refs)
    x_ref, g_ref, gf_ref, wgu_ref, wd_ref, _ = refs[:6]
    xs_ref = refs[6] if first else None
    n_in = 7 if first else 6
    cast_src = refs[n_in:n_in + n_casts]
    o_ref = refs[n_in + n_casts]
    cast_dst = refs[n_in + n_casts + 1:n_in + 2 * n_casts + 1]
    h_ref = refs[-1]
    i, f = pl.program_id(0), pl.program_id(1)
    last_tile = i == pl.num_programs(0) - 1

    for src, dst in zip(cast_src, cast_dst):
        tn = dst.shape[2]
        for c in range(dst.shape[0]):
            dst[c] = src[:, c * tn:(c + 1) * tn].astype(BF16)

    @pl.when(f == 0)
    def _():
        tail = (last_tile, split_row, xs_ref) if first else None
        _ffn_first_step(x_ref, g_ref, h_ref, wgu_ref[...], wd_ref[...], o_ref,
                        tail, rc=x_ref.shape[0] // 4)

    @pl.when(f > 0)
    def _():
        _ffn_step(h_ref, wgu_ref[...], wd_ref[...], o_ref)

    if last:
        @pl.when(f == pl.num_programs(1) - 1)
        def _():
            _rms_rows_inplace(o_ref, gf_ref)


def _ffn(x, xs, g3, gf2, w_gate, w_up, w_down, layer, which, casts, *, n_prompt, n_sample,
         first, last, tm, tf_head, tf_tail):
    d = x.shape[1]
    dff = w_gate.shape[-1]
    m = n_prompt + n_sample
    assert m % tm == 0
    n_tiles = m // tm
    split_row = n_prompt - (n_tiles - 1) * tm
    assert 0 < split_row < tm and split_row % 64 == 0 and tm - split_row == n_sample
    out_rows = m
    name = f"ffn_{layer}_{which}"

    once = pl.Buffered(1)
    nf_head = dff // tf_head
    assert tf_head == tf_tail
    head_out, wgu16, wd16 = pl.pallas_call(
        functools.partial(_ffn_head_kernel, final_norm=last),
        grid=(nf_head,),
        in_specs=[
            pl.BlockSpec((tm, d), lambda f: (0, 0), pipeline_mode=once),
            pl.BlockSpec((None, 1, d), lambda f: (layer, 0, 0)),
            pl.BlockSpec((1, d), lambda f: (0, 0)),
            pl.BlockSpec((None, None, d, tf_head), lambda f: (layer, which, 0, f)),
            pl.BlockSpec((None, None, d, tf_head), lambda f: (layer, which, 0, f)),
            pl.BlockSpec((None, None, tf_head, d), lambda f: (layer, which, f, 0)),
        ],
        out_specs=[
            pl.BlockSpec((tm, d), lambda f: (0, 0), pipeline_mode=once),
            pl.BlockSpec((None, d, 2 * tf_head), lambda f: (f, 0, 0)),
            pl.BlockSpec((None, tf_head, d), lambda f: (f, 0, 0)),
        ],
        out_shape=[
            jax.ShapeDtypeStruct((out_rows, d), F32),
            jax.ShapeDtypeStruct((nf_head, d, 2 * tf_head), BF16),
            jax.ShapeDtypeStruct((nf_head, tf_head, d), BF16),
        ],
        scratch_shapes=[pltpu.VMEM((tm, d), BF16)],
        compiler_params=_params(("arbitrary",)),
        name=name + "_head",
    )(x, g3, gf2, w_gate, w_up, w_down)

    in_specs = [
        pl.BlockSpec((tm, d), _read_once_rows(n_tiles, first_tile=1)),
        pl.BlockSpec((None, 1, d), lambda i, f: (layer, 0, 0)),
        pl.BlockSpec((1, d), lambda i, f: (0, 0)),
        pl.BlockSpec((None, d, 2 * tf_tail), lambda i, f: (f, 0, 0)),
        pl.BlockSpec((None, tf_tail, d), lambda i, f: (f, 0, 0)),
        pl.BlockSpec(memory_space=pl.ANY),
    ]
    args = [x, g3, gf2, wgu16, wd16, head_out]
    if first:
        in_specs.append(pl.BlockSpec((n_sample, d), lambda i, f: (0, 0)))
        args.append(xs)
    nf = dff // tf_tail
    out_specs = [pl.BlockSpec((tm, d), lambda i, f: (i + 1, 0))]
    out_shape = [jax.ShapeDtypeStruct((out_rows, d), F32)]
    for w, lj, rb, tn in casts:
        _, rows, cols = w.shape
        n_blk = rows // rb
        assert rows % rb == 0 and cols % tn == 0 and n_blk <= (n_tiles - 1) * nf
        blk = lambda i, f, n_blk=n_blk: jnp.minimum(i * nf + f, n_blk - 1)
        in_specs.append(pl.BlockSpec((None, rb, cols), lambda i, f, lj=lj, blk=blk: (lj, blk(i, f), 0)))
        args.append(w)
        out_specs.append(pl.BlockSpec((cols // tn, rb, tn), lambda i, f, blk=blk: (0, blk(i, f), 0)))
        out_shape.append(jax.ShapeDtypeStruct((cols // tn, rows, tn), BF16))
    outs = pl.pallas_call(
        functools.partial(_ffn_tail_kernel, first=first, last=last, split_row=split_row,
                          n_casts=len(casts)),
        grid=(n_tiles - 1, nf),
        in_specs=in_specs,
        out_specs=out_specs,
        out_shape=out_shape,
        scratch_shapes=[pltpu.VMEM((tm, d), BF16)],
        input_output_aliases={5: 0},
        compiler_params=_params(("arbitrary", "arbitrary")),
        name=name + "_tail",
    )(*args)
    return outs[0], list(outs[1:])


def _a1_kernel(x_ref, g_ref, wa_ref, wg_ref, ba_ref, bg_ref, o_ref, h_ref):
    @pl.when(pl.program_id(1) == 0)
    def _():
        _rms_rows(x_ref, g_ref, h_ref)

    h = h_ref[...]
    a = _dot(h, wa_ref[...]) + ba_ref[...]
    gt = _dot(h, wg_ref[...]) + bg_ref[...]
    o_ref[...] = a * jax.nn.sigmoid(gt)


def _a1(x, g3, w1, b1, layer, j, *, row0, n_rows, tm):
    d = x.shape[1]
    nb, tn = w1.shape[0] // 2, w1.shape[2]
    d_a = nb * tn
    rb0 = row0 // tm
    return pl.pallas_call(
        _a1_kernel,
        grid=(n_rows // tm, nb),
        in_specs=[
            pl.BlockSpec((tm, d), lambda i, n: (rb0 + i, 0)),
            pl.BlockSpec((None, 1, d), lambda i, n: (layer, 0, 0)),
            pl.BlockSpec((None, d, tn), lambda i, n: (n, 0, 0)),
            pl.BlockSpec((None, d, tn), lambda i, n: (n + nb, 0, 0)),
            pl.BlockSpec((None, 1, tn), lambda i, n: (j, 0, n)),
            pl.BlockSpec((None, 1, tn), lambda i, n: (j, 0, n + nb)),
        ],
        out_specs=pl.BlockSpec((tm, tn), lambda i, n: (i, n)),
        out_shape=jax.ShapeDtypeStruct((n_rows, d_a), F32),
        scratch_shapes=[pltpu.VMEM((tm, d), BF16)],
        compiler_params=_params(("parallel", "arbitrary")),
        name=f"mix_a1s_{j}",
    )(x, g3, w1, w1, b1, b1)


def _dwconv_block(xp_ref, w_ref, r0, rows, cols, taps, halo):
    lead = halo - (taps - 1)
    assert halo % SUBLANES == 0 and lead >= 0
    acc = None
    for s in range(SUBLANES):
        group = [k for k in range(taps) if (k + lead) % SUBLANES == s]
        if not group:
            continue
        ext = rows + (SUBLANES if s else 0)
        part = None
        for k in group:
            base = r0 + ((k + lead) // SUBLANES) * SUBLANES
            term = xp_ref[pl.ds(base, ext), cols] * w_ref[k:k + 1, cols]
            part = term if part is None else part + term
        if s:
            part = part[s:s + rows]
        acc = part if acc is None else acc + part
    return acc


def _snake(i, k, nb):
    return jnp.where(i % 2 == 0, k, nb - 1 - k)


def _start_tile(x_ref, g_ref, h_ref, hal_ref, first_in_seq):
    _rms_rows(x_ref, g_ref, h_ref)

    @pl.when(first_in_seq)
    def _():
        hal_ref[...] = jnp.zeros(hal_ref.shape, hal_ref.dtype)


def _af_kernel(x_ref, xr_ref, g_ref, wa_ref, wg_ref, ba_ref, bgt_ref, wdw_ref, bdw_ref, lg_ref,
               lb_ref, w2_ref, o_ref, st_ref, h_ref, gbuf_ref, wbuf_ref, hal_ref, y_ref, act_ref,
               *, taps, tiles_per_seq, nb, rc):
    i, n = pl.program_id(0), pl.program_id(1)
    tm = x_ref.shape[0]
    tn = wa_ref.shape[1]
    halo = hal_ref.shape[0]
    cb = 128

    @pl.when(n == 0)
    def _():
        _start_tile(x_ref, g_ref, h_ref, hal_ref, i % tiles_per_seq == 0)

    @pl.when(n < nb)
    def _():
        col0 = pl.multiple_of(_snake(i, n, nb) * tn, tn)
        gbuf_ref[0:halo, :] = hal_ref[:, pl.ds(col0, tn)]
        wbuf_ref[0:taps, :] = wdw_ref[:, pl.ds(col0, tn)]
        wa, wg = wa_ref[...], wg_ref[...]
        ba, bgt = ba_ref[...], bgt_ref[...]
        for r in range(tm // rc):
            hc = h_ref[r * rc:(r + 1) * rc, :]
            glu = (_dot(hc, wa) + ba) * jax.nn.sigmoid(_dot(hc, wg) + bgt)
            gbuf_ref[halo + r * rc:halo + (r + 1) * rc, :] = glu
            for sub in range(rc // cb):
                r0 = r * rc + sub * cb
                for strip in range(tn // LANES):
                    cols = slice(strip * LANES, (strip + 1) * LANES)
                    y = _dwconv_block(gbuf_ref, wbuf_ref, r0, cb, cols, taps, halo)
                    y_ref[r0:r0 + cb, pl.ds(pl.multiple_of(col0 + strip * LANES, LANES), LANES)] = y
        hal_ref[:, pl.ds(col0, tn)] = gbuf_ref[tm:tm + halo, :]

    @pl.when(n == nb)
    def _():
        w2 = w2_ref[...]
        step, cw = 32, 512
        for r in range(tm // rc):
            for s in range(rc // step):
                blk = slice(r * rc + s * step, r * rc + (s + 1) * step)

                def load(c, blk=blk):
                    cols = slice(c * cw, (c + 1) * cw)
                    return y_ref[blk, cols] + bdw_ref[:, cols]

                def store(c, v, blk=blk):
                    act_ref[blk, c * cw:(c + 1) * cw] = _silu(v).astype(BF16)

                _layernorm_sweeps(load, store, y_ref.shape[1], lg_ref, lb_ref, cw)
            rows = slice(r * rc, (r + 1) * rc)
            o_ref[rows, :] = xr_ref[rows, :] + _dot(act_ref[rows, :], w2)

    @pl.when(n > nb)
    def _():
        o_ref[...] = xr_ref[...] + _dot(act_ref[...], w2_ref[...])

    @pl.when((n == 2 * nb - 1) & (i % tiles_per_seq == tiles_per_seq - 1))
    def _():
        st_ref[...] = hal_ref[halo - (taps - 1):halo, :]


def _a_fused(x, g3, w1, b1, w_dw, b_dw, ln_g, ln_b, w2, layer, j, *, n_prompt, seq, tm):
    m, d = x.shape
    taps = w_dw.shape[1]
    nb, tn = w2.shape[0], w2.shape[2]
    halo = 32
    assert w1.shape == (2 * nb, d, tn) and nb * tn == d and taps - 1 <= halo and seq % tm == 0
    kern = functools.partial(_af_kernel, taps=taps, tiles_per_seq=seq // tm, nb=nb, rc=256)
    taps_pad = -(-taps // SUBLANES) * SUBLANES
    blk1 = lambda i, n: _snake(i, jnp.minimum(n, nb - 1), nb)
    blk2 = lambda i, n: _snake(i, jnp.maximum(n - nb, 0), nb)
    row = lambda i, n: (j, 0, 0)
    return pl.pallas_call(
        kern,
        grid=(n_prompt // tm, 2 * nb),
        in_specs=[
            pl.BlockSpec((tm, d), _read_once_rows(n_prompt // tm)),
            pl.BlockSpec((tm, tn), lambda i, n: (i, blk2(i, n))),
            pl.BlockSpec((None, 1, d), lambda i, n: (layer, 0, 0)),
            pl.BlockSpec((None, d, tn), lambda i, n: (blk1(i, n), 0, 0)),
            pl.BlockSpec((None, d, tn), lambda i, n: (blk1(i, n) + nb, 0, 0)),
            pl.BlockSpec((None, 1, tn), lambda i, n: (j, 0, blk1(i, n))),
            pl.BlockSpec((None, 1, tn), lambda i, n: (j, 0, blk1(i, n) + nb)),
            pl.BlockSpec((None, taps, d), row),
            pl.BlockSpec((None, 1, d), row),
            pl.BlockSpec((None, 1, d), row),
            pl.BlockSpec((None, 1, d), row),
            pl.BlockSpec((None, d, tn), lambda i, n: (blk2(i, n), 0, 0)),
        ],
        out_specs=[
            pl.BlockSpec((tm, tn), lambda i, n: (i, blk2(i, n))),
            pl.BlockSpec((None, taps - 1, d), lambda i, n: (i // (seq // tm), 0, 0)),
        ],
        out_shape=[
            jax.ShapeDtypeStruct((m, d), F32),
            jax.ShapeDtypeStruct((n_prompt // seq, taps - 1, d), F32),
        ],
        scratch_shapes=[
            pltpu.VMEM((tm, d), BF16),
            pltpu.VMEM((halo + tm, tn), F32),
            pltpu.VMEM((taps_pad, tn), F32),
            pltpu.VMEM((halo, d), F32),
            pltpu.VMEM((tm, d), F32),
            pltpu.VMEM((tm, d), BF16),
        ],
        compiler_params=_params(("arbitrary", "arbitrary")),
        name=f"mix_a_{j}",
    )(x, x, g3, w1, w1, b1, b1, w_dw, b_dw, ln_g, ln_b, w2)


def _a_sconv_kernel(*refs):
    st_ref, glu_ref, w_ref, b_ref = refs[:4]
    y_ref, ns_ref = refs[-2:]
    past = st_ref.shape[0]
    acc = st_ref[0] * w_ref[0:1, :]
    for k in range(1, past):
        acc = acc + st_ref[k] * w_ref[k:k + 1, :]
        ns_ref[k - 1] = st_ref[k]
    g = glu_ref[...]
    ns_ref[past - 1] = g
    y_ref[...] = acc + g * w_ref[past:past + 1, :] + b_ref[...]


def _a_sample_conv(state_t, glu, w_dw, b_dw, prev, j, *, tn):
    n_layers, past, nb, d = state_t.shape
    taps = w_dw.shape[1]
    assert taps == past + 1 and glu.shape == (nb, d)
    in_specs = [
        pl.BlockSpec((None, past, nb, tn), lambda n: (j, 0, 0, n)),
        pl.BlockSpec((nb, tn), lambda n: (0, n)),
        pl.BlockSpec((None, taps, tn), lambda n: (j, 0, n)),
        pl.BlockSpec((None, 1, tn), lambda n: (j, 0, n)),
    ]
    args = [state_t, glu, w_dw, b_dw]
    aliases = {}
    if prev is not None:
        in_specs.append(pl.BlockSpec(memory_space=pl.ANY))
        args.append(prev)
        aliases = {4: 1}
    return pl.pallas_call(
        _a_sconv_kernel,
        grid=(d // tn,),
        in_specs=in_specs,
        out_specs=[
            pl.BlockSpec((nb, tn), lambda n: (0, n)),
            pl.BlockSpec((None, past, nb, tn), lambda n: (j, 0, 0, n)),
        ],
        out_shape=[
            jax.ShapeDtypeStruct((nb, d), F32),
            jax.ShapeDtypeStruct((n_layers, past, nb, d), F32),
        ],
        input_output_aliases=aliases,
        compiler_params=_params(("parallel",)),
        name=f"mix_a_sconv_{j}",
    )(*args)


def _a2s_kernel(x_ref, y_ref, lg_ref, lb_ref, w2_ref, _, o_ref, act_ref):
    @pl.when(pl.program_id(0) == 0)
    def _():
        act_ref[...] = _silu(_layernorm(y_ref[...], lg_ref[...], lb_ref[...])).astype(BF16)

    o_ref[...] = x_ref[...] + _dot(act_ref[...], w2_ref[...])


def _a2_sample(x, x_new, y, ln_g, ln_b, w2, j, *, n_prompt):
    m, d = x.shape
    ns = y.shape[0]
    tn = w2.shape[2]
    rb = n_prompt // ns
    return pl.pallas_call(
        _a2s_kernel,
        grid=(d // tn,),
        in_specs=[
            pl.BlockSpec((ns, tn), lambda n: (rb, n)),
            pl.BlockSpec((ns, d), lambda n: (0, 0)),
            pl.BlockSpec((None, 1, d), lambda n: (j, 0, 0)),
            pl.BlockSpec((None, 1, d), lambda n: (j, 0, 0)),
            pl.BlockSpec((None, d, tn), lambda n: (n, 0, 0)),
            pl.BlockSpec(memory_space=pl.ANY),
        ],
        out_specs=pl.BlockSpec((ns, tn), lambda n: (rb, n)),
        out_shape=jax.ShapeDtypeStruct((m, d), F32),
        scratch_shapes=[pltpu.VMEM((ns, d), BF16)],
        input_output_aliases={5: 0},
        compiler_params=_params(("arbitrary",)),
        name=f"mix_a2s_{j}",
    )(x, y, ln_g, ln_b, w2, x_new)


def _b1_kernel(x_ref, g_ref, w_ref, b_ref, lg_ref, lb_ref, o_ref, h_ref):
    n = pl.program_id(1)
    tn = w_ref.shape[1]

    @pl.when(n == 0)
    def _():
        _rms_rows(x_ref, g_ref, h_ref)

    z = _dot(h_ref[...], w_ref[...]) + b_ref[...]
    o_ref[:, pl.ds(pl.multiple_of(n * tn, tn), tn)] = _gelu(z)

    @pl.when(n == pl.num_programs(1) - 1)
    def _():
        rows, step, cw = o_ref.shape[0], 32, 512

        def body(c, carry):
            blk = pl.ds(pl.multiple_of(c * step, step), step)

            def load(k):
                return o_ref[blk, k * cw:(k + 1) * cw]

            def store(k, v):
                o_ref[blk, k * cw:(k + 1) * cw] = v

            _layernorm_sweeps(load, store, o_ref.shape[1], lg_ref, lb_ref, cw)
            return carry

        lax.fori_loop(0, rows // step, body, 0)


def _b1(x, g3, w_in, b_in, ln_g, ln_b, layer, j, *, tm):
    m, d = x.shape
    nb, tn = w_in.shape[0] // 2, w_in.shape[2]
    d_sgu = nb * tn
    return pl.pallas_call(
        _b1_kernel,
        grid=(m // tm, nb),
        in_specs=[
            pl.BlockSpec((tm, d), _read_once_rows(m // tm)),
            pl.BlockSpec((None, 1, d), lambda i, n: (layer, 0, 0)),
            pl.BlockSpec((None, d, tn), lambda i, n: (n + nb, 0, 0)),
            pl.BlockSpec((None, 1, tn), lambda i, n: (j, 0, n + nb)),
            pl.BlockSpec((None, 1, d_sgu), lambda i, n: (j, 0, 0)),
            pl.BlockSpec((None, 1, d_sgu), lambda i, n: (j, 0, 0)),
        ],
        out_specs=pl.BlockSpec((tm, d_sgu), lambda i, n: (i, 0)),
        out_shape=jax.ShapeDtypeStruct((m, d_sgu), F32),
        scratch_shapes=[pltpu.VMEM((tm, d), BF16)],
        compiler_params=_params(("parallel", "arbitrary")),
        name=f"mix_b1_{j}",
    )(x, g3, w_in, b_in, ln_g, ln_b)


def _b2_kernel(x_ref, g_ref, v_ref, wi_ref, bi_ref, ws_ref, bs_ref, wo_ref, o_ref, h_ref,
               *, n_prompt_chunks):
    i = pl.program_id(0)
    tm = x_ref.shape[0]

    @pl.when(pl.program_id(1) == 0)
    def _():
        _rms_rows(x_ref, g_ref, h_ref, copy_ref=o_ref)

    u = _gelu(_dot(h_ref[...], wi_ref[...]) + bi_ref[...])
    ws = ws_ref[...]
    row = lax.broadcasted_iota(jnp.int32, ws.shape, 0)
    col = lax.broadcasted_iota(jnp.int32, ws.shape, 1)
    w_tril = jnp.where(col <= row, ws, 0.0).astype(BF16)
    bs = bs_ref[...]
    parts = []
    for c in range(tm // CHUNK):
        rows = slice(c * CHUNK, (c + 1) * CHUNK)
        v = v_ref[rows, :]
        s_prompt = _dot(w_tril, v.astype(BF16)) + bs
        s_sample = ws[0:1, 0:1] * v + bs[0:1, :]
        is_sample = i * (tm // CHUNK) + c >= n_prompt_chunks
        parts.append(u[rows, :] * jnp.where(is_sample, s_sample, s_prompt))
    y = jnp.concatenate(parts, axis=0).astype(BF16)
    o_ref[...] += _dot(y, wo_ref[...])


def _b2(x, g3, vn, w_in, b_in, w_s, b_s3, w_out, layer, j, *, n_prompt, tm):
    m, d = x.shape
    d_sgu = vn.shape[1]
    gw = d_sgu // N_SGU_GROUPS
    assert w_in.shape == (2 * N_SGU_GROUPS, d, gw) and w_out.shape == (1, d_sgu, d)
    kern = functools.partial(_b2_kernel, n_prompt_chunks=n_prompt // CHUNK)
    return pl.pallas_call(
        kern,
        grid=(m // tm, N_SGU_GROUPS),
        in_specs=[
            pl.BlockSpec((tm, d), _read_once_rows(m // tm)),
            pl.BlockSpec((None, 1, d), lambda i, g: (layer, 0, 0)),
            pl.BlockSpec((tm, gw), lambda i, g: (i, g)),
            pl.BlockSpec((None, d, gw), lambda i, g: (g, 0, 0)),
            pl.BlockSpec((None, 1, gw), lambda i, g: (j, 0, g)),
            pl.BlockSpec((None, None, CHUNK, CHUNK), lambda i, g: (j, g, 0, 0)),
            pl.BlockSpec((None, None, CHUNK, 1), lambda i, g: (j, g, 0, 0)),
            pl.BlockSpec((None, gw, d), lambda i, g: (0, g, 0)),
        ],
        out_specs=pl.BlockSpec((tm, d), lambda i, g: (i, 0)),
        out_shape=jax.ShapeDtypeStruct((m, d), F32),
        scratch_shapes=[pltpu.VMEM((tm, d), BF16)],
        compiler_params=_params(("parallel", "arbitrary")),
        name=f"mix_b2_{j}",
    )(x, g3, vn, w_in, b_in, w_s, b_s3, w_out)


def _c1_kernel(x_ref, g_ref, wb_ref, wc_ref, wx_ref, bg_ref, cx_ref, h_ref):
    @pl.when(pl.program_id(1) == 0)
    def _():
        _rms_rows(x_ref, g_ref, h_ref)

    h = h_ref[...]
    bg_ref[...] = _dot(h, wb_ref[...])
    cx_ref[...] = _dot(h, wc_ref[...]) * _dot(h, wx_ref[...])


def _c1(x, g3, w_in, layer, j, *, row0, n_rows, tm):
    d = x.shape[1]
    nb, tn = w_in.shape[0] // 3, w_in.shape[2]
    d_c = nb * tn
    rb0 = row0 // tm
    out = jax.ShapeDtypeStruct((n_rows, d_c), F32)
    return pl.pallas_call(
        _c1_kernel,
        grid=(n_rows // tm, nb),
        in_specs=[
            pl.BlockSpec((tm, d), lambda i, n: (rb0 + i, 0)),
            pl.BlockSpec((None, 1, d), lambda i, n: (layer, 0, 0)),
            pl.BlockSpec((None, d, tn), lambda i, n: (n, 0, 0)),
            pl.BlockSpec((None, d, tn), lambda i, n: (n + nb, 0, 0)),
            pl.BlockSpec((None, d, tn), lambda i, n: (n + 2 * nb, 0, 0)),
        ],
        out_specs=[pl.BlockSpec((tm, tn), lambda i, n: (i, n))] * 2,
        out_shape=[out, out],
        scratch_shapes=[pltpu.VMEM((tm, d), BF16)],
        compiler_params=_params(("parallel", "arbitrary")),
        name=f"mix_c1s_{j}",
    )(x, g3, w_in, w_in, w_in)


def _cf_kernel(x_ref, xr_ref, g_ref, wb_ref, wc_ref, wx_ref, wcv_ref, wo_ref, o_ref, st_ref,
               h_ref, gbuf_ref, wbuf_ref, hal_ref, act_ref, *, taps, tiles_per_seq, nb, rc):
    i, n = pl.program_id(0), pl.program_id(1)
    tm = x_ref.shape[0]
    tn = wb_ref.shape[1]
    halo = hal_ref.shape[0]
    cb = 128

    @pl.when(n == 0)
    def _():
        _start_tile(x_ref, g_ref, h_ref, hal_ref, i % tiles_per_seq == 0)

    @pl.when(n < nb)
    def _():
        col0 = pl.multiple_of(_snake(i, n, nb) * tn, tn)
        gbuf_ref[0:halo, :] = hal_ref[:, pl.ds(col0, tn)]
        wbuf_ref[0:taps, :] = wcv_ref[:, pl.ds(col0, tn)]
        wb, wc, wx = wb_ref[...], wc_ref[...], wx_ref[...]
        for r in range(tm // rc):
            hc = h_ref[r * rc:(r + 1) * rc, :]
            bg = _dot(hc, wb)
            gbuf_ref[halo + r * rc:halo + (r + 1) * rc, :] = _dot(hc, wc) * _dot(hc, wx)
            for sub in range(rc // cb):
                r0 = r * rc + sub * cb
                for strip in range(tn // LANES):
                    cols = slice(strip * LANES, (strip + 1) * LANES)
                    y = _dwconv_block(gbuf_ref, wbuf_ref, r0, cb, cols, taps, halo)
                    gate = bg[sub * cb:(sub + 1) * cb, cols]
                    act_ref[r0:r0 + cb, pl.ds(pl.multiple_of(col0 + strip * LANES, LANES), LANES)] = (
                        gate * y).astype(BF16)
        hal_ref[:, pl.ds(col0, tn)] = gbuf_ref[tm:tm + halo, :]

    @pl.when(n >= nb)
    def _():
        o_ref[...] = xr_ref[...] + _dot(act_ref[...], wo_ref[...])

    @pl.when((n == 2 * nb - 1) & (i % tiles_per_seq == tiles_per_seq - 1))
    def _():
        st_ref[...] = hal_ref[halo - (taps - 1):halo, :]


def _c_fused(x, g3, w_in, w_conv, w_out, layer, j, *, n_prompt, seq, tm):
    m, d = x.shape
    taps = w_conv.shape[1]
    nb, tn = w_out.shape[0], w_out.shape[2]
    halo = SUBLANES
    assert w_in.shape == (3 * nb, d, tn) and nb * tn == d and taps - 1 <= halo and seq % tm == 0
    kern = functools.partial(_cf_kernel, taps=taps, tiles_per_seq=seq // tm, nb=nb, rc=256)
    taps_pad = -(-taps // SUBLANES) * SUBLANES
    blk1 = lambda i, n: _snake(i, jnp.minimum(n, nb - 1), nb)
    blk2 = lambda i, n: _snake(i, jnp.maximum(n - nb, 0), nb)
    blk = lambda part: (lambda i, n: (blk1(i, n) + part * nb, 0, 0))
    return pl.pallas_call(
        kern,
        grid=(n_prompt // tm, 2 * nb),
        in_specs=[
            pl.BlockSpec((tm, d), _read_once_rows(n_prompt // tm)),
            pl.BlockSpec((tm, tn), lambda i, n: (i, blk2(i, n))),
            pl.BlockSpec((None, 1, d), lambda i, n: (layer, 0, 0)),
            pl.BlockSpec((None, d, tn), blk(0)),
            pl.BlockSpec((None, d, tn), blk(1)),
            pl.BlockSpec((None, d, tn), blk(2)),
            pl.BlockSpec((None, taps, d), lambda i, n: (j, 0, 0)),
            pl.BlockSpec((None, d, tn), lambda i, n: (blk2(i, n), 0, 0)),
        ],
        out_specs=[
            pl.BlockSpec((tm, tn), lambda i, n: (i, blk2(i, n))),
            pl.BlockSpec((None, taps - 1, d), lambda i, n: (i // (seq // tm), 0, 0)),
        ],
        out_shape=[
            jax.ShapeDtypeStruct((m, d), F32),
            jax.ShapeDtypeStruct((n_prompt // seq, taps - 1, d), F32),
        ],
        scratch_shapes=[
            pltpu.VMEM((tm, d), BF16),
            pltpu.VMEM((halo + tm, tn), F32),
            pltpu.VMEM((taps_pad, tn), F32),
            pltpu.VMEM((halo, d), F32),
            pltpu.VMEM((tm, d), BF16),
        ],
        compiler_params=_params(("arbitrary", "arbitrary")),
        name=f"mix_c_{j}",
    )(x, x, g3, w_in, w_in, w_in, w_conv, w_out)


def _c2s_kernel(x_ref, bg_ref, cx_ref, s0_ref, s1_ref, w_ref, wo_ref, _, o_ref, act_ref):
    @pl.when(pl.program_id(0) == 0)
    def _():
        w = w_ref[...]
        y = s0_ref[...] * w[0:1] + s1_ref[...] * w[1:2] + cx_ref[...] * w[2:3]
        act_ref[...] = (bg_ref[...] * y).astype(BF16)

    o_ref[...] = x_ref[...] + _dot(act_ref[...], wo_ref[...])


def _c2_sample(x, x_new, bg, cx, s0, s1, w_conv, w_out, j, *, n_prompt):
    m, d = x.shape
    ns = s0.shape[0]
    tn = w_out.shape[2]
    rb = n_prompt // ns
    taps = w_conv.shape[1]
    assert taps == 3
    return pl.pallas_call(
        _c2s_kernel,
        grid=(d // tn,),
        in_specs=[
            pl.BlockSpec((ns, tn), lambda n: (rb, n)),
            pl.BlockSpec((ns, d), lambda n: (0, 0)),
            pl.BlockSpec((ns, d), lambda n: (0, 0)),
            pl.BlockSpec((ns, d), lambda n: (0, 0)),
            pl.BlockSpec((ns, d), lambda n: (0, 0)),
            pl.BlockSpec((None, taps, d), lambda n: (j, 0, 0)),
            pl.BlockSpec((None, d, tn), lambda n: (n, 0, 0)),
            pl.BlockSpec(memory_space=pl.ANY),
        ],
        out_specs=pl.BlockSpec((ns, tn), lambda n: (rb, n)),
        out_shape=jax.ShapeDtypeStruct((m, d), F32),
        scratch_shapes=[pltpu.VMEM((ns, d), BF16)],
        input_output_aliases={7: 0},
        compiler_params=_params(("arbitrary",)),
        name=f"mix_c2s_{j}",
    )(x, bg, cx, s0, s1, w_conv, w_out, x_new)


def _last_rows(a, batch, seq, rows):
    return jnp.stack([a[(b + 1) * seq - rows:(b + 1) * seq] for b in range(batch)])


def kernel(x_prompt, x_sample, state_conv_a, state_conv_c, g_ffn1, g_mix, g_ffn2, g_final,
           w_ffn_gate, w_ffn_up, w_ffn_down,
           a_w_pw1, a_b_pw1, a_w_dw, a_b_dw, a_ln_g, a_ln_b, a_w_pw2,
           b_w_in, b_b_in, b_ln_g, b_ln_b, b_w_s, b_b_s, b_w_out,
           c_w_in, c_w_conv, c_w_out):
    batch, seq, d = x_prompt.shape
    n_sample = x_sample.shape[0]
    assert x_sample.shape[1] == 1
    n_prompt = batch * seq
    depth = g_ffn1.shape[0]
    past_c = state_conv_c.shape[2]
    assert past_c == 2
    d_sgu = b_ln_g.shape[-1]

    row3 = lambda a: a.reshape(a.shape[0], 1, a.shape[1])
    g1, gm, g2 = row3(g_ffn1), row3(g_mix), row3(g_ffn2)
    gf = g_final.reshape(1, d)
    a_b1, a_bd, a_lg, a_lb = row3(a_b_pw1), row3(a_b_dw), row3(a_ln_g), row3(a_ln_b)
    b_bi, b_lg, b_lb = row3(b_b_in), row3(b_ln_g), row3(b_ln_b)
    b_bs = b_b_s.reshape(*b_b_s.shape, 1)
    state_a_t = jnp.transpose(state_conv_a, (0, 2, 1, 3))

    ffn = functools.partial(_ffn, n_prompt=n_prompt, n_sample=n_sample,
                            tm=832, tf_head=512, tf_tail=512)
    new_a_p, new_b_p, new_b_s, new_c_p, new_c_s = [], [], [], [], []
    new_a_s_t = None

    x = x_prompt.reshape(n_prompt, d)
    xs = x_sample.reshape(n_sample, d)
    gw = d_sgu // N_SGU_GROUPS
    for i in range(depth):
        kind, j = i % 3, i // 3
        if kind == 0:
            casts = [(a_w_pw1, j, 32, 512), (a_w_pw2, j, 32, 512)]
        elif kind == 1:
            casts = [(b_w_in, j, 32, gw), (b_w_out, j, 64, d)]
        else:
            casts = [(c_w_in, j, 32, 512), (c_w_out, j, 32, 512)]
        x, (w_a16, w_b16) = ffn(x, xs, g1, gf, w_ffn_gate, w_ffn_up, w_ffn_down, i, 0, casts,
                                first=(i == 0), last=False)
        if kind == 0:
            glu_s = _a1(x, gm, w_a16, a_b1, i, j, row0=n_prompt, n_rows=n_sample, tm=n_sample)
            y_s, new_a_s_t = _a_sample_conv(state_a_t, glu_s, a_w_dw, a_bd, new_a_s_t, j, tn=256)
            x_new, st_p = _a_fused(x, gm, w_a16, a_b1, a_w_dw, a_bd, a_lg, a_lb, w_b16, i, j,
                                   n_prompt=n_prompt, seq=seq, tm=1024)
            x = _a2_sample(x, x_new, y_s, a_lg, a_lb, w_b16, j, n_prompt=n_prompt)
            new_a_p.append(st_p)
        elif kind == 1:
            vn = _b1(x, gm, w_a16, b_bi, b_lg, b_lb, i, j, tm=832)
            x = _b2(x, gm, vn, w_a16, b_bi, b_w_s, b_bs, w_b16, i, j, n_prompt=n_prompt, tm=640)
            new_b_p.append(_last_rows(vn, batch, seq, CHUNK))
            new_b_s.append(vn[n_prompt:].reshape(n_sample, 1, d_sgu))
        else:
            bg_s, cx_s = _c1(x, gm, w_a16, i, j, row0=n_prompt, n_rows=n_sample, tm=n_sample)
            s0, s1 = state_conv_c[j, :, 0], state_conv_c[j, :, 1]
            x_new, st_p = _c_fused(x, gm, w_a16, c_w_conv, w_b16, i, j,
                                   n_prompt=n_prompt, seq=seq, tm=1024)
            x = _c2_sample(x, x_new, bg_s, cx_s, s0, s1, c_w_conv, w_b16, j, n_prompt=n_prompt)
            new_c_p.append(st_p)
            new_c_s.append(jnp.stack([s1, cx_s], axis=1))
        last = i == depth - 1
        x, _ = ffn(x, xs, g2, gf, w_ffn_gate, w_ffn_up, w_ffn_down, i, 1, [],
                   first=False, last=last)

    return (x[:n_prompt].reshape(batch, seq, d), x[n_prompt:].reshape(n_sample, 1, d),
            jnp.stack(new_a_p), jnp.transpose(new_a_s_t, (0, 2, 1, 3)),
            jnp.stack(new_b_p), jnp.stack(new_b_s),
            jnp.stack(new_c_p), jnp.stack(new_c_s))
```

```python
import functools

import jax
import jax.numpy as jnp
from jax import lax
from jax.experimental import pallas as pl
from jax.experimental.pallas import tpu as pltpu

F32 = jnp.float32
BF16 = jnp.bfloat16

EPS = 1e-6
FFN_HALF = 0.5
CHUNK = 128
N_SGU_GROUPS = 8
INV_SQRT2 = 0.7071067811865476
SUBLANES = 8
LANES = 128

V7X_VMEM_BYTES = 64 * 1024 * 1024
VMEM_LIMIT = V7X_VMEM_BYTES - 6 * 1024 * 1024


def _params(sem):
    return pltpu.CompilerParams(dimension_semantics=sem, vmem_limit_bytes=VMEM_LIMIT)


def _read_once_rows(n_tiles):
    return lambda i, s: (jnp.minimum(i + jnp.minimum(s, 1), n_tiles - 1), 0)


def _rms(x, g):
    ms = jnp.mean(x * x, axis=-1, keepdims=True)
    return (x * lax.rsqrt(ms + EPS)) * g


def _layernorm(y, g, b):
    mu = jnp.mean(y, axis=-1, keepdims=True)
    yc = y - mu
    var = jnp.mean(yc * yc, axis=-1, keepdims=True)
    return (yc * lax.rsqrt(var + EPS)) * g + b


def _layernorm_sweeps(load, store, width, g_ref, b_ref, cw=512):
    def lane_sum(v):
        acc = v[:, 0:LANES]
        for t in range(1, cw // LANES):
            acc = acc + v[:, t * LANES:(t + 1) * LANES]
        return acc

    n = width // cw
    acc = lane_sum(load(0))
    for c in range(1, n):
        acc = acc + lane_sum(load(c))
    mu = jnp.sum(acc, axis=-1, keepdims=True) * (1.0 / width)
    acc = None
    for c in range(n):
        dev = load(c) - mu
        sq = lane_sum(dev * dev)
        acc = sq if acc is None else acc + sq
    var = jnp.sum(acc, axis=-1, keepdims=True) * (1.0 / width)
    scale = lax.rsqrt(var + EPS)
    for c in range(n):
        cols = slice(c * cw, (c + 1) * cw)
        store(c, ((load(c) - mu) * scale) * g_ref[:, cols] + b_ref[:, cols])


def _silu(x):
    return x * jax.nn.sigmoid(x)


def _gelu(x):
    return 0.5 * x * (1.0 + lax.erf(x * INV_SQRT2))


def _dot(a, w):
    return jnp.dot(a, w, preferred_element_type=F32)


def _rms_rows(x_ref, g_ref, h_ref, copy_ref=None, rows_per_step=64):
    rows = x_ref.shape[0]
    step = rows_per_step
    assert rows % step == 0
    g = g_ref[...]

    def body(c, carry):
        r0 = pl.multiple_of(c * step, step)
        x = x_ref[pl.ds(r0, step), :]
        h_ref[pl.ds(r0, step), :] = _rms(x, g).astype(BF16)
        if copy_ref is not None:
            copy_ref[pl.ds(r0, step), :] = x
        return carry

    lax.fori_loop(0, rows // step, body, 0)


def _rms_rows_inplace(o_ref, g_ref, rows_per_step=64):
    rows, step = o_ref.shape[0], rows_per_step
    g = g_ref[...]

    def body(c, carry):
        r0 = pl.multiple_of(c * step, step)
        o_ref[pl.ds(r0, step), :] = _rms(o_ref[pl.ds(r0, step), :], g)
        return carry

    lax.fori_loop(0, rows // step, body, 0)


def _ffn_rows(h, wg, wu, wd):
    gate = _dot(h, wg)
    up = _dot(h, wu)
    act = (_silu(gate) * (up * FFN_HALF)).astype(BF16)
    return _dot(act, wd)


def _ffn_step(h_ref, wg, wu, wd, o_ref):
    o_ref[...] += _ffn_rows(h_ref[...], wg, wu, wd)


def _ffn_first_step(x_ref, g_ref, h_ref, wg, wu, wd, o_ref, tail, rc, sub=16):
    tm = x_ref.shape[0]
    g = g_ref[...]
    for r in range(tm // rc):
        for s in range(rc // sub):
            r0 = r * rc + s * sub
            x = x_ref[r0:r0 + sub, :]
            if tail is not None and r0 >= tail[1]:
                use_tail, split_row, tail_ref = tail
                x = jnp.where(use_tail, tail_ref[r0 - split_row:r0 - split_row + sub, :], x)
            h_ref[r0:r0 + sub, :] = _rms(x, g).astype(BF16)
            o_ref[r0:r0 + sub, :] = x
        rows = slice(r * rc, (r + 1) * rc)
        o_ref[rows, :] += _ffn_rows(h_ref[rows, :], wg, wu, wd)


def _ffn_head_kernel(x_ref, g_ref, gf_ref, wg_ref, wu_ref, wd_ref,
                     o_ref, wg16_ref, wu16_ref, wd16_ref, h_ref, *, final_norm):
    f = pl.program_id(0)

    @pl.when(f == 0)
    def _():
        _rms_rows(x_ref, g_ref, h_ref, copy_ref=o_ref)

    wg = wg_ref[...].astype(BF16)
    wu = wu_ref[...].astype(BF16)
    wd = wd_ref[...].astype(BF16)
    wg16_ref[...] = wg
    wu16_ref[...] = wu
    wd16_ref[...] = wd
    _ffn_step(h_ref, wg, wu, wd, o_ref)

    if final_norm:
        @pl.when(f == pl.num_programs(0) - 1)
        def _():
            _rms_rows_inplace(o_ref, gf_ref)


def _ffn_tail_kernel(*refs, first, last, split_row, n_casts):
    refs = list(refs)
    x_ref, g_ref, gf_ref, wg_ref, wu_ref, wd_ref, _ = refs[:7]
    xs_ref = refs[7] if first else None
    n_in = 8 if first else 7
    cast_src = refs[n_in:n_in + n_casts]
    o_ref = refs[n_in + n_casts]
    cast_dst = refs[n_in + n_casts + 1:n_in + 2 * n_casts + 1]
    h_ref = refs[-1]
    i, f = pl.program_id(0), pl.program_id(1)
    last_tile = i == pl.num_programs(0) - 1

    for src, dst in zip(cast_src, cast_dst):
        tn = dst.shape[2]
        for c in range(dst.shape[0]):
            dst[c] = src[:, c * tn:(c + 1) * tn].astype(BF16)

    @pl.when(f == 0)
    def _():
        tail = (last_tile, split_row, xs_ref) if first else None
        _ffn_first_step(x_ref, g_ref, h_ref, wg_ref[...], wu_ref[...], wd_ref[...], o_ref,
                        tail, rc=x_ref.shape[0] // 4)

    @pl.when(f > 0)
    def _():
        _ffn_step(h_ref, wg_ref[...], wu_ref[...], wd_ref[...], o_ref)

    if last:
        @pl.when(f == pl.num_programs(1) - 1)
        def _():
            _rms_rows_inplace(o_ref, gf_ref)


def _ffn(x, xs, g3, gf2, w_gate, w_up, w_down, layer, which, casts, *, n_prompt, n_sample,
         first, last, tm, tf_head, tf_tail):
    d = x.shape[1]
    dff = w_gate.shape[-1]
    m = n_prompt + n_sample
    assert m % tm == 0
    n_tiles = m // tm
    split_row = n_prompt - (n_tiles - 1) * tm
    assert 0 < split_row < tm and split_row % 64 == 0 and tm - split_row == n_sample
    out_rows = m
    name = f"ffn_{layer}_{which}"

    once = pl.Buffered(1)
    nf_head = dff // tf_head
    assert tf_head == tf_tail
    head_out, wg16, wu16, wd16 = pl.pallas_call(
        functools.partial(_ffn_head_kernel, final_norm=last),
        grid=(nf_head,),
        in_specs=[
            pl.BlockSpec((tm, d), lambda f: (0, 0), pipeline_mode=once),
            pl.BlockSpec((None, 1, d), lambda f: (layer, 0, 0)),
            pl.BlockSpec((1, d), lambda f: (0, 0)),
            pl.BlockSpec((None, None, d, tf_head), lambda f: (layer, which, 0, f)),
            pl.BlockSpec((None, None, d, tf_head), lambda f: (layer, which, 0, f)),
            pl.BlockSpec((None, None, tf_head, d), lambda f: (layer, which, f, 0)),
        ],
        out_specs=[
            pl.BlockSpec((tm, d), lambda f: (0, 0), pipeline_mode=once),
            pl.BlockSpec((None, d, tf_head), lambda f: (f, 0, 0)),
            pl.BlockSpec((None, d, tf_head), lambda f: (f, 0, 0)),
            pl.BlockSpec((None, tf_head, d), lambda f: (f, 0, 0)),
        ],
        out_shape=[
            jax.ShapeDtypeStruct((out_rows, d), F32),
            jax.ShapeDtypeStruct((nf_head, d, tf_head), BF16),
            jax.ShapeDtypeStruct((nf_head, d, tf_head), BF16),
            jax.ShapeDtypeStruct((nf_head, tf_head, d), BF16),
        ],
        scratch_shapes=[pltpu.VMEM((tm, d), BF16)],
        compiler_params=_params(("arbitrary",)),
        name=name + "_head",
    )(x, g3, gf2, w_gate, w_up, w_down)

    in_specs = [
        pl.BlockSpec((tm, d), lambda i, f: (i + 1, 0)),
        pl.BlockSpec((None, 1, d), lambda i, f: (layer, 0, 0)),
        pl.BlockSpec((1, d), lambda i, f: (0, 0)),
        pl.BlockSpec((None, d, tf_tail), lambda i, f: (f, 0, 0)),
        pl.BlockSpec((None, d, tf_tail), lambda i, f: (f, 0, 0)),
        pl.BlockSpec((None, tf_tail, d), lambda i, f: (f, 0, 0)),
        pl.BlockSpec(memory_space=pl.ANY),
    ]
    args = [x, g3, gf2, wg16, wu16, wd16, head_out]
    if first:
        in_specs.append(pl.BlockSpec((n_sample, d), lambda i, f: (0, 0)))
        args.append(xs)
    nf = dff // tf_tail
    out_specs = [pl.BlockSpec((tm, d), lambda i, f: (i + 1, 0))]
    out_shape = [jax.ShapeDtypeStruct((out_rows, d), F32)]
    for w, lj, rb, tn in casts:
        _, rows, cols = w.shape
        n_blk = rows // rb
        assert rows % rb == 0 and cols % tn == 0 and n_blk <= (n_tiles - 1) * nf
        blk = lambda i, f, n_blk=n_blk: jnp.minimum(i * nf + f, n_blk - 1)
        in_specs.append(pl.BlockSpec((None, rb, cols), lambda i, f, lj=lj, blk=blk: (lj, blk(i, f), 0)))
        args.append(w)
        out_specs.append(pl.BlockSpec((cols // tn, rb, tn), lambda i, f, blk=blk: (0, blk(i, f), 0)))
        out_shape.append(jax.ShapeDtypeStruct((cols // tn, rows, tn), BF16))
    outs = pl.pallas_call(
        functools.partial(_ffn_tail_kernel, first=first, last=last, split_row=split_row,
                          n_casts=len(casts)),
        grid=(n_tiles - 1, nf),
        in_specs=in_specs,
        out_specs=out_specs,
        out_shape=out_shape,
        scratch_shapes=[pltpu.VMEM((tm, d), BF16)],
        input_output_aliases={6: 0},
        compiler_params=_params(("arbitrary", "arbitrary")),
        name=name + "_tail",
    )(*args)
    return outs[0], list(outs[1:])


def _a1_kernel(x_ref, g_ref, wa_ref, wg_ref, ba_ref, bg_ref, o_ref, h_ref):
    @pl.when(pl.program_id(1) == 0)
    def _():
        _rms_rows(x_ref, g_ref, h_ref)

    h = h_ref[...]
    a = _dot(h, wa_ref[...]) + ba_ref[...]
    gt = _dot(h, wg_ref[...]) + bg_ref[...]
    o_ref[...] = a * jax.nn.sigmoid(gt)


def _a1(x, g3, w1, b1, layer, j, *, row0, n_rows, tm):
    d = x.shape[1]
    nb, tn = w1.shape[0] // 2, w1.shape[2]
    d_a = nb * tn
    rb0 = row0 // tm
    return pl.pallas_call(
        _a1_kernel,
        grid=(n_rows // tm, nb),
        in_specs=[
            pl.BlockSpec((tm, d), lambda i, n: (rb0 + i, 0)),
            pl.BlockSpec((None, 1, d), lambda i, n: (layer, 0, 0)),
            pl.BlockSpec((None, d, tn), lambda i, n: (n, 0, 0)),
            pl.BlockSpec((None, d, tn), lambda i, n: (n + nb, 0, 0)),
            pl.BlockSpec((None, 1, tn), lambda i, n: (j, 0, n)),
            pl.BlockSpec((None, 1, tn), lambda i, n: (j, 0, n + nb)),
        ],
        out_specs=pl.BlockSpec((tm, tn), lambda i, n: (i, n)),
        out_shape=jax.ShapeDtypeStruct((n_rows, d_a), F32),
        scratch_shapes=[pltpu.VMEM((tm, d), BF16)],
        compiler_params=_params(("parallel", "arbitrary")),
        name=f"mix_a1s_{j}",
    )(x, g3, w1, w1, b1, b1)


def _dwconv_block(xp_ref, w_ref, r0, rows, cols, taps, halo):
    lead = halo - (taps - 1)
    assert halo % SUBLANES == 0 and lead >= 0
    acc = None
    for s in range(SUBLANES):
        group = [k for k in range(taps) if (k + lead) % SUBLANES == s]
        if not group:
            continue
        ext = rows + (SUBLANES if s else 0)
        part = None
        for k in group:
            base = r0 + ((k + lead) // SUBLANES) * SUBLANES
            term = xp_ref[pl.ds(base, ext), cols] * w_ref[k:k + 1, cols]
            part = term if part is None else part + term
        if s:
            part = part[s:s + rows]
        acc = part if acc is None else acc + part
    return acc


def _snake(i, k, nb):
    return jnp.where(i % 2 == 0, k, nb - 1 - k)


def _start_tile(x_ref, g_ref, h_ref, hal_ref, first_in_seq):
    _rms_rows(x_ref, g_ref, h_ref)

    @pl.when(first_in_seq)
    def _():
        hal_ref[...] = jnp.zeros(hal_ref.shape, hal_ref.dtype)


def _af_kernel(x_ref, xr_ref, g_ref, wa_ref, wg_ref, ba_ref, bgt_ref, wdw_ref, bdw_ref, lg_ref,
               lb_ref, w2_ref, o_ref, st_ref, h_ref, gbuf_ref, wbuf_ref, hal_ref, y_ref, act_ref,
               *, taps, tiles_per_seq, nb, rc):
    i, n = pl.program_id(0), pl.program_id(1)
    tm = x_ref.shape[0]
    tn = wa_ref.shape[1]
    halo = hal_ref.shape[0]
    cb = 128

    @pl.when(n == 0)
    def _():
        _start_tile(x_ref, g_ref, h_ref, hal_ref, i % tiles_per_seq == 0)

    @pl.when(n < nb)
    def _():
        col0 = pl.multiple_of(_snake(i, n, nb) * tn, tn)
        gbuf_ref[0:halo, :] = hal_ref[:, pl.ds(col0, tn)]
        wbuf_ref[0:taps, :] = wdw_ref[:, pl.ds(col0, tn)]
        wa, wg = wa_ref[...], wg_ref[...]
        ba, bgt = ba_ref[...], bgt_ref[...]
        for r in range(tm // rc):
            hc = h_ref[r * rc:(r + 1) * rc, :]
            glu = (_dot(hc, wa) + ba) * jax.nn.sigmoid(_dot(hc, wg) + bgt)
            gbuf_ref[halo + r * rc:halo + (r + 1) * rc, :] = glu
            for sub in range(rc // cb):
                r0 = r * rc + sub * cb
                for strip in range(tn // LANES):
                    cols = slice(strip * LANES, (strip + 1) * LANES)
                    y = _dwconv_block(gbuf_ref, wbuf_ref, r0, cb, cols, taps, halo)
                    y_ref[r0:r0 + cb, pl.ds(pl.multiple_of(col0 + strip * LANES, LANES), LANES)] = y
        hal_ref[:, pl.ds(col0, tn)] = gbuf_ref[tm:tm + halo, :]

    @pl.when(n == nb)
    def _():
        w2 = w2_ref[...]
        step, cw = 32, 512
        for r in range(tm // rc):
            for s in range(rc // step):
                blk = slice(r * rc + s * step, r * rc + (s + 1) * step)

                def load(c, blk=blk):
                    cols = slice(c * cw, (c + 1) * cw)
                    return y_ref[blk, cols] + bdw_ref[:, cols]

                def store(c, v, blk=blk):
                    act_ref[blk, c * cw:(c + 1) * cw] = _silu(v).astype(BF16)

                _layernorm_sweeps(load, store, y_ref.shape[1], lg_ref, lb_ref, cw)
            rows = slice(r * rc, (r + 1) * rc)
            o_ref[rows, :] = xr_ref[rows, :] + _dot(act_ref[rows, :], w2)

    @pl.when(n > nb)
    def _():
        o_ref[...] = xr_ref[...] + _dot(act_ref[...], w2_ref[...])

    @pl.when((n == 2 * nb - 1) & (i % tiles_per_seq == tiles_per_seq - 1))
    def _():
        st_ref[...] = hal_ref[halo - (taps - 1):halo, :]


def _a_fused(x, g3, w1, b1, w_dw, b_dw, ln_g, ln_b, w2, layer, j, *, n_prompt, seq, tm):
    m, d = x.shape
    taps = w_dw.shape[1]
    nb, tn = w2.shape[0], w2.shape[2]
    halo = 32
    assert w1.shape == (2 * nb, d, tn) and nb * tn == d and taps - 1 <= halo and seq % tm == 0
    kern = functools.partial(_af_kernel, taps=taps, tiles_per_seq=seq // tm, nb=nb, rc=256)
    taps_pad = -(-taps // SUBLANES) * SUBLANES
    blk1 = lambda i, n: _snake(i, jnp.minimum(n, nb - 1), nb)
    blk2 = lambda i, n: _snake(i, jnp.maximum(n - nb, 0), nb)
    row = lambda i, n: (j, 0, 0)
    return pl.pallas_call(
        kern,
        grid=(n_prompt // tm, 2 * nb),
        in_specs=[
            pl.BlockSpec((tm, d), _read_once_rows(n_prompt // tm)),
            pl.BlockSpec((tm, tn), lambda i, n: (i, blk2(i, n))),
            pl.BlockSpec((None, 1, d), lambda i, n: (layer, 0, 0)),
            pl.BlockSpec((None, d, tn), lambda i, n: (blk1(i, n), 0, 0)),
            pl.BlockSpec((None, d, tn), lambda i, n: (blk1(i, n) + nb, 0, 0)),
            pl.BlockSpec((None, 1, tn), lambda i, n: (j, 0, blk1(i, n))),
            pl.BlockSpec((None, 1, tn), lambda i, n: (j, 0, blk1(i, n) + nb)),
            pl.BlockSpec((None, taps, d), row),
            pl.BlockSpec((None, 1, d), row),
            pl.BlockSpec((None, 1, d), row),
            pl.BlockSpec((None, 1, d), row),
            pl.BlockSpec((None, d, tn), lambda i, n: (blk2(i, n), 0, 0)),
        ],
        out_specs=[
            pl.BlockSpec((tm, tn), lambda i, n: (i, blk2(i, n))),
            pl.BlockSpec((None, taps - 1, d), lambda i, n: (i // (seq // tm), 0, 0)),
        ],
        out_shape=[
            jax.ShapeDtypeStruct((m, d), F32),
            jax.ShapeDtypeStruct((n_prompt // seq, taps - 1, d), F32),
        ],
        scratch_shapes=[
            pltpu.VMEM((tm, d), BF16),
            pltpu.VMEM((halo + tm, tn), F32),
            pltpu.VMEM((taps_pad, tn), F32),
            pltpu.VMEM((halo, d), F32),
            pltpu.VMEM((tm, d), F32),
            pltpu.VMEM((tm, d), BF16),
        ],
        compiler_params=_params(("arbitrary", "arbitrary")),
        name=f"mix_a_{j}",
    )(x, x, g3, w1, w1, b1, b1, w_dw, b_dw, ln_g, ln_b, w2)


def _a_sconv_kernel(*refs):
    st_ref, glu_ref, w_ref, b_ref = refs[:4]
    y_ref, ns_ref = refs[-2:]
    past = st_ref.shape[0]
    acc = st_ref[0] * w_ref[0:1, :]
    for k in range(1, past):
        acc = acc + st_ref[k] * w_ref[k:k + 1, :]
        ns_ref[k - 1] = st_ref[k]
    g = glu_ref[...]
    ns_ref[past - 1] = g
    y_ref[...] = acc + g * w_ref[past:past + 1, :] + b_ref[...]


def _a_sample_conv(state_t, glu, w_dw, b_dw, prev, j, *, tn):
    n_layers, past, nb, d = state_t.shape
    taps = w_dw.shape[1]
    assert taps == past + 1 and glu.shape == (nb, d)
    in_specs = [
        pl.BlockSpec((None, past, nb, tn), lambda n: (j, 0, 0, n)),
        pl.BlockSpec((nb, tn), lambda n: (0, n)),
        pl.BlockSpec((None, taps, tn), lambda n: (j, 0, n)),
        pl.BlockSpec((None, 1, tn), lambda n: (j, 0, n)),
    ]
    args = [state_t, glu, w_dw, b_dw]
    aliases = {}
    if prev is not None:
        in_specs.append(pl.BlockSpec(memory_space=pl.ANY))
        args.append(prev)
        aliases = {4: 1}
    return pl.pallas_call(
        _a_sconv_kernel,
        grid=(d // tn,),
        in_specs=in_specs,
        out_specs=[
            pl.BlockSpec((nb, tn), lambda n: (0, n)),
            pl.BlockSpec((None, past, nb, tn), lambda n: (j, 0, 0, n)),
        ],
        out_shape=[
            jax.ShapeDtypeStruct((nb, d), F32),
            jax.ShapeDtypeStruct((n_layers, past, nb, d), F32),
        ],
        input_output_aliases=aliases,
        compiler_params=_params(("parallel",)),
        name=f"mix_a_sconv_{j}",
    )(*args)


def _a2s_kernel(x_ref, y_ref, lg_ref, lb_ref, w2_ref, _, o_ref, act_ref):
    @pl.when(pl.program_id(0) == 0)
    def _():
        act_ref[...] = _silu(_layernorm(y_ref[...], lg_ref[...], lb_ref[...])).astype(BF16)

    o_ref[...] = x_ref[...] + _dot(act_ref[...], w2_ref[...])


def _a2_sample(x, x_new, y, ln_g, ln_b, w2, j, *, n_prompt):
    m, d = x.shape
    ns = y.shape[0]
    tn = w2.shape[2]
    rb = n_prompt // ns
    return pl.pallas_call(
        _a2s_kernel,
        grid=(d // tn,),
        in_specs=[
            pl.BlockSpec((ns, tn), lambda n: (rb, n)),
            pl.BlockSpec((ns, d), lambda n: (0, 0)),
            pl.BlockSpec((None, 1, d), lambda n: (j, 0, 0)),
            pl.BlockSpec((None, 1, d), lambda n: (j, 0, 0)),
            pl.BlockSpec((None, d, tn), lambda n: (n, 0, 0)),
            pl.BlockSpec(memory_space=pl.ANY),
        ],
        out_specs=pl.BlockSpec((ns, tn), lambda n: (rb, n)),
        out_shape=jax.ShapeDtypeStruct((m, d), F32),
        scratch_shapes=[pltpu.VMEM((ns, d), BF16)],
        input_output_aliases={5: 0},
        compiler_params=_params(("arbitrary",)),
        name=f"mix_a2s_{j}",
    )(x, y, ln_g, ln_b, w2, x_new)


def _b1_kernel(x_ref, g_ref, w_ref, b_ref, lg_ref, lb_ref, o_ref, h_ref):
    n = pl.program_id(1)
    tn = w_ref.shape[1]

    @pl.when(n == 0)
    def _():
        _rms_rows(x_ref, g_ref, h_ref)

    z = _dot(h_ref[...], w_ref[...]) + b_ref[...]
    o_ref[:, pl.ds(pl.multiple_of(n * tn, tn), tn)] = _gelu(z)

    @pl.when(n == pl.num_programs(1) - 1)
    def _():
        rows, step, cw = o_ref.shape[0], 32, 512

        def body(c, carry):
            blk = pl.ds(pl.multiple_of(c * step, step), step)

            def load(k):
                return o_ref[blk, k * cw:(k + 1) * cw]

            def store(k, v):
                o_ref[blk, k * cw:(k + 1) * cw] = v

            _layernorm_sweeps(load, store, o_ref.shape[1], lg_ref, lb_ref, cw)
            return carry

        lax.fori_loop(0, rows // step, body, 0)


def _b1(x, g3, w_in, b_in, ln_g, ln_b, layer, j, *, tm):
    m, d = x.shape
    nb, tn = w_in.shape[0] // 2, w_in.shape[2]
    d_sgu = nb * tn
    return pl.pallas_call(
        _b1_kernel,
        grid=(m // tm, nb),
        in_specs=[
            pl.BlockSpec((tm, d), lambda i, n: (i, 0)),
            pl.BlockSpec((None, 1, d), lambda i, n: (layer, 0, 0)),
            pl.BlockSpec((None, d, tn), lambda i, n: (n + nb, 0, 0)),
            pl.BlockSpec((None, 1, tn), lambda i, n: (j, 0, n + nb)),
            pl.BlockSpec((None, 1, d_sgu), lambda i, n: (j, 0, 0)),
            pl.BlockSpec((None, 1, d_sgu), lambda i, n: (j, 0, 0)),
        ],
        out_specs=pl.BlockSpec((tm, d_sgu), lambda i, n: (i, 0)),
        out_shape=jax.ShapeDtypeStruct((m, d_sgu), F32),
        scratch_shapes=[pltpu.VMEM((tm, d), BF16)],
        compiler_params=_params(("parallel", "arbitrary")),
        name=f"mix_b1_{j}",
    )(x, g3, w_in, b_in, ln_g, ln_b)


def _b2_kernel(x_ref, g_ref, v_ref, wi_ref, bi_ref, ws_ref, bs_ref, wo_ref, o_ref, h_ref,
               *, n_prompt_chunks):
    i = pl.program_id(0)
    tm = x_ref.shape[0]

    @pl.when(pl.program_id(1) == 0)
    def _():
        _rms_rows(x_ref, g_ref, h_ref, copy_ref=o_ref)

    u = _gelu(_dot(h_ref[...], wi_ref[...]) + bi_ref[...])
    ws = ws_ref[...]
    row = lax.broadcasted_iota(jnp.int32, ws.shape, 0)
    col = lax.broadcasted_iota(jnp.int32, ws.shape, 1)
    w_tril = jnp.where(col <= row, ws, 0.0).astype(BF16)
    bs = bs_ref[...]
    parts = []
    for c in range(tm // CHUNK):
        rows = slice(c * CHUNK, (c + 1) * CHUNK)
        v = v_ref[rows, :]
        s_prompt = _dot(w_tril, v.astype(BF16)) + bs
        s_sample = ws[0:1, 0:1] * v + bs[0:1, :]
        is_sample = i * (tm // CHUNK) + c >= n_prompt_chunks
        parts.append(u[rows, :] * jnp.where(is_sample, s_sample, s_prompt))
    y = jnp.concatenate(parts, axis=0).astype(BF16)
    o_ref[...] += _dot(y, wo_ref[...])


def _b2(x, g3, vn, w_in, b_in, w_s, b_s3, w_out, layer, j, *, n_prompt, tm):
    m, d = x.shape
    d_sgu = vn.shape[1]
    gw = d_sgu // N_SGU_GROUPS
    assert w_in.shape == (2 * N_SGU_GROUPS, d, gw) and w_out.shape == (1, d_sgu, d)
    kern = functools.partial(_b2_kernel, n_prompt_chunks=n_prompt // CHUNK)
    return pl.pallas_call(
        kern,
        grid=(m // tm, N_SGU_GROUPS),
        in_specs=[
            pl.BlockSpec((tm, d), lambda i, g: (i, 0)),
            pl.BlockSpec((None, 1, d), lambda i, g: (layer, 0, 0)),
            pl.BlockSpec((tm, gw), lambda i, g: (i, g)),
            pl.BlockSpec((None, d, gw), lambda i, g: (g, 0, 0)),
            pl.BlockSpec((None, 1, gw), lambda i, g: (j, 0, g)),
            pl.BlockSpec((None, None, CHUNK, CHUNK), lambda i, g: (j, g, 0, 0)),
            pl.BlockSpec((None, None, CHUNK, 1), lambda i, g: (j, g, 0, 0)),
            pl.BlockSpec((None, gw, d), lambda i, g: (0, g, 0)),
        ],
        out_specs=pl.BlockSpec((tm, d), lambda i, g: (i, 0)),
        out_shape=jax.ShapeDtypeStruct((m, d), F32),
        scratch_shapes=[pltpu.VMEM((tm, d), BF16)],
        compiler_params=_params(("parallel", "arbitrary")),
        name=f"mix_b2_{j}",
    )(x, g3, vn, w_in, b_in, w_s, b_s3, w_out)


def _c1_kernel(x_ref, g_ref, wb_ref, wc_ref, wx_ref, bg_ref, cx_ref, h_ref):
    @pl.when(pl.program_id(1) == 0)
    def _():
        _rms_rows(x_ref, g_ref, h_ref)

    h = h_ref[...]
    bg_ref[...] = _dot(h, wb_ref[...])
    cx_ref[...] = _dot(h, wc_ref[...]) * _dot(h, wx_ref[...])


def _c1(x, g3, w_in, layer, j, *, row0, n_rows, tm):
    d = x.shape[1]
    nb, tn = w_in.shape[0] // 3, w_in.shape[2]
    d_c = nb * tn
    rb0 = row0 // tm
    out = jax.ShapeDtypeStruct((n_rows, d_c), F32)
    return pl.pallas_call(
        _c1_kernel,
        grid=(n_rows // tm, nb),
        in_specs=[
            pl.BlockSpec((tm, d), lambda i, n: (rb0 + i, 0)),
            pl.BlockSpec((None, 1, d), lambda i, n: (layer, 0, 0)),
            pl.BlockSpec((None, d, tn), lambda i, n: (n, 0, 0)),
            pl.BlockSpec((None, d, tn), lambda i, n: (n + nb, 0, 0)),
            pl.BlockSpec((None, d, tn), lambda i, n: (n + 2 * nb, 0, 0)),
        ],
        out_specs=[pl.BlockSpec((tm, tn), lambda i, n: (i, n))] * 2,
        out_shape=[out, out],
        scratch_shapes=[pltpu.VMEM((tm, d), BF16)],
        compiler_params=_params(("parallel", "arbitrary")),
        name=f"mix_c1s_{j}",
    )(x, g3, w_in, w_in, w_in)


def _cf_kernel(x_ref, xr_ref, g_ref, wb_ref, wc_ref, wx_ref, wcv_ref, wo_ref, o_ref, st_ref,
               h_ref, gbuf_ref, wbuf_ref, hal_ref, act_ref, *, taps, tiles_per_seq, nb, rc):
    i, n = pl.program_id(0), pl.program_id(1)
    tm = x_ref.shape[0]
    tn = wb_ref.shape[1]
    halo = hal_ref.shape[0]
    cb = 128

    @pl.when(n == 0)
    def _():
        _start_tile(x_ref, g_ref, h_ref, hal_ref, i % tiles_per_seq == 0)

    @pl.when(n < nb)
    def _():
        col0 = pl.multiple_of(_snake(i, n, nb) * tn, tn)
        gbuf_ref[0:halo, :] = hal_ref[:, pl.ds(col0, tn)]
        wbuf_ref[0:taps, :] = wcv_ref[:, pl.ds(col0, tn)]
        wb, wc, wx = wb_ref[...], wc_ref[...], wx_ref[...]
        for r in range(tm // rc):
            hc = h_ref[r * rc:(r + 1) * rc, :]
            bg = _dot(hc, wb)
            gbuf_ref[halo + r * rc:halo + (r + 1) * rc, :] = _dot(hc, wc) * _dot(hc, wx)
            for sub in range(rc // cb):
                r0 = r * rc + sub * cb
                for strip in range(tn // LANES):
                    cols = slice(strip * LANES, (strip + 1) * LANES)
                    y = _dwconv_block(gbuf_ref, wbuf_ref, r0, cb, cols, taps, halo)
                    gate = bg[sub * cb:(sub + 1) * cb, cols]
                    act_ref[r0:r0 + cb, pl.ds(pl.multiple_of(col0 + strip * LANES, LANES), LANES)] = (
                        gate * y).astype(BF16)
        hal_ref[:, pl.ds(col0, tn)] = gbuf_ref[tm:tm + halo, :]

    @pl.when(n >= nb)
    def _():
        o_ref[...] = xr_ref[...] + _dot(act_ref[...], wo_ref[...])

    @pl.when((n == 2 * nb - 1) & (i % tiles_per_seq == tiles_per_seq - 1))
    def _():
        st_ref[...] = hal_ref[halo - (taps - 1):halo, :]


def _c_fused(x, g3, w_in, w_conv, w_out, layer, j, *, n_prompt, seq, tm):
    m, d = x.shape
    taps = w_conv.shape[1]
    nb, tn = w_out.shape[0], w_out.shape[2]
    halo = SUBLANES
    assert w_in.shape == (3 * nb, d, tn) and nb * tn == d and taps - 1 <= halo and seq % tm == 0
    kern = functools.partial(_cf_kernel, taps=taps, tiles_per_seq=seq // tm, nb=nb, rc=256)
    taps_pad = -(-taps // SUBLANES) * SUBLANES
    blk1 = lambda i, n: _snake(i, jnp.minimum(n, nb - 1), nb)
    blk2 = lambda i, n: _snake(i, jnp.maximum(n - nb, 0), nb)
    blk = lambda part: (lambda i, n: (blk1(i, n) + part * nb, 0, 0))
    return pl.pallas_call(
        kern,
        grid=(n_prompt // tm, 2 * nb),
        in_specs=[
            pl.BlockSpec((tm, d), _read_once_rows(n_prompt // tm)),
            pl.BlockSpec((tm, tn), lambda i, n: (i, blk2(i, n))),
            pl.BlockSpec((None, 1, d), lambda i, n: (layer, 0, 0)),
            pl.BlockSpec((None, d, tn), blk(0)),
            pl.BlockSpec((None, d, tn), blk(1)),
            pl.BlockSpec((None, d, tn), blk(2)),
            pl.BlockSpec((None, taps, d), lambda i, n: (j, 0, 0)),
            pl.BlockSpec((None, d, tn), lambda i, n: (blk2(i, n), 0, 0)),
        ],
        out_specs=[
            pl.BlockSpec((tm, tn), lambda i, n: (i, blk2(i, n))),
            pl.BlockSpec((None, taps - 1, d), lambda i, n: (i // (seq // tm), 0, 0)),
        ],
        out_shape=[
            jax.ShapeDtypeStruct((m, d), F32),
            jax.ShapeDtypeStruct((n_prompt // seq, taps - 1, d), F32),
        ],
        scratch_shapes=[
            pltpu.VMEM((tm, d), BF16),
            pltpu.VMEM((halo + tm, tn), F32),
            pltpu.VMEM((taps_pad, tn), F32),
            pltpu.VMEM((halo, d), F32),
            pltpu.VMEM((tm, d), BF16),
        ],
        compiler_params=_params(("arbitrary", "arbitrary")),
        name=f"mix_c_{j}",
    )(x, x, g3, w_in, w_in, w_in, w_conv, w_out)


def _c2s_kernel(x_ref, bg_ref, cx_ref, s0_ref, s1_ref, w_ref, wo_ref, _, o_ref, act_ref):
    @pl.when(pl.program_id(0) == 0)
    def _():
        w = w_ref[...]
        y = s0_ref[...] * w[0:1] + s1_ref[...] * w[1:2] + cx_ref[...] * w[2:3]
        act_ref[...] = (bg_ref[...] * y).astype(BF16)

    o_ref[...] = x_ref[...] + _dot(act_ref[...], wo_ref[...])


def _c2_sample(x, x_new, bg, cx, s0, s1, w_conv, w_out, j, *, n_prompt):
    m, d = x.shape
    ns = s0.shape[0]
    tn = w_out.shape[2]
    rb = n_prompt // ns
    taps = w_conv.shape[1]
    assert taps == 3
    return pl.pallas_call(
        _c2s_kernel,
        grid=(d // tn,),
        in_specs=[
            pl.BlockSpec((ns, tn), lambda n: (rb, n)),
            pl.BlockSpec((ns, d), lambda n: (0, 0)),
            pl.BlockSpec((ns, d), lambda n: (0, 0)),
            pl.BlockSpec((ns, d), lambda n: (0, 0)),
            pl.BlockSpec((ns, d), lambda n: (0, 0)),
            pl.BlockSpec((None, taps, d), lambda n: (j, 0, 0)),
            pl.BlockSpec((None, d, tn), lambda n: (n, 0, 0)),
            pl.BlockSpec(memory_space=pl.ANY),
        ],
        out_specs=pl.BlockSpec((ns, tn), lambda n: (rb, n)),
        out_shape=jax.ShapeDtypeStruct((m, d), F32),
        scratch_shapes=[pltpu.VMEM((ns, d), BF16)],
        input_output_aliases={7: 0},
        compiler_params=_params(("arbitrary",)),
        name=f"mix_c2s_{j}",
    )(x, bg, cx, s0, s1, w_conv, w_out, x_new)


def _last_rows(a, batch, seq, rows):
    return jnp.stack([a[(b + 1) * seq - rows:(b + 1) * seq] for b in range(batch)])


def kernel(x_prompt, x_sample, state_conv_a, state_conv_c, g_ffn1, g_mix, g_ffn2, g_final,
           w_ffn_gate, w_ffn_up, w_ffn_down,
           a_w_pw1, a_b_pw1, a_w_dw, a_b_dw, a_ln_g, a_ln_b, a_w_pw2,
           b_w_in, b_b_in, b_ln_g, b_ln_b, b_w_s, b_b_s, b_w_out,
           c_w_in, c_w_conv, c_w_out):
    batch, seq, d = x_prompt.shape
    n_sample = x_sample.shape[0]
    assert x_sample.shape[1] == 1
    n_prompt = batch * seq
    depth = g_ffn1.shape[0]
    past_c = state_conv_c.shape[2]
    assert past_c == 2
    d_sgu = b_ln_g.shape[-1]

    row3 = lambda a: a.reshape(a.shape[0], 1, a.shape[1])
    g1, gm, g2 = row3(g_ffn1), row3(g_mix), row3(g_ffn2)
    gf = g_final.reshape(1, d)
    a_b1, a_bd, a_lg, a_lb = row3(a_b_pw1), row3(a_b_dw), row3(a_ln_g), row3(a_ln_b)
    b_bi, b_lg, b_lb = row3(b_b_in), row3(b_ln_g), row3(b_ln_b)
    b_bs = b_b_s.reshape(*b_b_s.shape, 1)
    state_a_t = jnp.transpose(state_conv_a, (0, 2, 1, 3))

    ffn = functools.partial(_ffn, n_prompt=n_prompt, n_sample=n_sample,
                            tm=832, tf_head=512, tf_tail=512)
    new_a_p, new_b_p, new_b_s, new_c_p, new_c_s = [], [], [], [], []
    new_a_s_t = None

    x = x_prompt.reshape(n_prompt, d)
    xs = x_sample.reshape(n_sample, d)
    gw = d_sgu // N_SGU_GROUPS
    for i in range(depth):
        kind, j = i % 3, i // 3
        if kind == 0:
            casts = [(a_w_pw1, j, 32, 512), (a_w_pw2, j, 32, 512)]
        elif kind == 1:
            casts = [(b_w_in, j, 32, gw), (b_w_out, j, 64, d)]
        else:
            casts = [(c_w_in, j, 32, 512), (c_w_out, j, 32, 512)]
        x, (w_a16, w_b16) = ffn(x, xs, g1, gf, w_ffn_gate, w_ffn_up, w_ffn_down, i, 0, casts,
                                first=(i == 0), last=False)
        if kind == 0:
            glu_s = _a1(x, gm, w_a16, a_b1, i, j, row0=n_prompt, n_rows=n_sample, tm=n_sample)
            y_s, new_a_s_t = _a_sample_conv(state_a_t, glu_s, a_w_dw, a_bd, new_a_s_t, j, tn=256)
            x_new, st_p = _a_fused(x, gm, w_a16, a_b1, a_w_dw, a_bd, a_lg, a_lb, w_b16, i, j,
                                   n_prompt=n_prompt, seq=seq, tm=1024)
            x = _a2_sample(x, x_new, y_s, a_lg, a_lb, w_b16, j, n_prompt=n_prompt)
            new_a_p.append(st_p)
        elif kind == 1:
            vn = _b1(x, gm, w_a16, b_bi, b_lg, b_lb, i, j, tm=832)
            x = _b2(x, gm, vn, w_a16, b_bi, b_w_s, b_bs, w_b16, i, j, n_prompt=n_prompt, tm=640)
            new_b_p.append(_last_rows(vn, batch, seq, CHUNK))
            new_b_s.append(vn[n_prompt:].reshape(n_sample, 1, d_sgu))
        else:
            bg_s, cx_s = _c1(x, gm, w_a16, i, j, row0=n_prompt, n_rows=n_sample, tm=n_sample)
            s0, s1 = state_conv_c[j, :, 0], state_conv_c[j, :, 1]
            x_new, st_p = _c_fused(x, gm, w_a16, c_w_conv, w_b16, i, j,
                                   n_prompt=n_prompt, seq=seq, tm=1024)
            x = _c2_sample(x, x_new, bg_s, cx_s, s0, s1, c_w_conv, w_b16, j, n_prompt=n_prompt)
            new_c_p.append(st_p)
            new_c_s.append(jnp.stack([s1, cx_s], axis=1))
        last = i == depth - 1
        x, _ = ffn(x, xs, g2, gf, w_ffn_gate, w_ffn_up, w_ffn_down, i, 1, [],
                   first=False, last=last)

    return (x[:n_prompt].reshape(batch, seq, d), x[n_prompt:].reshape(n_sample, 1, d),
            jnp.stack(new_a_p), jnp.transpose(new_a_s_t, (0, 2, 1, 3)),
            jnp.stack(new_b_p), jnp.stack(new_b_s),
            jnp.stack(new_c_p), jnp.stack(new_c_s))
```

```python
import functools

import jax
import jax.numpy as jnp
from jax import lax
from jax.experimental import pallas as pl
from jax.experimental.pallas import tpu as pltpu

F32 = jnp.float32
BF16 = jnp.bfloat16

EPS = 1e-6
FFN_HALF = 0.5
CHUNK = 128
N_SGU_GROUPS = 8
INV_SQRT2 = 0.7071067811865476
SUBLANES = 8
LANES = 128

V7X_VMEM_BYTES = 64 * 1024 * 1024
VMEM_LIMIT = V7X_VMEM_BYTES - 6 * 1024 * 1024


def _params(sem):
    return pltpu.CompilerParams(dimension_semantics=sem, vmem_limit_bytes=VMEM_LIMIT)


def _read_once_rows(n_tiles):
    return lambda i, s: (jnp.minimum(i + jnp.minimum(s, 1), n_tiles - 1), 0)


def _rms(x, g):
    ms = jnp.mean(x * x, axis=-1, keepdims=True)
    return (x * lax.rsqrt(ms + EPS)) * g


def _layernorm(y, g, b):
    mu = jnp.mean(y, axis=-1, keepdims=True)
    yc = y - mu
    var = jnp.mean(yc * yc, axis=-1, keepdims=True)
    return (yc * lax.rsqrt(var + EPS)) * g + b


def _lane_sum(v):
    acc = v[:, 0:LANES]
    for t in range(1, v.shape[1] // LANES):
        acc = acc + v[:, t * LANES:(t + 1) * LANES]
    return acc


def _layernorm_sweeps(load, store, width, g_ref, b_ref, cw=512):
    lane_sum = _lane_sum
    n = width // cw
    acc = lane_sum(load(0))
    for c in range(1, n):
        acc = acc + lane_sum(load(c))
    mu = jnp.sum(acc, axis=-1, keepdims=True) * (1.0 / width)
    acc = None
    for c in range(n):
        dev = load(c) - mu
        sq = lane_sum(dev * dev)
        acc = sq if acc is None else acc + sq
    var = jnp.sum(acc, axis=-1, keepdims=True) * (1.0 / width)
    scale = lax.rsqrt(var + EPS)
    for c in range(n):
        cols = slice(c * cw, (c + 1) * cw)
        store(c, ((load(c) - mu) * scale) * g_ref[:, cols] + b_ref[:, cols])


def _silu(x):
    return x * jax.nn.sigmoid(x)


def _gelu(x):
    return 0.5 * x * (1.0 + lax.erf(x * INV_SQRT2))


def _dot(a, w):
    return jnp.dot(a, w, preferred_element_type=F32)


def _rms_rows(x_ref, g_ref, h_ref, copy_ref=None, cw=512):
    rows, width = x_ref.shape
    step = rows // 4
    assert rows % 4 == 0 and step % 16 == 0 and width % cw == 0

    def body(c, carry):
        blk = pl.ds(pl.multiple_of(c * step, step), step)
        acc = None
        for k in range(width // cw):
            x = x_ref[blk, k * cw:(k + 1) * cw]
            sq = _lane_sum(x * x)
            acc = sq if acc is None else acc + sq
        scale = lax.rsqrt(jnp.sum(acc, axis=-1, keepdims=True) * (1.0 / width) + EPS)
        for k in range(width // cw):
            cols = slice(k * cw, (k + 1) * cw)
            x = x_ref[blk, cols]
            h_ref[blk, cols] = ((x * scale) * g_ref[:, cols]).astype(BF16)
            if copy_ref is not None:
                copy_ref[blk, cols] = x
        return carry

    lax.fori_loop(0, 4, body, 0)


def _rms_rows_inplace(o_ref, g_ref, rows_per_step=64):
    rows, step = o_ref.shape[0], rows_per_step
    g = g_ref[...]

    def body(c, carry):
        r0 = pl.multiple_of(c * step, step)
        o_ref[pl.ds(r0, step), :] = _rms(o_ref[pl.ds(r0, step), :], g)
        return carry

    lax.fori_loop(0, rows // step, body, 0)


def _ffn_rows(h, wg, wu, wd):
    gate = _dot(h, wg)
    up = _dot(h, wu)
    act = (_silu(gate) * (up * FFN_HALF)).astype(BF16)
    return _dot(act, wd)


def _ffn_step(h_ref, wg, wu, wd, o_ref):
    o_ref[...] += _ffn_rows(h_ref[...], wg, wu, wd)


def _ffn_first_step(x_ref, g_ref, h_ref, wg, wu, wd, o_ref, tail, rc, sub=16):
    tm = x_ref.shape[0]
    g = g_ref[...]
    for r in range(tm // rc):
        for s in range(rc // sub):
            r0 = r * rc + s * sub
            x = x_ref[r0:r0 + sub, :]
            if tail is not None and r0 >= tail[1]:
                use_tail, split_row, tail_ref = tail
                x = jnp.where(use_tail, tail_ref[r0 - split_row:r0 - split_row + sub, :], x)
            h_ref[r0:r0 + sub, :] = _rms(x, g).astype(BF16)
            o_ref[r0:r0 + sub, :] = x
        rows = slice(r * rc, (r + 1) * rc)
        o_ref[rows, :] += _ffn_rows(h_ref[rows, :], wg, wu, wd)


def _ffn_head_kernel(x_ref, g_ref, gf_ref, wg_ref, wu_ref, wd_ref,
                     o_ref, wg16_ref, wu16_ref, wd16_ref, h_ref, *, final_norm):
    f = pl.program_id(0)

    @pl.when(f == 0)
    def _():
        _rms_rows(x_ref, g_ref, h_ref, copy_ref=o_ref)

    wg = wg_ref[...].astype(BF16)
    wu = wu_ref[...].astype(BF16)
    wd = wd_ref[...].astype(BF16)
    wg16_ref[...] = wg
    wu16_ref[...] = wu
    wd16_ref[...] = wd
    _ffn_step(h_ref, wg, wu, wd, o_ref)

    if final_norm:
        @pl.when(f == pl.num_programs(0) - 1)
        def _():
            _rms_rows_inplace(o_ref, gf_ref)


def _ffn_tail_kernel(*refs, first, last, split_row, n_casts):
    refs = list(refs)
    x_ref, g_ref, gf_ref, wg_ref, wu_ref, wd_ref, _ = refs[:7]
    xs_ref = refs[7] if first else None
    n_in = 8 if first else 7
    cast_src = refs[n_in:n_in + n_casts]
    o_ref = refs[n_in + n_casts]
    cast_dst = refs[n_in + n_casts + 1:n_in + 2 * n_casts + 1]
    h_ref = refs[-1]
    i, f = pl.program_id(0), pl.program_id(1)
    last_tile = i == pl.num_programs(0) - 1

    for src, dst in zip(cast_src, cast_dst):
        tn = dst.shape[2]
        for c in range(dst.shape[0]):
            dst[c] = src[:, c * tn:(c + 1) * tn].astype(BF16)

    @pl.when(f == 0)
    def _():
        tail = (last_tile, split_row, xs_ref) if first else None
        _ffn_first_step(x_ref, g_ref, h_ref, wg_ref[...], wu_ref[...], wd_ref[...], o_ref,
                        tail, rc=x_ref.shape[0] // 4)

    @pl.when(f > 0)
    def _():
        _ffn_step(h_ref, wg_ref[...], wu_ref[...], wd_ref[...], o_ref)

    if last:
        @pl.when(f == pl.num_programs(1) - 1)
        def _():
            _rms_rows_inplace(o_ref, gf_ref)


def _ffn(x, xs, g3, gf2, w_gate, w_up, w_down, layer, which, casts, *, n_prompt, n_sample,
         first, last, tm, tf_head, tf_tail):
    d = x.shape[1]
    dff = w_gate.shape[-1]
    m = n_prompt + n_sample
    assert m % tm == 0
    n_tiles = m // tm
    split_row = n_prompt - (n_tiles - 1) * tm
    assert 0 < split_row < tm and split_row % 64 == 0 and tm - split_row == n_sample
    out_rows = m
    name = f"ffn_{layer}_{which}"

    once = pl.Buffered(1)
    nf_head = dff // tf_head
    assert tf_head == tf_tail
    head_out, wg16, wu16, wd16 = pl.pallas_call(
        functools.partial(_ffn_head_kernel, final_norm=last),
        grid=(nf_head,),
        in_specs=[
            pl.BlockSpec((tm, d), lambda f: (0, 0), pipeline_mode=once),
            pl.BlockSpec((None, 1, d), lambda f: (layer, 0, 0)),
            pl.BlockSpec((1, d), lambda f: (0, 0)),
            pl.BlockSpec((None, None, d, tf_head), lambda f: (layer, which, 0, f)),
            pl.BlockSpec((None, None, d, tf_head), lambda f: (layer, which, 0, f)),
            pl.BlockSpec((None, None, tf_head, d), lambda f: (layer, which, f, 0)),
        ],
        out_specs=[
            pl.BlockSpec((tm, d), lambda f: (0, 0), pipeline_mode=once),
            pl.BlockSpec((None, d, tf_head), lambda f: (f, 0, 0)),
            pl.BlockSpec((None, d, tf_head), lambda f: (f, 0, 0)),
            pl.BlockSpec((None, tf_head, d), lambda f: (f, 0, 0)),
        ],
        out_shape=[
            jax.ShapeDtypeStruct((out_rows, d), F32),
            jax.ShapeDtypeStruct((nf_head, d, tf_head), BF16),
            jax.ShapeDtypeStruct((nf_head, d, tf_head), BF16),
            jax.ShapeDtypeStruct((nf_head, tf_head, d), BF16),
        ],
        scratch_shapes=[pltpu.VMEM((tm, d), BF16)],
        compiler_params=_params(("arbitrary",)),
        name=name + "_head",
    )(x, g3, gf2, w_gate, w_up, w_down)

    in_specs = [
        pl.BlockSpec((tm, d), lambda i, f: (i + 1, 0)),
        pl.BlockSpec((None, 1, d), lambda i, f: (layer, 0, 0)),
        pl.BlockSpec((1, d), lambda i, f: (0, 0)),
        pl.BlockSpec((None, d, tf_tail), lambda i, f: (f, 0, 0)),
        pl.BlockSpec((None, d, tf_tail), lambda i, f: (f, 0, 0)),
        pl.BlockSpec((None, tf_tail, d), lambda i, f: (f, 0, 0)),
        pl.BlockSpec(memory_space=pl.ANY),
    ]
    args = [x, g3, gf2, wg16, wu16, wd16, head_out]
    if first:
        in_specs.append(pl.BlockSpec((n_sample, d), lambda i, f: (0, 0)))
        args.append(xs)
    nf = dff // tf_tail
    out_specs = [pl.BlockSpec((tm, d), lambda i, f: (i + 1, 0))]
    out_shape = [jax.ShapeDtypeStruct((out_rows, d), F32)]
    for w, lj, rb, tn in casts:
        _, rows, cols = w.shape
        n_blk = rows // rb
        assert rows % rb == 0 and cols % tn == 0 and n_blk <= (n_tiles - 1) * nf
        blk = lambda i, f, n_blk=n_blk: jnp.minimum(i * nf + f, n_blk - 1)
        in_specs.append(pl.BlockSpec((None, rb, cols), lambda i, f, lj=lj, blk=blk: (lj, blk(i, f), 0)))
        args.append(w)
        out_specs.append(pl.BlockSpec((cols // tn, rb, tn), lambda i, f, blk=blk: (0, blk(i, f), 0)))
        out_shape.append(jax.ShapeDtypeStruct((cols // tn, rows, tn), BF16))
    outs = pl.pallas_call(
        functools.partial(_ffn_tail_kernel, first=first, last=last, split_row=split_row,
                          n_casts=len(casts)),
        grid=(n_tiles - 1, nf),
        in_specs=in_specs,
        out_specs=out_specs,
        out_shape=out_shape,
        scratch_shapes=[pltpu.VMEM((tm, d), BF16)],
        input_output_aliases={6: 0},
        compiler_params=_params(("arbitrary", "arbitrary")),
        name=name + "_tail",
    )(*args)
    return outs[0], list(outs[1:])


def _a1_kernel(x_ref, g_ref, wa_ref, wg_ref, ba_ref, bg_ref, o_ref, h_ref):
    @pl.when(pl.program_id(1) == 0)
    def _():
        _rms_rows(x_ref, g_ref, h_ref)

    h = h_ref[...]
    a = _dot(h, wa_ref[...]) + ba_ref[...]
    gt = _dot(h, wg_ref[...]) + bg_ref[...]
    o_ref[...] = a * jax.nn.sigmoid(gt)


def _a1(x, g3, w1, b1, layer, j, *, row0, n_rows, tm):
    d = x.shape[1]
    nb, tn = w1.shape[0] // 2, w1.shape[2]
    d_a = nb * tn
    rb0 = row0 // tm
    return pl.pallas_call(
        _a1_kernel,
        grid=(n_rows // tm, nb),
        in_specs=[
            pl.BlockSpec((tm, d), lambda i, n: (rb0 + i, 0)),
            pl.BlockSpec((None, 1, d), lambda i, n: (layer, 0, 0)),
            pl.BlockSpec((None, d, tn), lambda i, n: (n, 0, 0)),
            pl.BlockSpec((None, d, tn), lambda i, n: (n + nb, 0, 0)),
            pl.BlockSpec((None, 1, tn), lambda i, n: (j, 0, n)),
            pl.BlockSpec((None, 1, tn), lambda i, n: (j, 0, n + nb)),
        ],
        out_specs=pl.BlockSpec((tm, tn), lambda i, n: (i, n)),
        out_shape=jax.ShapeDtypeStruct((n_rows, d_a), F32),
        scratch_shapes=[pltpu.VMEM((tm, d), BF16)],
        compiler_params=_params(("parallel", "arbitrary")),
        name=f"mix_a1s_{j}",
    )(x, g3, w1, w1, b1, b1)


def _dwconv_block(xp_ref, w_ref, r0, rows, cols, taps, halo):
    lead = halo - (taps - 1)
    assert halo % SUBLANES == 0 and lead >= 0
    acc = None
    for s in range(SUBLANES):
        group = [k for k in range(taps) if (k + lead) % SUBLANES == s]
        if not group:
            continue
        ext = rows + (SUBLANES if s else 0)
        part = None
        for k in group:
            base = r0 + ((k + lead) // SUBLANES) * SUBLANES
            term = xp_ref[pl.ds(base, ext), cols] * w_ref[k:k + 1, cols]
            part = term if part is None else part + term
        if s:
            part = part[s:s + rows]
        acc = part if acc is None else acc + part
    return acc


def _snake(i, k, nb):
    return jnp.where(i % 2 == 0, k, nb - 1 - k)


def _start_tile(x_ref, g_ref, h_ref, hal_ref, first_in_seq):
    _rms_rows(x_ref, g_ref, h_ref)

    @pl.when(first_in_seq)
    def _():
        hal_ref[...] = jnp.zeros(hal_ref.shape, hal_ref.dtype)


def _af_kernel(x_ref, xr_ref, g_ref, wa_ref, wg_ref, ba_ref, bgt_ref, wdw_ref, bdw_ref, lg_ref,
               lb_ref, w2_ref, o_ref, st_ref, h_ref, gbuf_ref, wbuf_ref, hal_ref, y_ref, act_ref,
               *, taps, tiles_per_seq, nb, rc):
    i, n = pl.program_id(0), pl.program_id(1)
    tm = x_ref.shape[0]
    tn = wa_ref.shape[1]
    halo = hal_ref.shape[0]
    cb = 128

    @pl.when(n == 0)
    def _():
        _start_tile(x_ref, g_ref, h_ref, hal_ref, i % tiles_per_seq == 0)

    @pl.when(n < nb)
    def _():
        col0 = pl.multiple_of(_snake(i, n, nb) * tn, tn)
        gbuf_ref[0:halo, :] = hal_ref[:, pl.ds(col0, tn)]
        wbuf_ref[0:taps, :] = wdw_ref[:, pl.ds(col0, tn)]
        wa, wg = wa_ref[...], wg_ref[...]
        ba, bgt = ba_ref[...], bgt_ref[...]
        for r in range(tm // rc):
            hc = h_ref[r * rc:(r + 1) * rc, :]
            glu = (_dot(hc, wa) + ba) * jax.nn.sigmoid(_dot(hc, wg) + bgt)
            gbuf_ref[halo + r * rc:halo + (r + 1) * rc, :] = glu
            for sub in range(rc // cb):
                r0 = r * rc + sub * cb
                for strip in range(tn // LANES):
                    cols = slice(strip * LANES, (strip + 1) * LANES)
                    y = _dwconv_block(gbuf_ref, wbuf_ref, r0, cb, cols, taps, halo)
                    y_ref[r0:r0 + cb, pl.ds(pl.multiple_of(col0 + strip * LANES, LANES), LANES)] = y
        hal_ref[:, pl.ds(col0, tn)] = gbuf_ref[tm:tm + halo, :]

    @pl.when(n == nb)
    def _():
        w2 = w2_ref[...]
        step, cw = 64, 512
        for r in range(tm // rc):
            for s in range(rc // step):
                blk = slice(r * rc + s * step, r * rc + (s + 1) * step)

                def load(c, blk=blk):
                    cols = slice(c * cw, (c + 1) * cw)
                    return y_ref[blk, cols] + bdw_ref[:, cols]

                def store(c, v, blk=blk):
                    act_ref[blk, c * cw:(c + 1) * cw] = _silu(v).astype(BF16)

                _layernorm_sweeps(load, store, y_ref.shape[1], lg_ref, lb_ref, cw)
            rows = slice(r * rc, (r + 1) * rc)
            o_ref[rows, :] = xr_ref[rows, :] + _dot(act_ref[rows, :], w2)

    @pl.when(n > nb)
    def _():
        o_ref[...] = xr_ref[...] + _dot(act_ref[...], w2_ref[...])

    @pl.when((n == 2 * nb - 1) & (i % tiles_per_seq == tiles_per_seq - 1))
    def _():
        st_ref[...] = hal_ref[halo - (taps - 1):halo, :]


def _a_fused(x, g3, w1, b1, w_dw, b_dw, ln_g, ln_b, w2, layer, j, *, n_prompt, seq, tm):
    m, d = x.shape
    taps = w_dw.shape[1]
    nb, tn = w2.shape[0], w2.shape[2]
    halo = 32
    assert w1.shape == (2 * nb, d, tn) and nb * tn == d and taps - 1 <= halo and seq % tm == 0
    kern = functools.partial(_af_kernel, taps=taps, tiles_per_seq=seq // tm, nb=nb, rc=256)
    taps_pad = -(-taps // SUBLANES) * SUBLANES
    blk1 = lambda i, n: _snake(i, jnp.minimum(n, nb - 1), nb)
    blk2 = lambda i, n: _snake(i, jnp.maximum(n - nb, 0), nb)
    row = lambda i, n: (j, 0, 0)
    return pl.pallas_call(
        kern,
        grid=(n_prompt // tm, 2 * nb),
        in_specs=[
            pl.BlockSpec((tm, d), _read_once_rows(n_prompt // tm)),
            pl.BlockSpec((tm, tn), lambda i, n: (i, blk2(i, n))),
            pl.BlockSpec((None, 1, d), lambda i, n: (layer, 0, 0)),
            pl.BlockSpec((None, d, tn), lambda i, n: (blk1(i, n), 0, 0)),
            pl.BlockSpec((None, d, tn), lambda i, n: (blk1(i, n) + nb, 0, 0)),
            pl.BlockSpec((None, 1, tn), lambda i, n: (j, 0, blk1(i, n))),
            pl.BlockSpec((None, 1, tn), lambda i, n: (j, 0, blk1(i, n) + nb)),
            pl.BlockSpec((None, taps, d), row),
            pl.BlockSpec((None, 1, d), row),
            pl.BlockSpec((None, 1, d), row),
            pl.BlockSpec((None, 1, d), row),
            pl.BlockSpec((None, d, tn), lambda i, n: (blk2(i, n), 0, 0)),
        ],
        out_specs=[
            pl.BlockSpec((tm, tn), lambda i, n: (i, blk2(i, n))),
            pl.BlockSpec((None, taps - 1, d), lambda i, n: (i // (seq // tm), 0, 0)),
        ],
        out_shape=[
            jax.ShapeDtypeStruct((m, d), F32),
            jax.ShapeDtypeStruct((n_prompt // seq, taps - 1, d), F32),
        ],
        scratch_shapes=[
            pltpu.VMEM((tm, d), BF16),
            pltpu.VMEM((halo + tm, tn), F32),
            pltpu.VMEM((taps_pad, tn), F32),
            pltpu.VMEM((halo, d), F32),
            pltpu.VMEM((tm, d), F32),
            pltpu.VMEM((tm, d), BF16),
        ],
        compiler_params=_params(("arbitrary", "arbitrary")),
        name=f"mix_a_{j}",
    )(x, x, g3, w1, w1, b1, b1, w_dw, b_dw, ln_g, ln_b, w2)


def _a_sconv_kernel(*refs):
    st_ref, glu_ref, w_ref, b_ref = refs[:4]
    y_ref, ns_ref = refs[-2:]
    past = st_ref.shape[0]
    acc = st_ref[0] * w_ref[0:1, :]
    for k in range(1, past):
        acc = acc + st_ref[k] * w_ref[k:k + 1, :]
        ns_ref[k - 1] = st_ref[k]
    g = glu_ref[...]
    ns_ref[past - 1] = g
    y_ref[...] = acc + g * w_ref[past:past + 1, :] + b_ref[...]


def _a_sample_conv(state_t, glu, w_dw, b_dw, prev, j, *, tn):
    n_layers, past, nb, d = state_t.shape
    taps = w_dw.shape[1]
    assert taps == past + 1 and glu.shape == (nb, d)
    in_specs = [
        pl.BlockSpec((None, past, nb, tn), lambda n: (j, 0, 0, n)),
        pl.BlockSpec((nb, tn), lambda n: (0, n)),
        pl.BlockSpec((None, taps, tn), lambda n: (j, 0, n)),
        pl.BlockSpec((None, 1, tn), lambda n: (j, 0, n)),
    ]
    args = [state_t, glu, w_dw, b_dw]
    aliases = {}
    if prev is not None:
        in_specs.append(pl.BlockSpec(memory_space=pl.ANY))
        args.append(prev)
        aliases = {4: 1}
    return pl.pallas_call(
        _a_sconv_kernel,
        grid=(d // tn,),
        in_specs=in_specs,
        out_specs=[
            pl.BlockSpec((nb, tn), lambda n: (0, n)),
            pl.BlockSpec((None, past, nb, tn), lambda n: (j, 0, 0, n)),
        ],
        out_shape=[
            jax.ShapeDtypeStruct((nb, d), F32),
            jax.ShapeDtypeStruct((n_layers, past, nb, d), F32),
        ],
        input_output_aliases=aliases,
        compiler_params=_params(("parallel",)),
        name=f"mix_a_sconv_{j}",
    )(*args)


def _a2s_kernel(x_ref, y_ref, lg_ref, lb_ref, w2_ref, _, o_ref, act_ref):
    @pl.when(pl.program_id(0) == 0)
    def _():
        act_ref[...] = _silu(_layernorm(y_ref[...], lg_ref[...], lb_ref[...])).astype(BF16)

    o_ref[...] = x_ref[...] + _dot(act_ref[...], w2_ref[...])


def _a2_sample(x, x_new, y, ln_g, ln_b, w2, j, *, n_prompt):
    m, d = x.shape
    ns = y.shape[0]
    tn = w2.shape[2]
    rb = n_prompt // ns
    return pl.pallas_call(
        _a2s_kernel,
        grid=(d // tn,),
        in_specs=[
            pl.BlockSpec((ns, tn), lambda n: (rb, n)),
            pl.BlockSpec((ns, d), lambda n: (0, 0)),
            pl.BlockSpec((None, 1, d), lambda n: (j, 0, 0)),
            pl.BlockSpec((None, 1, d), lambda n: (j, 0, 0)),
            pl.BlockSpec((None, d, tn), lambda n: (n, 0, 0)),
            pl.BlockSpec(memory_space=pl.ANY),
        ],
        out_specs=pl.BlockSpec((ns, tn), lambda n: (rb, n)),
        out_shape=jax.ShapeDtypeStruct((m, d), F32),
        scratch_shapes=[pltpu.VMEM((ns, d), BF16)],
        input_output_aliases={5: 0},
        compiler_params=_params(("arbitrary",)),
        name=f"mix_a2s_{j}",
    )(x, y, ln_g, ln_b, w2, x_new)


def _b1_kernel(x_ref, g_ref, w_ref, b_ref, lg_ref, lb_ref, o_ref, h_ref):
    n = pl.program_id(1)
    tn = w_ref.shape[1]

    @pl.when(n == 0)
    def _():
        _rms_rows(x_ref, g_ref, h_ref)

    z = _dot(h_ref[...], w_ref[...]) + b_ref[...]
    o_ref[:, pl.ds(pl.multiple_of(n * tn, tn), tn)] = _gelu(z)

    @pl.when(n == pl.num_programs(1) - 1)
    def _():
        rows, step, cw = o_ref.shape[0], 64, 512

        def body(c, carry):
            blk = pl.ds(pl.multiple_of(c * step, step), step)

            def load(k):
                return o_ref[blk, k * cw:(k + 1) * cw]

            def store(k, v):
                o_ref[blk, k * cw:(k + 1) * cw] = v

            _layernorm_sweeps(load, store, o_ref.shape[1], lg_ref, lb_ref, cw)
            return carry

        lax.fori_loop(0, rows // step, body, 0)


def _b1(x, g3, w_in, b_in, ln_g, ln_b, layer, j, *, tm):
    m, d = x.shape
    nb, tn = w_in.shape[0] // 2, w_in.shape[2]
    d_sgu = nb * tn
    return pl.pallas_call(
        _b1_kernel,
        grid=(m // tm, nb),
        in_specs=[
            pl.BlockSpec((tm, d), lambda i, n: (i, 0)),
            pl.BlockSpec((None, 1, d), lambda i, n: (layer, 0, 0)),
            pl.BlockSpec((None, d, tn), lambda i, n: (n + nb, 0, 0)),
            pl.BlockSpec((None, 1, tn), lambda i, n: (j, 0, n + nb)),
            pl.BlockSpec((None, 1, d_sgu), lambda i, n: (j, 0, 0)),
            pl.BlockSpec((None, 1, d_sgu), lambda i, n: (j, 0, 0)),
        ],
        out_specs=pl.BlockSpec((tm, d_sgu), lambda i, n: (i, 0)),
        out_shape=jax.ShapeDtypeStruct((m, d_sgu), F32),
        scratch_shapes=[pltpu.VMEM((tm, d), BF16)],
        compiler_params=_params(("parallel", "arbitrary")),
        name=f"mix_b1_{j}",
    )(x, g3, w_in, b_in, ln_g, ln_b)


def _b2_kernel(x_ref, g_ref, v_ref, wi_ref, bi_ref, ws_ref, bs_ref, wo_ref, o_ref, h_ref,
               *, n_prompt_chunks):
    i = pl.program_id(0)
    tm = x_ref.shape[0]

    @pl.when(pl.program_id(1) == 0)
    def _():
        _rms_rows(x_ref, g_ref, h_ref, copy_ref=o_ref)

    u = _gelu(_dot(h_ref[...], wi_ref[...]) + bi_ref[...])
    ws = ws_ref[...]
    row = lax.broadcasted_iota(jnp.int32, ws.shape, 0)
    col = lax.broadcasted_iota(jnp.int32, ws.shape, 1)
    w_tril = jnp.where(col <= row, ws, 0.0).astype(BF16)
    bs = bs_ref[...]
    parts = []
    for c in range(tm // CHUNK):
        rows = slice(c * CHUNK, (c + 1) * CHUNK)
        v = v_ref[rows, :]
        s_prompt = _dot(w_tril, v.astype(BF16)) + bs
        s_sample = ws[0:1, 0:1] * v + bs[0:1, :]
        is_sample = i * (tm // CHUNK) + c >= n_prompt_chunks
        parts.append(u[rows, :] * jnp.where(is_sample, s_sample, s_prompt))
    y = jnp.concatenate(parts, axis=0).astype(BF16)
    o_ref[...] += _dot(y, wo_ref[...])


def _b2(x, g3, vn, w_in, b_in, w_s, b_s3, w_out, layer, j, *, n_prompt, tm):
    m, d = x.shape
    d_sgu = vn.shape[1]
    gw = d_sgu // N_SGU_GROUPS
    assert w_in.shape == (2 * N_SGU_GROUPS, d, gw) and w_out.shape == (1, d_sgu, d)
    kern = functools.partial(_b2_kernel, n_prompt_chunks=n_prompt // CHUNK)
    return pl.pallas_call(
        kern,
        grid=(m // tm, N_SGU_GROUPS),
        in_specs=[
            pl.BlockSpec((tm, d), lambda i, g: (i, 0)),
            pl.BlockSpec((None, 1, d), lambda i, g: (layer, 0, 0)),
            pl.BlockSpec((tm, gw), lambda i, g: (i, g)),
            pl.BlockSpec((None, d, gw), lambda i, g: (g, 0, 0)),
            pl.BlockSpec((None, 1, gw), lambda i, g: (j, 0, g)),
            pl.BlockSpec((None, None, CHUNK, CHUNK), lambda i, g: (j, g, 0, 0)),
            pl.BlockSpec((None, None, CHUNK, 1), lambda i, g: (j, g, 0, 0)),
            pl.BlockSpec((None, gw, d), lambda i, g: (0, g, 0)),
        ],
        out_specs=pl.BlockSpec((tm, d), lambda i, g: (i, 0)),
        out_shape=jax.ShapeDtypeStruct((m, d), F32),
        scratch_shapes=[pltpu.VMEM((tm, d), BF16)],
        compiler_params=_params(("parallel", "arbitrary")),
        name=f"mix_b2_{j}",
    )(x, g3, vn, w_in, b_in, w_s, b_s3, w_out)


def _c1_kernel(x_ref, g_ref, wb_ref, wc_ref, wx_ref, bg_ref, cx_ref, h_ref):
    @pl.when(pl.program_id(1) == 0)
    def _():
        _rms_rows(x_ref, g_ref, h_ref)

    h = h_ref[...]
    bg_ref[...] = _dot(h, wb_ref[...])
    cx_ref[...] = _dot(h, wc_ref[...]) * _dot(h, wx_ref[...])


def _c1(x, g3, w_in, layer, j, *, row0, n_rows, tm):
    d = x.shape[1]
    nb, tn = w_in.shape[0] // 3, w_in.shape[2]
    d_c = nb * tn
    rb0 = row0 // tm
    out = jax.ShapeDtypeStruct((n_rows, d_c), F32)
    return pl.pallas_call(
        _c1_kernel,
        grid=(n_rows // tm, nb),
        in_specs=[
            pl.BlockSpec((tm, d), lambda i, n: (rb0 + i, 0)),
            pl.BlockSpec((None, 1, d), lambda i, n: (layer, 0, 0)),
            pl.BlockSpec((None, d, tn), lambda i, n: (n, 0, 0)),
            pl.BlockSpec((None, d, tn), lambda i, n: (n + nb, 0, 0)),
            pl.BlockSpec((None, d, tn), lambda i, n: (n + 2 * nb, 0, 0)),
        ],
        out_specs=[pl.BlockSpec((tm, tn), lambda i, n: (i, n))] * 2,
        out_shape=[out, out],
        scratch_shapes=[pltpu.VMEM((tm, d), BF16)],
        compiler_params=_params(("parallel", "arbitrary")),
        name=f"mix_c1s_{j}",
    )(x, g3, w_in, w_in, w_in)


def _cf_kernel(x_ref, xr_ref, g_ref, wb_ref, wc_ref, wx_ref, wcv_ref, wo_ref, o_ref, st_ref,
               h_ref, gbuf_ref, wbuf_ref, hal_ref, act_ref, *, taps, tiles_per_seq, nb, rc):
    i, n = pl.program_id(0), pl.program_id(1)
    tm = x_ref.shape[0]
    tn = wb_ref.shape[1]
    halo = hal_ref.shape[0]
    cb = 128

    @pl.when(n == 0)
    def _():
        _start_tile(x_ref, g_ref, h_ref, hal_ref, i % tiles_per_seq == 0)

    @pl.when(n < nb)
    def _():
        col0 = pl.multiple_of(_snake(i, n, nb) * tn, tn)
        gbuf_ref[0:halo, :] = hal_ref[:, pl.ds(col0, tn)]
        wbuf_ref[0:taps, :] = wcv_ref[:, pl.ds(col0, tn)]
        wb, wc, wx = wb_ref[...], wc_ref[...], wx_ref[...]
        for r in range(tm // rc):
            hc = h_ref[r * rc:(r + 1) * rc, :]
            bg = _dot(hc, wb)
            gbuf_ref[halo + r * rc:halo + (r + 1) * rc, :] = _dot(hc, wc) * _dot(hc, wx)
            for sub in range(rc // cb):
                r0 = r * rc + sub * cb
                for strip in range(tn // LANES):
                    cols = slice(strip * LANES, (strip + 1) * LANES)
                    y = _dwconv_block(gbuf_ref, wbuf_ref, r0, cb, cols, taps, halo)
                    gate = bg[sub * cb:(sub + 1) * cb, cols]
                    act_ref[r0:r0 + cb, pl.ds(pl.multiple_of(col0 + strip * LANES, LANES), LANES)] = (
                        gate * y).astype(BF16)
        hal_ref[:, pl.ds(col0, tn)] = gbuf_ref[tm:tm + halo, :]

    @pl.when(n >= nb)
    def _():
        o_ref[...] = xr_ref[...] + _dot(act_ref[...], wo_ref[...])

    @pl.when((n == 2 * nb - 1) & (i % tiles_per_seq == tiles_per_seq - 1))
    def _():
        st_ref[...] = hal_ref[halo - (taps - 1):halo, :]


def _c_fused(x, g3, w_in, w_conv, w_out, layer, j, *, n_prompt, seq, tm):
    m, d = x.shape
    taps = w_conv.shape[1]
    nb, tn = w_out.shape[0], w_out.shape[2]
    halo = SUBLANES
    assert w_in.shape == (3 * nb, d, tn) and nb * tn == d and taps - 1 <= halo and seq % tm == 0
    kern = functools.partial(_cf_kernel, taps=taps, tiles_per_seq=seq // tm, nb=nb, rc=256)
    taps_pad = -(-taps // SUBLANES) * SUBLANES
    blk1 = lambda i, n: _snake(i, jnp.minimum(n, nb - 1), nb)
    blk2 = lambda i, n: _snake(i, jnp.maximum(n - nb, 0), nb)
    blk = lambda part: (lambda i, n: (blk1(i, n) + part * nb, 0, 0))
    return pl.pallas_call(
        kern,
        grid=(n_prompt // tm, 2 * nb),
        in_specs=[
            pl.BlockSpec((tm, d), _read_once_rows(n_prompt // tm)),
            pl.BlockSpec((tm, tn), lambda i, n: (i, blk2(i, n))),
            pl.BlockSpec((None, 1, d), lambda i, n: (layer, 0, 0)),
            pl.BlockSpec((None, d, tn), blk(0)),
            pl.BlockSpec((None, d, tn), blk(1)),
            pl.BlockSpec((None, d, tn), blk(2)),
            pl.BlockSpec((None, taps, d), lambda i, n: (j, 0, 0)),
            pl.BlockSpec((None, d, tn), lambda i, n: (blk2(i, n), 0, 0)),
        ],
        out_specs=[
            pl.BlockSpec((tm, tn), lambda i, n: (i, blk2(i, n))),
            pl.BlockSpec((None, taps - 1, d), lambda i, n: (i // (seq // tm), 0, 0)),
        ],
        out_shape=[
            jax.ShapeDtypeStruct((m, d), F32),
            jax.ShapeDtypeStruct((n_prompt // seq, taps - 1, d), F32),
        ],
        scratch_shapes=[
            pltpu.VMEM((tm, d), BF16),
            pltpu.VMEM((halo + tm, tn), F32),
            pltpu.VMEM((taps_pad, tn), F32),
            pltpu.VMEM((halo, d), F32),
            pltpu.VMEM((tm, d), BF16),
        ],
        compiler_params=_params(("arbitrary", "arbitrary")),
        name=f"mix_c_{j}",
    )(x, x, g3, w_in, w_in, w_in, w_conv, w_out)


def _c2s_kernel(x_ref, bg_ref, cx_ref, s0_ref, s1_ref, w_ref, wo_ref, _, o_ref, act_ref):
    @pl.when(pl.program_id(0) == 0)
    def _():
        w = w_ref[...]
        y = s0_ref[...] * w[0:1] + s1_ref[...] * w[1:2] + cx_ref[...] * w[2:3]
        act_ref[...] = (bg_ref[...] * y).astype(BF16)

    o_ref[...] = x_ref[...] + _dot(act_ref[...], wo_ref[...])


def _c2_sample(x, x_new, bg, cx, s0, s1, w_conv, w_out, j, *, n_prompt):
    m, d = x.shape
    ns = s0.shape[0]
    tn = w_out.shape[2]
    rb = n_prompt // ns
    taps = w_conv.shape[1]
    assert taps == 3
    return pl.pallas_call(
        _c2s_kernel,
        grid=(d // tn,),
        in_specs=[
            pl.BlockSpec((ns, tn), lambda n: (rb, n)),
            pl.BlockSpec((ns, d), lambda n: (0, 0)),
            pl.BlockSpec((ns, d), lambda n: (0, 0)),
            pl.BlockSpec((ns, d), lambda n: (0, 0)),
            pl.BlockSpec((ns, d), lambda n: (0, 0)),
            pl.BlockSpec((None, taps, d), lambda n: (j, 0, 0)),
            pl.BlockSpec((None, d, tn), lambda n: (n, 0, 0)),
            pl.BlockSpec(memory_space=pl.ANY),
        ],
        out_specs=pl.BlockSpec((ns, tn), lambda n: (rb, n)),
        out_shape=jax.ShapeDtypeStruct((m, d), F32),
        scratch_shapes=[pltpu.VMEM((ns, d), BF16)],
        input_output_aliases={7: 0},
        compiler_params=_params(("arbitrary",)),
        name=f"mix_c2s_{j}",
    )(x, bg, cx, s0, s1, w_conv, w_out, x_new)


def _last_rows(a, batch, seq, rows):
    return jnp.stack([a[(b + 1) * seq - rows:(b + 1) * seq] for b in range(batch)])


def kernel(x_prompt, x_sample, state_conv_a, state_conv_c, g_ffn1, g_mix, g_ffn2, g_final,
           w_ffn_gate, w_ffn_up, w_ffn_down,
           a_w_pw1, a_b_pw1, a_w_dw, a_b_dw, a_ln_g, a_ln_b, a_w_pw2,
           b_w_in, b_b_in, b_ln_g, b_ln_b, b_w_s, b_b_s, b_w_out,
           c_w_in, c_w_conv, c_w_out):
    batch, seq, d = x_prompt.shape
    n_sample = x_sample.shape[0]
    assert x_sample.shape[1] == 1
    n_prompt = batch * seq
    depth = g_ffn1.shape[0]
    past_c = state_conv_c.shape[2]
    assert past_c == 2
    d_sgu = b_ln_g.shape[-1]

    row3 = lambda a: a.reshape(a.shape[0], 1, a.shape[1])
    g1, gm, g2 = row3(g_ffn1), row3(g_mix), row3(g_ffn2)
    gf = g_final.reshape(1, d)
    a_b1, a_bd, a_lg, a_lb = row3(a_b_pw1), row3(a_b_dw), row3(a_ln_g), row3(a_ln_b)
    b_bi, b_lg, b_lb = row3(b_b_in), row3(b_ln_g), row3(b_ln_b)
    b_bs = b_b_s.reshape(*b_b_s.shape, 1)
    state_a_t = jnp.transpose(state_conv_a, (0, 2, 1, 3))

    ffn = functools.partial(_ffn, n_prompt=n_prompt, n_sample=n_sample,
                            tm=832, tf_head=512, tf_tail=512)
    new_a_p, new_b_p, new_b_s, new_c_p, new_c_s = [], [], [], [], []
    new_a_s_t = None

    x = x_prompt.reshape(n_prompt, d)
    xs = x_sample.reshape(n_sample, d)
    gw = d_sgu // N_SGU_GROUPS
    for i in range(depth):
        kind, j = i % 3, i // 3
        if kind == 0:
            casts = [(a_w_pw1, j, 32, 512), (a_w_pw2, j, 32, 512)]
        elif kind == 1:
            casts = [(b_w_in, j, 32, gw), (b_w_out, j, 64, d)]
        else:
            casts = [(c_w_in, j, 32, 512), (c_w_out, j, 32, 512)]
        x, (w_a16, w_b16) = ffn(x, xs, g1, gf, w_ffn_gate, w_ffn_up, w_ffn_down, i, 0, casts,
                                first=(i == 0), last=False)
        if kind == 0:
            glu_s = _a1(x, gm, w_a16, a_b1, i, j, row0=n_prompt, n_rows=n_sample, tm=n_sample)
            y_s, new_a_s_t = _a_sample_conv(state_a_t, glu_s, a_w_dw, a_bd, new_a_s_t, j, tn=256)
            x_new, st_p = _a_fused(x, gm, w_a16, a_b1, a_w_dw, a_bd, a_lg, a_lb, w_b16, i, j,
                                   n_prompt=n_prompt, seq=seq, tm=1024)
            x = _a2_sample(x, x_new, y_s, a_lg, a_lb, w_b16, j, n_prompt=n_prompt)
            new_a_p.append(st_p)
        elif kind == 1:
            vn = _b1(x, gm, w_a16, b_bi, b_lg, b_lb, i, j, tm=832)
            x = _b2(x, gm, vn, w_a16, b_bi, b_w_s, b_bs, w_b16, i, j, n_prompt=n_prompt, tm=640)
            new_b_p.append(_last_rows(vn, batch, seq, CHUNK))
            new_b_s.append(vn[n_prompt:].reshape(n_sample, 1, d_sgu))
        else:
            bg_s, cx_s = _c1(x, gm, w_a16, i, j, row0=n_prompt, n_rows=n_sample, tm=n_sample)
            s0, s1 = state_conv_c[j, :, 0], state_conv_c[j, :, 1]
            x_new, st_p = _c_fused(x, gm, w_a16, c_w_conv, w_b16, i, j,
                                   n_prompt=n_prompt, seq=seq, tm=1024)
            x = _c2_sample(x, x_new, bg_s, cx_s, s0, s1, c_w_conv, w_b16, j, n_prompt=n_prompt)
            new_c_p.append(st_p)
            new_c_s.append(jnp.stack([s1, cx_s], axis=1))
        last = i == depth - 1
        x, _ = ffn(x, xs, g2, gf, w_ffn_gate, w_ffn_up, w_ffn_down, i, 1, [],
                   first=False, last=last)

    return (x[:n_prompt].reshape(batch, seq, d), x[n_prompt:].reshape(n_sample, 1, d),
            jnp.stack(new_a_p), jnp.transpose(new_a_s_t, (0, 2, 1, 3)),
            jnp.stack(new_b_p), jnp.stack(new_b_s),
            jnp.stack(new_c_p), jnp.stack(new_c_s))
```

```python
import functools

import jax
import jax.numpy as jnp
from jax import lax
from jax.experimental import pallas as pl
from jax.experimental.pallas import tpu as pltpu

F32 = jnp.float32
BF16 = jnp.bfloat16

EPS = 1e-6
FFN_HALF = 0.5
CHUNK = 128
N_SGU_GROUPS = 8
INV_SQRT2 = 0.7071067811865476
SUBLANES = 8
LANES = 128

V7X_VMEM_BYTES = 64 * 1024 * 1024
VMEM_LIMIT = V7X_VMEM_BYTES - 6 * 1024 * 1024


def _params(sem):
    return pltpu.CompilerParams(dimension_semantics=sem, vmem_limit_bytes=VMEM_LIMIT)


def _read_once_rows(n_tiles):
    return lambda i, s: (jnp.minimum(i + jnp.minimum(s, 1), n_tiles - 1), 0)


def _layernorm(y, g, b):
    mu = jnp.mean(y, axis=-1, keepdims=True)
    yc = y - mu
    var = jnp.mean(yc * yc, axis=-1, keepdims=True)
    return (yc * lax.rsqrt(var + EPS)) * g + b


def _lane_sum(v):
    acc = v[:, 0:LANES]
    for t in range(1, v.shape[1] // LANES):
        acc = acc + v[:, t * LANES:(t + 1) * LANES]
    return acc


def _layernorm_sweeps(load, store, width, g_ref, b_ref, cw=512):
    lane_sum = _lane_sum
    n = width // cw
    acc = lane_sum(load(0))
    for c in range(1, n):
        acc = acc + lane_sum(load(c))
    mu = jnp.sum(acc, axis=-1, keepdims=True) * (1.0 / width)
    acc = None
    for c in range(n):
        dev = load(c) - mu
        sq = lane_sum(dev * dev)
        acc = sq if acc is None else acc + sq
    var = jnp.sum(acc, axis=-1, keepdims=True) * (1.0 / width)
    scale = lax.rsqrt(var + EPS)
    for c in range(n):
        cols = slice(c * cw, (c + 1) * cw)
        store(c, ((load(c) - mu) * scale) * g_ref[:, cols] + b_ref[:, cols])


def _silu(x):
    return x * jax.nn.sigmoid(x)


def _gelu(x):
    return 0.5 * x * (1.0 + lax.erf(x * INV_SQRT2))


def _dot(a, w):
    return jnp.dot(a, w, preferred_element_type=F32)


def _rms_rows(x_ref, g_ref, h_ref, copy_ref=None, cw=512):
    rows, width = x_ref.shape
    step = rows // 4
    assert rows % 4 == 0 and step % 16 == 0 and width % cw == 0

    def body(c, carry):
        blk = pl.ds(pl.multiple_of(c * step, step), step)
        acc = None
        for k in range(width // cw):
            x = x_ref[blk, k * cw:(k + 1) * cw]
            sq = _lane_sum(x * x)
            acc = sq if acc is None else acc + sq
        scale = lax.rsqrt(jnp.sum(acc, axis=-1, keepdims=True) * (1.0 / width) + EPS)
        for k in range(width // cw):
            cols = slice(k * cw, (k + 1) * cw)
            x = x_ref[blk, cols]
            h_ref[blk, cols] = ((x * scale) * g_ref[:, cols]).astype(BF16)
            if copy_ref is not None:
                copy_ref[blk, cols] = x
        return carry

    lax.fori_loop(0, 4, body, 0)


def _rms_rows_inplace(o_ref, g_ref, cw=512):
    rows, width = o_ref.shape
    step = rows // 4
    assert rows % 4 == 0 and step % SUBLANES == 0 and width % cw == 0

    def body(c, carry):
        blk = pl.ds(pl.multiple_of(c * step, step), step)
        acc = None
        for k in range(width // cw):
            x = o_ref[blk, k * cw:(k + 1) * cw]
            sq = _lane_sum(x * x)
            acc = sq if acc is None else acc + sq
        scale = lax.rsqrt(jnp.sum(acc, axis=-1, keepdims=True) * (1.0 / width) + EPS)
        for k in range(width // cw):
            cols = slice(k * cw, (k + 1) * cw)
            o_ref[blk, cols] = (o_ref[blk, cols] * scale) * g_ref[:, cols]
        return carry

    lax.fori_loop(0, 4, body, 0)


def _ffn_rows(h, wg, wu, wd):
    gate = _dot(h, wg)
    up = _dot(h, wu)
    act = (_silu(gate) * (up * FFN_HALF)).astype(BF16)
    return _dot(act, wd)


def _ffn_step(h_ref, wg, wu, wd, o_ref):
    o_ref[...] += _ffn_rows(h_ref[...], wg, wu, wd)


def _ffn_first_step(x_ref, g_ref, h_ref, wg, wu, wd, o_ref, tail, rc):
    tm, width = x_ref.shape
    cw = 512

    def norm_rows(a, b):
        def load(k):
            cols = slice(k * cw, (k + 1) * cw)
            x = x_ref[a:b, cols]
            if tail is not None and a >= tail[1]:
                use_tail, split_row, tail_ref = tail
                x = jnp.where(use_tail, tail_ref[a - split_row:b - split_row, cols], x)
            return x

        acc = None
        for k in range(width // cw):
            x = load(k)
            sq = _lane_sum(x * x)
            acc = sq if acc is None else acc + sq
        scale = lax.rsqrt(jnp.sum(acc, axis=-1, keepdims=True) * (1.0 / width) + EPS)
        for k in range(width // cw):
            cols = slice(k * cw, (k + 1) * cw)
            x = load(k)
            h_ref[a:b, cols] = ((x * scale) * g_ref[:, cols]).astype(BF16)
            o_ref[a:b, cols] = x

    for r in range(tm // rc):
        a, b = r * rc, (r + 1) * rc
        if tail is not None and a < tail[1] < b:
            norm_rows(a, tail[1])
            norm_rows(tail[1], b)
        else:
            norm_rows(a, b)
        o_ref[a:b, :] += _ffn_rows(h_ref[a:b, :], wg, wu, wd)


def _ffn_head_kernel(x_ref, g_ref, gf_ref, wg_ref, wu_ref, wd_ref,
                     o_ref, wg16_ref, wu16_ref, wd16_ref, h_ref, *, final_norm):
    f = pl.program_id(0)

    @pl.when(f == 0)
    def _():
        _rms_rows(x_ref, g_ref, h_ref, copy_ref=o_ref)

    wg = wg_ref[...].astype(BF16)
    wu = wu_ref[...].astype(BF16)
    wd = wd_ref[...].astype(BF16)
    wg16_ref[...] = wg
    wu16_ref[...] = wu
    wd16_ref[...] = wd
    _ffn_step(h_ref, wg, wu, wd, o_ref)

    if final_norm:
        @pl.when(f == pl.num_programs(0) - 1)
        def _():
            _rms_rows_inplace(o_ref, gf_ref)


def _ffn_tail_kernel(*refs, first, last, split_row, n_casts):
    refs = list(refs)
    x_ref, g_ref, gf_ref, wg_ref, wu_ref, wd_ref, _ = refs[:7]
    xs_ref = refs[7] if first else None
    n_in = 8 if first else 7
    cast_src = refs[n_in:n_in + n_casts]
    o_ref = refs[n_in + n_casts]
    cast_dst = refs[n_in + n_casts + 1:n_in + 2 * n_casts + 1]
    h_ref = refs[-1]
    i, f = pl.program_id(0), pl.program_id(1)
    last_tile = i == pl.num_programs(0) - 1

    for src, dst in zip(cast_src, cast_dst):
        tn = dst.shape[2]
        for c in range(dst.shape[0]):
            dst[c] = src[:, c * tn:(c + 1) * tn].astype(BF16)

    @pl.when(f == 0)
    def _():
        tail = (last_tile, split_row, xs_ref) if first else None
        _ffn_first_step(x_ref, g_ref, h_ref, wg_ref[...], wu_ref[...], wd_ref[...], o_ref,
                        tail, rc=x_ref.shape[0] // 4)

    @pl.when(f > 0)
    def _():
        _ffn_step(h_ref, wg_ref[...], wu_ref[...], wd_ref[...], o_ref)

    if last:
        @pl.when(f == pl.num_programs(1) - 1)
        def _():
            _rms_rows_inplace(o_ref, gf_ref)


def _ffn(x, xs, g3, gf2, w_gate, w_up, w_down, layer, which, casts, *, n_prompt, n_sample,
         first, last, tm, tf_head, tf_tail):
    d = x.shape[1]
    dff = w_gate.shape[-1]
    m = n_prompt + n_sample
    assert m % tm == 0
    n_tiles = m // tm
    split_row = n_prompt - (n_tiles - 1) * tm
    assert 0 < split_row < tm and split_row % 64 == 0 and tm - split_row == n_sample
    out_rows = m
    name = f"ffn_{layer}_{which}"

    once = pl.Buffered(1)
    nf_head = dff // tf_head
    assert tf_head == tf_tail
    head_out, wg16, wu16, wd16 = pl.pallas_call(
        functools.partial(_ffn_head_kernel, final_norm=last),
        grid=(nf_head,),
        in_specs=[
            pl.BlockSpec((tm, d), lambda f: (0, 0), pipeline_mode=once),
            pl.BlockSpec((None, 1, d), lambda f: (layer, 0, 0)),
            pl.BlockSpec((1, d), lambda f: (0, 0)),
            pl.BlockSpec((None, None, d, tf_head), lambda f: (layer, which, 0, f)),
            pl.BlockSpec((None, None, d, tf_head), lambda f: (layer, which, 0, f)),
            pl.BlockSpec((None, None, tf_head, d), lambda f: (layer, which, f, 0)),
        ],
        out_specs=[
            pl.BlockSpec((tm, d), lambda f: (0, 0), pipeline_mode=once),
            pl.BlockSpec((None, d, tf_head), lambda f: (f, 0, 0)),
            pl.BlockSpec((None, d, tf_head), lambda f: (f, 0, 0)),
            pl.BlockSpec((None, tf_head, d), lambda f: (f, 0, 0)),
        ],
        out_shape=[
            jax.ShapeDtypeStruct((out_rows, d), F32),
            jax.ShapeDtypeStruct((nf_head, d, tf_head), BF16),
            jax.ShapeDtypeStruct((nf_head, d, tf_head), BF16),
            jax.ShapeDtypeStruct((nf_head, tf_head, d), BF16),
        ],
        scratch_shapes=[pltpu.VMEM((tm, d), BF16)],
        compiler_params=_params(("arbitrary",)),
        name=name + "_head",
    )(x, g3, gf2, w_gate, w_up, w_down)

    in_specs = [
        pl.BlockSpec((tm, d), lambda i, f: (i + 1, 0)),
        pl.BlockSpec((None, 1, d), lambda i, f: (layer, 0, 0)),
        pl.BlockSpec((1, d), lambda i, f: (0, 0)),
        pl.BlockSpec((None, d, tf_tail), lambda i, f: (f, 0, 0)),
        pl.BlockSpec((None, d, tf_tail), lambda i, f: (f, 0, 0)),
        pl.BlockSpec((None, tf_tail, d), lambda i, f: (f, 0, 0)),
        pl.BlockSpec(memory_space=pl.ANY),
    ]
    args = [x, g3, gf2, wg16, wu16, wd16, head_out]
    if first:
        in_specs.append(pl.BlockSpec((n_sample, d), lambda i, f: (0, 0)))
        args.append(xs)
    nf = dff // tf_tail
    out_specs = [pl.BlockSpec((tm, d), lambda i, f: (i + 1, 0))]
    out_shape = [jax.ShapeDtypeStruct((out_rows, d), F32)]
    for w, lj, rb, tn in casts:
        _, rows, cols = w.shape
        n_blk = rows // rb
        assert rows % rb == 0 and cols % tn == 0 and n_blk <= (n_tiles - 1) * nf
        blk = lambda i, f, n_blk=n_blk: jnp.minimum(i * nf + f, n_blk - 1)
        in_specs.append(pl.BlockSpec((None, rb, cols), lambda i, f, lj=lj, blk=blk: (lj, blk(i, f), 0)))
        args.append(w)
        out_specs.append(pl.BlockSpec((cols // tn, rb, tn), lambda i, f, blk=blk: (0, blk(i, f), 0)))
        out_shape.append(jax.ShapeDtypeStruct((cols // tn, rows, tn), BF16))
    outs = pl.pallas_call(
        functools.partial(_ffn_tail_kernel, first=first, last=last, split_row=split_row,
                          n_casts=len(casts)),
        grid=(n_tiles - 1, nf),
        in_specs=in_specs,
        out_specs=out_specs,
        out_shape=out_shape,
        scratch_shapes=[pltpu.VMEM((tm, d), BF16)],
        input_output_aliases={6: 0},
        compiler_params=_params(("arbitrary", "arbitrary")),
        name=name + "_tail",
    )(*args)
    return outs[0], list(outs[1:])


def _a1_kernel(x_ref, g_ref, wa_ref, wg_ref, ba_ref, bg_ref, o_ref, h_ref):
    @pl.when(pl.program_id(1) == 0)
    def _():
        _rms_rows(x_ref, g_ref, h_ref)

    h = h_ref[...]
    a = _dot(h, wa_ref[...]) + ba_ref[...]
    gt = _dot(h, wg_ref[...]) + bg_ref[...]
    o_ref[...] = a * jax.nn.sigmoid(gt)


def _a1(x, g3, w1, b1, layer, j, *, row0, n_rows, tm):
    d = x.shape[1]
    nb, tn = w1.shape[0] // 2, w1.shape[2]
    d_a = nb * tn
    rb0 = row0 // tm
    return pl.pallas_call(
        _a1_kernel,
        grid=(n_rows // tm, nb),
        in_specs=[
            pl.BlockSpec((tm, d), lambda i, n: (rb0 + i, 0)),
            pl.BlockSpec((None, 1, d), lambda i, n: (layer, 0, 0)),
            pl.BlockSpec((None, d, tn), lambda i, n: (n, 0, 0)),
            pl.BlockSpec((None, d, tn), lambda i, n: (n + nb, 0, 0)),
            pl.BlockSpec((None, 1, tn), lambda i, n: (j, 0, n)),
            pl.BlockSpec((None, 1, tn), lambda i, n: (j, 0, n + nb)),
        ],
        out_specs=pl.BlockSpec((tm, tn), lambda i, n: (i, n)),
        out_shape=jax.ShapeDtypeStruct((n_rows, d_a), F32),
        scratch_shapes=[pltpu.VMEM((tm, d), BF16)],
        compiler_params=_params(("parallel", "arbitrary")),
        name=f"mix_a1s_{j}",
    )(x, g3, w1, w1, b1, b1)


def _dwconv_block(xp_ref, w_ref, r0, rows, cols, taps, halo):
    lead = halo - (taps - 1)
    assert halo % SUBLANES == 0 and lead >= 0
    acc = None
    for s in range(SUBLANES):
        group = [k for k in range(taps) if (k + lead) % SUBLANES == s]
        if not group:
            continue
        ext = rows + (SUBLANES if s else 0)
        part = None
        for k in group:
            base = r0 + ((k + lead) // SUBLANES) * SUBLANES
            term = xp_ref[pl.ds(base, ext), cols] * w_ref[k:k + 1, cols]
            part = term if part is None else part + term
        if s:
            part = part[s:s + rows]
        acc = part if acc is None else acc + part
    return acc


def _snake(i, k, nb):
    return jnp.where(i % 2 == 0, k, nb - 1 - k)


def _start_tile(x_ref, g_ref, h_ref, hal_ref, first_in_seq):
    _rms_rows(x_ref, g_ref, h_ref)

    @pl.when(first_in_seq)
    def _():
        hal_ref[...] = jnp.zeros(hal_ref.shape, hal_ref.dtype)


def _af_kernel(x_ref, xr_ref, g_ref, wa_ref, wg_ref, ba_ref, bgt_ref, wdw_ref, bdw_ref, lg_ref,
               lb_ref, w2_ref, o_ref, st_ref, h_ref, gbuf_ref, wbuf_ref, hal_ref, y_ref, act_ref,
               *, taps, tiles_per_seq, nb, rc):
    i, n = pl.program_id(0), pl.program_id(1)
    tm = x_ref.shape[0]
    tn = wa_ref.shape[1]
    halo = hal_ref.shape[0]
    cb = 128

    @pl.when(n == 0)
    def _():
        _start_tile(x_ref, g_ref, h_ref, hal_ref, i % tiles_per_seq == 0)

    @pl.when(n < nb)
    def _():
        col0 = pl.multiple_of(_snake(i, n, nb) * tn, tn)
        gbuf_ref[0:halo, :] = hal_ref[:, pl.ds(col0, tn)]
        wbuf_ref[0:taps, :] = wdw_ref[:, pl.ds(col0, tn)]
        wa, wg = wa_ref[...], wg_ref[...]
        ba, bgt = ba_ref[...], bgt_ref[...]
        for r in range(tm // rc):
            hc = h_ref[r * rc:(r + 1) * rc, :]
            glu = (_dot(hc, wa) + ba) * jax.nn.sigmoid(_dot(hc, wg) + bgt)
            gbuf_ref[halo + r * rc:halo + (r + 1) * rc, :] = glu
            for sub in range(rc // cb):
                r0 = r * rc + sub * cb
                for strip in range(tn // LANES):
                    cols = slice(strip * LANES, (strip + 1) * LANES)
                    y = _dwconv_block(gbuf_ref, wbuf_ref, r0, cb, cols, taps, halo)
                    y_ref[r0:r0 + cb, pl.ds(pl.multiple_of(col0 + strip * LANES, LANES), LANES)] = y
        hal_ref[:, pl.ds(col0, tn)] = gbuf_ref[tm:tm + halo, :]

    @pl.when(n == nb)
    def _():
        w2 = w2_ref[...]
        step, cw = 64, 512
        for r in range(tm // rc):
            for s in range(rc // step):
                blk = slice(r * rc + s * step, r * rc + (s + 1) * step)

                def load(c, blk=blk):
                    cols = slice(c * cw, (c + 1) * cw)
                    return y_ref[blk, cols] + bdw_ref[:, cols]

                def store(c, v, blk=blk):
                    act_ref[blk, c * cw:(c + 1) * cw] = _silu(v).astype(BF16)

                _layernorm_sweeps(load, store, y_ref.shape[1], lg_ref, lb_ref, cw)
            rows = slice(r * rc, (r + 1) * rc)
            o_ref[rows, :] = xr_ref[rows, :] + _dot(act_ref[rows, :], w2)

    @pl.when(n > nb)
    def _():
        o_ref[...] = xr_ref[...] + _dot(act_ref[...], w2_ref[...])

    @pl.when((n == 2 * nb - 1) & (i % tiles_per_seq == tiles_per_seq - 1))
    def _():
        st_ref[...] = hal_ref[halo - (taps - 1):halo, :]


def _a_fused(x, g3, w1, b1, w_dw, b_dw, ln_g, ln_b, w2, layer, j, *, n_prompt, seq, tm):
    m, d = x.shape
    taps = w_dw.shape[1]
    nb, tn = w2.shape[0], w2.shape[2]
    halo = 32
    assert w1.shape == (2 * nb, d, tn) and nb * tn == d and taps - 1 <= halo and seq % tm == 0
    kern = functools.partial(_af_kernel, taps=taps, tiles_per_seq=seq // tm, nb=nb, rc=256)
    taps_pad = -(-taps // SUBLANES) * SUBLANES
    blk1 = lambda i, n: _snake(i, jnp.minimum(n, nb - 1), nb)
    blk2 = lambda i, n: _snake(i, jnp.maximum(n - nb, 0), nb)
    row = lambda i, n: (j, 0, 0)
    return pl.pallas_call(
        kern,
        grid=(n_prompt // tm, 2 * nb),
        in_specs=[
            pl.BlockSpec((tm, d), _read_once_rows(n_prompt // tm)),
            pl.BlockSpec((tm, tn), lambda i, n: (i, blk2(i, n))),
            pl.BlockSpec((None, 1, d), lambda i, n: (layer, 0, 0)),
            pl.BlockSpec((None, d, tn), lambda i, n: (blk1(i, n), 0, 0)),
            pl.BlockSpec((None, d, tn), lambda i, n: (blk1(i, n) + nb, 0, 0)),
            pl.BlockSpec((None, 1, tn), lambda i, n: (j, 0, blk1(i, n))),
            pl.BlockSpec((None, 1, tn), lambda i, n: (j, 0, blk1(i, n) + nb)),
            pl.BlockSpec((None, taps, d), row),
            pl.BlockSpec((None, 1, d), row),
            pl.BlockSpec((None, 1, d), row),
            pl.BlockSpec((None, 1, d), row),
            pl.BlockSpec((None, d, tn), lambda i, n: (blk2(i, n), 0, 0)),
        ],
        out_specs=[
            pl.BlockSpec((tm, tn), lambda i, n: (i, blk2(i, n))),
            pl.BlockSpec((None, taps - 1, d), lambda i, n: (i // (seq // tm), 0, 0)),
        ],
        out_shape=[
            jax.ShapeDtypeStruct((m, d), F32),
            jax.ShapeDtypeStruct((n_prompt // seq, taps - 1, d), F32),
        ],
        scratch_shapes=[
            pltpu.VMEM((tm, d), BF16),
            pltpu.VMEM((halo + tm, tn), F32),
            pltpu.VMEM((taps_pad, tn), F32),
            pltpu.VMEM((halo, d), F32),
            pltpu.VMEM((tm, d), F32),
            pltpu.VMEM((tm, d), BF16),
        ],
        compiler_params=_params(("arbitrary", "arbitrary")),
        name=f"mix_a_{j}",
    )(x, x, g3, w1, w1, b1, b1, w_dw, b_dw, ln_g, ln_b, w2)


def _a_sconv_kernel(*refs):
    st_ref, glu_ref, w_ref, b_ref = refs[:4]
    y_ref, ns_ref = refs[-2:]
    past = st_ref.shape[0]
    acc = st_ref[0] * w_ref[0:1, :]
    for k in range(1, past):
        acc = acc + st_ref[k] * w_ref[k:k + 1, :]
        ns_ref[k - 1] = st_ref[k]
    g = glu_ref[...]
    ns_ref[past - 1] = g
    y_ref[...] = acc + g * w_ref[past:past + 1, :] + b_ref[...]


def _a_sample_conv(state_t, glu, w_dw, b_dw, prev, j, *, tn):
    n_layers, past, nb, d = state_t.shape
    taps = w_dw.shape[1]
    assert taps == past + 1 and glu.shape == (nb, d)
    in_specs = [
        pl.BlockSpec((None, past, nb, tn), lambda n: (j, 0, 0, n)),
        pl.BlockSpec((nb, tn), lambda n: (0, n)),
        pl.BlockSpec((None, taps, tn), lambda n: (j, 0, n)),
        pl.BlockSpec((None, 1, tn), lambda n: (j, 0, n)),
    ]
    args = [state_t, glu, w_dw, b_dw]
    aliases = {}
    if prev is not None:
        in_specs.append(pl.BlockSpec(memory_space=pl.ANY))
        args.append(prev)
        aliases = {4: 1}
    return pl.pallas_call(
        _a_sconv_kernel,
        grid=(d // tn,),
        in_specs=in_specs,
        out_specs=[
            pl.BlockSpec((nb, tn), lambda n: (0, n)),
            pl.BlockSpec((None, past, nb, tn), lambda n: (j, 0, 0, n)),
        ],
        out_shape=[
            jax.ShapeDtypeStruct((nb, d), F32),
            jax.ShapeDtypeStruct((n_layers, past, nb, d), F32),
        ],
        input_output_aliases=aliases,
        compiler_params=_params(("parallel",)),
        name=f"mix_a_sconv_{j}",
    )(*args)


def _a2s_kernel(x_ref, y_ref, lg_ref, lb_ref, w2_ref, _, o_ref, act_ref):
    @pl.when(pl.program_id(0) == 0)
    def _():
        act_ref[...] = _silu(_layernorm(y_ref[...], lg_ref[...], lb_ref[...])).astype(BF16)

    o_ref[...] = x_ref[...] + _dot(act_ref[...], w2_ref[...])


def _a2_sample(x, x_new, y, ln_g, ln_b, w2, j, *, n_prompt):
    m, d = x.shape
    ns = y.shape[0]
    tn = w2.shape[2]
    rb = n_prompt // ns
    return pl.pallas_call(
        _a2s_kernel,
        grid=(d // tn,),
        in_specs=[
            pl.BlockSpec((ns, tn), lambda n: (rb, n)),
            pl.BlockSpec((ns, d), lambda n: (0, 0)),
            pl.BlockSpec((None, 1, d), lambda n: (j, 0, 0)),
            pl.BlockSpec((None, 1, d), lambda n: (j, 0, 0)),
            pl.BlockSpec((None, d, tn), lambda n: (n, 0, 0)),
            pl.BlockSpec(memory_space=pl.ANY),
        ],
        out_specs=pl.BlockSpec((ns, tn), lambda n: (rb, n)),
        out_shape=jax.ShapeDtypeStruct((m, d), F32),
        scratch_shapes=[pltpu.VMEM((ns, d), BF16)],
        input_output_aliases={5: 0},
        compiler_params=_params(("arbitrary",)),
        name=f"mix_a2s_{j}",
    )(x, y, ln_g, ln_b, w2, x_new)


def _b1_kernel(x_ref, g_ref, w_ref, b_ref, lg_ref, lb_ref, o_ref, h_ref):
    n = pl.program_id(1)
    tn = w_ref.shape[1]

    @pl.when(n == 0)
    def _():
        _rms_rows(x_ref, g_ref, h_ref)

    z = _dot(h_ref[...], w_ref[...]) + b_ref[...]
    o_ref[:, pl.ds(pl.multiple_of(n * tn, tn), tn)] = _gelu(z)

    @pl.when(n == pl.num_programs(1) - 1)
    def _():
        rows, step, cw = o_ref.shape[0], 64, 512

        def body(c, carry):
            blk = pl.ds(pl.multiple_of(c * step, step), step)

            def load(k):
                return o_ref[blk, k * cw:(k + 1) * cw]

            def store(k, v):
                o_ref[blk, k * cw:(k + 1) * cw] = v

            _layernorm_sweeps(load, store, o_ref.shape[1], lg_ref, lb_ref, cw)
            return carry

        lax.fori_loop(0, rows // step, body, 0)


def _b1(x, g3, w_in, b_in, ln_g, ln_b, layer, j, *, tm):
    m, d = x.shape
    nb, tn = w_in.shape[0] // 2, w_in.shape[2]
    d_sgu = nb * tn
    return pl.pallas_call(
        _b1_kernel,
        grid=(m // tm, nb),
        in_specs=[
            pl.BlockSpec((tm, d), lambda i, n: (i, 0)),
            pl.BlockSpec((None, 1, d), lambda i, n: (layer, 0, 0)),
            pl.BlockSpec((None, d, tn), lambda i, n: (n + nb, 0, 0)),
            pl.BlockSpec((None, 1, tn), lambda i, n: (j, 0, n + nb)),
            pl.BlockSpec((None, 1, d_sgu), lambda i, n: (j, 0, 0)),
            pl.BlockSpec((None, 1, d_sgu), lambda i, n: (j, 0, 0)),
        ],
        out_specs=pl.BlockSpec((tm, d_sgu), lambda i, n: (i, 0)),
        out_shape=jax.ShapeDtypeStruct((m, d_sgu), F32),
        scratch_shapes=[pltpu.VMEM((tm, d), BF16)],
        compiler_params=_params(("parallel", "arbitrary")),
        name=f"mix_b1_{j}",
    )(x, g3, w_in, b_in, ln_g, ln_b)


def _b2_kernel(x_ref, g_ref, v_ref, wi_ref, bi_ref, ws_ref, bs_ref, wo_ref, o_ref, h_ref,
               *, n_prompt_chunks):
    i = pl.program_id(0)
    tm = x_ref.shape[0]

    @pl.when(pl.program_id(1) == 0)
    def _():
        _rms_rows(x_ref, g_ref, h_ref, copy_ref=o_ref)

    u = _gelu(_dot(h_ref[...], wi_ref[...]) + bi_ref[...])
    ws = ws_ref[...]
    row = lax.broadcasted_iota(jnp.int32, ws.shape, 0)
    col = lax.broadcasted_iota(jnp.int32, ws.shape, 1)
    w_tril = jnp.where(col <= row, ws, 0.0).astype(BF16)
    bs = bs_ref[...]
    parts = []
    for c in range(tm // CHUNK):
        rows = slice(c * CHUNK, (c + 1) * CHUNK)
        v = v_ref[rows, :]
        s_prompt = _dot(w_tril, v.astype(BF16)) + bs
        s_sample = ws[0:1, 0:1] * v + bs[0:1, :]
        is_sample = i * (tm // CHUNK) + c >= n_prompt_chunks
        parts.append(u[rows, :] * jnp.where(is_sample, s_sample, s_prompt))
    y = jnp.concatenate(parts, axis=0).astype(BF16)
    o_ref[...] += _dot(y, wo_ref[...])


def _b2(x, g3, vn, w_in, b_in, w_s, b_s3, w_out, layer, j, *, n_prompt, tm):
    m, d = x.shape
    d_sgu = vn.shape[1]
    gw = d_sgu // N_SGU_GROUPS
    assert w_in.shape == (2 * N_SGU_GROUPS, d, gw) and w_out.shape == (1, d_sgu, d)
    kern = functools.partial(_b2_kernel, n_prompt_chunks=n_prompt // CHUNK)
    return pl.pallas_call(
        kern,
        grid=(m // tm, N_SGU_GROUPS),
        in_specs=[
            pl.BlockSpec((tm, d), lambda i, g: (i, 0)),
            pl.BlockSpec((None, 1, d), lambda i, g: (layer, 0, 0)),
            pl.BlockSpec((tm, gw), lambda i, g: (i, g)),
            pl.BlockSpec((None, d, gw), lambda i, g: (g, 0, 0)),
            pl.BlockSpec((None, 1, gw), lambda i, g: (j, 0, g)),
            pl.BlockSpec((None, None, CHUNK, CHUNK), lambda i, g: (j, g, 0, 0)),
            pl.BlockSpec((None, None, CHUNK, 1), lambda i, g: (j, g, 0, 0)),
            pl.BlockSpec((None, gw, d), lambda i, g: (0, g, 0)),
        ],
        out_specs=pl.BlockSpec((tm, d), lambda i, g: (i, 0)),
        out_shape=jax.ShapeDtypeStruct((m, d), F32),
        scratch_shapes=[pltpu.VMEM((tm, d), BF16)],
        compiler_params=_params(("parallel", "arbitrary")),
        name=f"mix_b2_{j}",
    )(x, g3, vn, w_in, b_in, w_s, b_s3, w_out)


def _c1_kernel(x_ref, g_ref, wb_ref, wc_ref, wx_ref, bg_ref, cx_ref, h_ref):
    @pl.when(pl.program_id(1) == 0)
    def _():
        _rms_rows(x_ref, g_ref, h_ref)

    h = h_ref[...]
    bg_ref[...] = _dot(h, wb_ref[...])
    cx_ref[...] = _dot(h, wc_ref[...]) * _dot(h, wx_ref[...])


def _c1(x, g3, w_in, layer, j, *, row0, n_rows, tm):
    d = x.shape[1]
    nb, tn = w_in.shape[0] // 3, w_in.shape[2]
    d_c = nb * tn
    rb0 = row0 // tm
    out = jax.ShapeDtypeStruct((n_rows, d_c), F32)
    return pl.pallas_call(
        _c1_kernel,
        grid=(n_rows // tm, nb),
        in_specs=[
            pl.BlockSpec((tm, d), lambda i, n: (rb0 + i, 0)),
            pl.BlockSpec((None, 1, d), lambda i, n: (layer, 0, 0)),
            pl.BlockSpec((None, d, tn), lambda i, n: (n, 0, 0)),
            pl.BlockSpec((None, d, tn), lambda i, n: (n + nb, 0, 0)),
            pl.BlockSpec((None, d, tn), lambda i, n: (n + 2 * nb, 0, 0)),
        ],
        out_specs=[pl.BlockSpec((tm, tn), lambda i, n: (i, n))] * 2,
        out_shape=[out, out],
        scratch_shapes=[pltpu.VMEM((tm, d), BF16)],
        compiler_params=_params(("parallel", "arbitrary")),
        name=f"mix_c1s_{j}",
    )(x, g3, w_in, w_in, w_in)


def _cf_kernel(x_ref, xr_ref, g_ref, wb_ref, wc_ref, wx_ref, wcv_ref, wo_ref, o_ref, st_ref,
               h_ref, gbuf_ref, wbuf_ref, hal_ref, act_ref, *, taps, tiles_per_seq, nb, rc):
    i, n = pl.program_id(0), pl.program_id(1)
    tm = x_ref.shape[0]
    tn = wb_ref.shape[1]
    halo = hal_ref.shape[0]
    cb = 128

    @pl.when(n == 0)
    def _():
        _start_tile(x_ref, g_ref, h_ref, hal_ref, i % tiles_per_seq == 0)

    @pl.when(n < nb)
    def _():
        col0 = pl.multiple_of(_snake(i, n, nb) * tn, tn)
        gbuf_ref[0:halo, :] = hal_ref[:, pl.ds(col0, tn)]
        wbuf_ref[0:taps, :] = wcv_ref[:, pl.ds(col0, tn)]
        wb, wc, wx = wb_ref[...], wc_ref[...], wx_ref[...]
        for r in range(tm // rc):
            hc = h_ref[r * rc:(r + 1) * rc, :]
            bg = _dot(hc, wb)
            gbuf_ref[halo + r * rc:halo + (r + 1) * rc, :] = _dot(hc, wc) * _dot(hc, wx)
            for sub in range(rc // cb):
                r0 = r * rc + sub * cb
                for strip in range(tn // LANES):
                    cols = slice(strip * LANES, (strip + 1) * LANES)
                    y = _dwconv_block(gbuf_ref, wbuf_ref, r0, cb, cols, taps, halo)
                    gate = bg[sub * cb:(sub + 1) * cb, cols]
                    act_ref[r0:r0 + cb, pl.ds(pl.multiple_of(col0 + strip * LANES, LANES), LANES)] = (
                        gate * y).astype(BF16)
        hal_ref[:, pl.ds(col0, tn)] = gbuf_ref[tm:tm + halo, :]

    @pl.when(n >= nb)
    def _():
        o_ref[...] = xr_ref[...] + _dot(act_ref[...], wo_ref[...])

    @pl.when((n == 2 * nb - 1) & (i % tiles_per_seq == tiles_per_seq - 1))
    def _():
        st_ref[...] = hal_ref[halo - (taps - 1):halo, :]


def _c_fused(x, g3, w_in, w_conv, w_out, layer, j, *, n_prompt, seq, tm):
    m, d = x.shape
    taps = w_conv.shape[1]
    nb, tn = w_out.shape[0], w_out.shape[2]
    halo = SUBLANES
    assert w_in.shape == (3 * nb, d, tn) and nb * tn == d and taps - 1 <= halo and seq % tm == 0
    kern = functools.partial(_cf_kernel, taps=taps, tiles_per_seq=seq // tm, nb=nb, rc=256)
    taps_pad = -(-taps // SUBLANES) * SUBLANES
    blk1 = lambda i, n: _snake(i, jnp.minimum(n, nb - 1), nb)
    blk2 = lambda i, n: _snake(i, jnp.maximum(n - nb, 0), nb)
    blk = lambda part: (lambda i, n: (blk1(i, n) + part * nb, 0, 0))
    return pl.pallas_call(
        kern,
        grid=(n_prompt // tm, 2 * nb),
        in_specs=[
            pl.BlockSpec((tm, d), _read_once_rows(n_prompt // tm)),
            pl.BlockSpec((tm, tn), lambda i, n: (i, blk2(i, n))),
            pl.BlockSpec((None, 1, d), lambda i, n: (layer, 0, 0)),
            pl.BlockSpec((None, d, tn), blk(0)),
            pl.BlockSpec((None, d, tn), blk(1)),
            pl.BlockSpec((None, d, tn), blk(2)),
            pl.BlockSpec((None, taps, d), lambda i, n: (j, 0, 0)),
            pl.BlockSpec((None, d, tn), lambda i, n: (blk2(i, n), 0, 0)),
        ],
        out_specs=[
            pl.BlockSpec((tm, tn), lambda i, n: (i, blk2(i, n))),
            pl.BlockSpec((None, taps - 1, d), lambda i, n: (i // (seq // tm), 0, 0)),
        ],
        out_shape=[
            jax.ShapeDtypeStruct((m, d), F32),
            jax.ShapeDtypeStruct((n_prompt // seq, taps - 1, d), F32),
        ],
        scratch_shapes=[
            pltpu.VMEM((tm, d), BF16),
            pltpu.VMEM((halo + tm, tn), F32),
            pltpu.VMEM((taps_pad, tn), F32),
            pltpu.VMEM((halo, d), F32),
            pltpu.VMEM((tm, d), BF16),
        ],
        compiler_params=_params(("arbitrary", "arbitrary")),
        name=f"mix_c_{j}",
    )(x, x, g3, w_in, w_in, w_in, w_conv, w_out)


def _c2s_kernel(x_ref, bg_ref, cx_ref, s0_ref, s1_ref, w_ref, wo_ref, _, o_ref, act_ref):
    @pl.when(pl.program_id(0) == 0)
    def _():
        w = w_ref[...]
        y = s0_ref[...] * w[0:1] + s1_ref[...] * w[1:2] + cx_ref[...] * w[2:3]
        act_ref[...] = (bg_ref[...] * y).astype(BF16)

    o_ref[...] = x_ref[...] + _dot(act_ref[...], wo_ref[...])


def _c2_sample(x, x_new, bg, cx, s0, s1, w_conv, w_out, j, *, n_prompt):
    m, d = x.shape
    ns = s0.shape[0]
    tn = w_out.shape[2]
    rb = n_prompt // ns
    taps = w_conv.shape[1]
    assert taps == 3
    return pl.pallas_call(
        _c2s_kernel,
        grid=(d // tn,),
        in_specs=[
            pl.BlockSpec((ns, tn), lambda n: (rb, n)),
            pl.BlockSpec((ns, d), lambda n: (0, 0)),
            pl.BlockSpec((ns, d), lambda n: (0, 0)),
            pl.BlockSpec((ns, d), lambda n: (0, 0)),
            pl.BlockSpec((ns, d), lambda n: (0, 0)),
            pl.BlockSpec((None, taps, d), lambda n: (j, 0, 0)),
            pl.BlockSpec((None, d, tn), lambda n: (n, 0, 0)),
            pl.BlockSpec(memory_space=pl.ANY),
        ],
        out_specs=pl.BlockSpec((ns, tn), lambda n: (rb, n)),
        out_shape=jax.ShapeDtypeStruct((m, d), F32),
        scratch_shapes=[pltpu.VMEM((ns, d), BF16)],
        input_output_aliases={7: 0},
        compiler_params=_params(("arbitrary",)),
        name=f"mix_c2s_{j}",
    )(x, bg, cx, s0, s1, w_conv, w_out, x_new)


def _last_rows(a, batch, seq, rows):
    return jnp.stack([a[(b + 1) * seq - rows:(b + 1) * seq] for b in range(batch)])


def kernel(x_prompt, x_sample, state_conv_a, state_conv_c, g_ffn1, g_mix, g_ffn2, g_final,
           w_ffn_gate, w_ffn_up, w_ffn_down,
           a_w_pw1, a_b_pw1, a_w_dw, a_b_dw, a_ln_g, a_ln_b, a_w_pw2,
           b_w_in, b_b_in, b_ln_g, b_ln_b, b_w_s, b_b_s, b_w_out,
           c_w_in, c_w_conv, c_w_out):
    batch, seq, d = x_prompt.shape
    n_sample = x_sample.shape[0]
    assert x_sample.shape[1] == 1
    n_prompt = batch * seq
    depth = g_ffn1.shape[0]
    past_c = state_conv_c.shape[2]
    assert past_c == 2
    d_sgu = b_ln_g.shape[-1]

    row3 = lambda a: a.reshape(a.shape[0], 1, a.shape[1])
    g1, gm, g2 = row3(g_ffn1), row3(g_mix), row3(g_ffn2)
    gf = g_final.reshape(1, d)
    a_b1, a_bd, a_lg, a_lb = row3(a_b_pw1), row3(a_b_dw), row3(a_ln_g), row3(a_ln_b)
    b_bi, b_lg, b_lb = row3(b_b_in), row3(b_ln_g), row3(b_ln_b)
    b_bs = b_b_s.reshape(*b_b_s.shape, 1)
    state_a_t = jnp.transpose(state_conv_a, (0, 2, 1, 3))

    ffn = functools.partial(_ffn, n_prompt=n_prompt, n_sample=n_sample,
                            tm=832, tf_head=512, tf_tail=512)
    new_a_p, new_b_p, new_b_s, new_c_p, new_c_s = [], [], [], [], []
    new_a_s_t = None

    x = x_prompt.reshape(n_prompt, d)
    xs = x_sample.reshape(n_sample, d)
    gw = d_sgu // N_SGU_GROUPS
    for i in range(depth):
        kind, j = i % 3, i // 3
        if kind == 0:
            casts = [(a_w_pw1, j, 32, 512), (a_w_pw2, j, 32, 512)]
        elif kind == 1:
            casts = [(b_w_in, j, 32, gw), (b_w_out, j, 64, d)]
        else:
            casts = [(c_w_in, j, 32, 512), (c_w_out, j, 32, 512)]
        x, (w_a16, w_b16) = ffn(x, xs, g1, gf, w_ffn_gate, w_ffn_up, w_ffn_down, i, 0, casts,
                                first=(i == 0), last=False)
        if kind == 0:
            glu_s = _a1(x, gm, w_a16, a_b1, i, j, row0=n_prompt, n_rows=n_sample, tm=n_sample)
            y_s, new_a_s_t = _a_sample_conv(state_a_t, glu_s, a_w_dw, a_bd, new_a_s_t, j, tn=256)
            x_new, st_p = _a_fused(x, gm, w_a16, a_b1, a_w_dw, a_bd, a_lg, a_lb, w_b16, i, j,
                                   n_prompt=n_prompt, seq=seq, tm=1024)
            x = _a2_sample(x, x_new, y_s, a_lg, a_lb, w_b16, j, n_prompt=n_prompt)
            new_a_p.append(st_p)
        elif kind == 1:
            vn = _b1(x, gm, w_a16, b_bi, b_lg, b_lb, i, j, tm=832)
            x = _b2(x, gm, vn, w_a16, b_bi, b_w_s, b_bs, w_b16, i, j, n_prompt=n_prompt, tm=640)
            new_b_p.append(_last_rows(vn, batch, seq, CHUNK))
            new_b_s.append(vn[n_prompt:].reshape(n_sample, 1, d_sgu))
        else:
            bg_s, cx_s = _c1(x, gm, w_a16, i, j, row0=n_prompt, n_rows=n_sample, tm=n_sample)
            s0, s1 = state_conv_c[j, :, 0], state_conv_c[j, :, 1]
            x_new, st_p = _c_fused(x, gm, w_a16, c_w_conv, w_b16, i, j,
                                   n_prompt=n_prompt, seq=seq, tm=1024)
            x = _c2_sample(x, x_new, bg_s, cx_s, s0, s1, c_w_conv, w_b16, j, n_prompt=n_prompt)
            new_c_p.append(st_p)
            new_c_s.append(jnp.stack([s1, cx_s], axis=1))
        last = i == depth - 1
        x, _ = ffn(x, xs, g2, gf, w_ffn_gate, w_ffn_up, w_ffn_down, i, 1, [],
                   first=False, last=last)

    return (x[:n_prompt].reshape(batch, seq, d), x[n_prompt:].reshape(n_sample, 1, d),
            jnp.stack(new_a_p), jnp.transpose(new_a_s_t, (0, 2, 1, 3)),
            jnp.stack(new_b_p), jnp.stack(new_b_s),
            jnp.stack(new_c_p), jnp.stack(new_c_s))
```

```python
import functools

import jax
import jax.numpy as jnp
from jax import lax
from jax.experimental import pallas as pl
from jax.experimental.pallas import tpu as pltpu

F32 = jnp.float32
BF16 = jnp.bfloat16

EPS = 1e-6
FFN_HALF = 0.5
CHUNK = 128
N_SGU_GROUPS = 8
INV_SQRT2 = 0.7071067811865476
SUBLANES = 8
LANES = 128

V7X_VMEM_BYTES = 64 * 1024 * 1024
VMEM_LIMIT = V7X_VMEM_BYTES - 6 * 1024 * 1024


def _params(sem):
    return pltpu.CompilerParams(dimension_semantics=sem, vmem_limit_bytes=VMEM_LIMIT)


def _read_once_rows(n_tiles):
    return lambda i, s: (jnp.minimum(i + jnp.minimum(s, 1), n_tiles - 1), 0)


def _layernorm(y, g, b):
    mu = jnp.mean(y, axis=-1, keepdims=True)
    yc = y - mu
    var = jnp.mean(yc * yc, axis=-1, keepdims=True)
    return (yc * lax.rsqrt(var + EPS)) * g + b


def _lane_sum(v):
    acc = v[:, 0:LANES]
    for t in range(1, v.shape[1] // LANES):
        acc = acc + v[:, t * LANES:(t + 1) * LANES]
    return acc


def _layernorm_sweeps(load, store, width, g_ref, b_ref, cw=512):
    lane_sum = _lane_sum
    n = width // cw
    acc = lane_sum(load(0))
    for c in range(1, n):
        acc = acc + lane_sum(load(c))
    mu = jnp.sum(acc, axis=-1, keepdims=True) * (1.0 / width)
    acc = None
    for c in range(n):
        dev = load(c) - mu
        sq = lane_sum(dev * dev)
        acc = sq if acc is None else acc + sq
    var = jnp.sum(acc, axis=-1, keepdims=True) * (1.0 / width)
    scale = lax.rsqrt(var + EPS)
    for c in range(n):
        cols = slice(c * cw, (c + 1) * cw)
        store(c, ((load(c) - mu) * scale) * g_ref[:, cols] + b_ref[:, cols])


def _silu(x):
    return x * jax.nn.sigmoid(x)


def _gelu(x):
    return 0.5 * x * (1.0 + lax.erf(x * INV_SQRT2))


def _dot(a, w):
    return jnp.dot(a, w, preferred_element_type=F32)


def _rms_rows(x_ref, g_ref, h_ref, copy_ref=None, cw=512):
    rows, width = x_ref.shape
    step = rows // 4
    assert rows % 4 == 0 and step % 16 == 0 and width % cw == 0

    def body(c, carry):
        blk = pl.ds(pl.multiple_of(c * step, step), step)
        acc = None
        for k in range(width // cw):
            x = x_ref[blk, k * cw:(k + 1) * cw]
            sq = _lane_sum(x * x)
            acc = sq if acc is None else acc + sq
        scale = lax.rsqrt(jnp.sum(acc, axis=-1, keepdims=True) * (1.0 / width) + EPS)
        for k in range(width // cw):
            cols = slice(k * cw, (k + 1) * cw)
            x = x_ref[blk, cols]
            h_ref[blk, cols] = ((x * scale) * g_ref[:, cols]).astype(BF16)
            if copy_ref is not None:
                copy_ref[blk, cols] = x
        return carry

    lax.fori_loop(0, 4, body, 0)


def _rms_rows_inplace(o_ref, g_ref, cw=512):
    rows, width = o_ref.shape
    step = rows // 4
    assert rows % 4 == 0 and step % SUBLANES == 0 and width % cw == 0

    def body(c, carry):
        blk = pl.ds(pl.multiple_of(c * step, step), step)
        acc = None
        for k in range(width // cw):
            x = o_ref[blk, k * cw:(k + 1) * cw]
            sq = _lane_sum(x * x)
            acc = sq if acc is None else acc + sq
        scale = lax.rsqrt(jnp.sum(acc, axis=-1, keepdims=True) * (1.0 / width) + EPS)
        for k in range(width // cw):
            cols = slice(k * cw, (k + 1) * cw)
            o_ref[blk, cols] = (o_ref[blk, cols] * scale) * g_ref[:, cols]
        return carry

    lax.fori_loop(0, 4, body, 0)


def _ffn_rows(h, wg, wu, wd):
    gate = _dot(h, wg)
    up = _dot(h, wu)
    act = (_silu(gate) * (up * FFN_HALF)).astype(BF16)
    return _dot(act, wd)


def _ffn_step(h_ref, wg, wu, wd, o_ref):
    o_ref[...] += _ffn_rows(h_ref[...], wg, wu, wd)


def _ffn_first_step(x_ref, g_ref, h_ref, wg, wu, wd, o_ref, tail, rc):
    tm, width = x_ref.shape
    cw = 512

    def norm_rows(a, b):
        def load(k):
            cols = slice(k * cw, (k + 1) * cw)
            x = x_ref[a:b, cols]
            if tail is not None and a >= tail[1]:
                use_tail, split_row, tail_ref = tail
                x = jnp.where(use_tail, tail_ref[a - split_row:b - split_row, cols], x)
            return x

        acc = None
        for k in range(width // cw):
            x = load(k)
            sq = _lane_sum(x * x)
            acc = sq if acc is None else acc + sq
        scale = lax.rsqrt(jnp.sum(acc, axis=-1, keepdims=True) * (1.0 / width) + EPS)
        for k in range(width // cw):
            cols = slice(k * cw, (k + 1) * cw)
            x = load(k)
            h_ref[a:b, cols] = ((x * scale) * g_ref[:, cols]).astype(BF16)
            o_ref[a:b, cols] = x

    for r in range(tm // rc):
        a, b = r * rc, (r + 1) * rc
        if tail is not None and a < tail[1] < b:
            norm_rows(a, tail[1])
            norm_rows(tail[1], b)
        else:
            norm_rows(a, b)
        o_ref[a:b, :] += _ffn_rows(h_ref[a:b, :], wg, wu, wd)


def _ffn_head_kernel(x_ref, g_ref, gf_ref, wg_ref, wu_ref, wd_ref,
                     o_ref, wg16_ref, wu16_ref, wd16_ref, h_ref, *, final_norm):
    f = pl.program_id(0)

    @pl.when(f == 0)
    def _():
        _rms_rows(x_ref, g_ref, h_ref, copy_ref=o_ref)

    wg = wg_ref[...].astype(BF16)
    wu = wu_ref[...].astype(BF16)
    wd = wd_ref[...].astype(BF16)
    wg16_ref[...] = wg
    wu16_ref[...] = wu
    wd16_ref[...] = wd
    _ffn_step(h_ref, wg, wu, wd, o_ref)

    if final_norm:
        @pl.when(f == pl.num_programs(0) - 1)
        def _():
            _rms_rows_inplace(o_ref, gf_ref)


def _ffn_tail_kernel(*refs, first, last, split_row, n_casts):
    refs = list(refs)
    x_ref, g_ref, gf_ref, wg_ref, wu_ref, wd_ref, _ = refs[:7]
    xs_ref = refs[7] if first else None
    n_in = 8 if first else 7
    cast_src = refs[n_in:n_in + n_casts]
    o_ref = refs[n_in + n_casts]
    cast_dst = refs[n_in + n_casts + 1:n_in + 2 * n_casts + 1]
    h_ref = refs[-1]
    i, f = pl.program_id(0), pl.program_id(1)
    last_tile = i == pl.num_programs(0) - 1

    for src, dst in zip(cast_src, cast_dst):
        tn = dst.shape[2]
        for c in range(dst.shape[0]):
            dst[c] = src[:, c * tn:(c + 1) * tn].astype(BF16)

    @pl.when(f == 0)
    def _():
        tail = (last_tile, split_row, xs_ref) if first else None
        _ffn_first_step(x_ref, g_ref, h_ref, wg_ref[...], wu_ref[...], wd_ref[...], o_ref,
                        tail, rc=x_ref.shape[0] // 4)

    @pl.when(f > 0)
    def _():
        _ffn_step(h_ref, wg_ref[...], wu_ref[...], wd_ref[...], o_ref)

    if last:
        @pl.when(f == pl.num_programs(1) - 1)
        def _():
            _rms_rows_inplace(o_ref, gf_ref)


def _ffn(x, xs, g3, gf2, w_gate, w_up, w_down, layer, which, casts, *, n_prompt, n_sample,
         first, last, tm, tf_head, tf_tail):
    d = x.shape[1]
    dff = w_gate.shape[-1]
    m = n_prompt + n_sample
    assert m % tm == 0
    n_tiles = m // tm
    split_row = n_prompt - (n_tiles - 1) * tm
    assert 0 < split_row < tm and split_row % 64 == 0 and tm - split_row == n_sample
    out_rows = m
    name = f"ffn_{layer}_{which}"

    once = pl.Buffered(1)
    nf_head = dff // tf_head
    assert tf_head == tf_tail
    head_out, wg16, wu16, wd16 = pl.pallas_call(
        functools.partial(_ffn_head_kernel, final_norm=last),
        grid=(nf_head,),
        in_specs=[
            pl.BlockSpec((tm, d), lambda f: (0, 0), pipeline_mode=once),
            pl.BlockSpec((None, 1, d), lambda f: (layer, 0, 0)),
            pl.BlockSpec((1, d), lambda f: (0, 0)),
            pl.BlockSpec((None, None, d, tf_head), lambda f: (layer, which, 0, f)),
            pl.BlockSpec((None, None, d, tf_head), lambda f: (layer, which, 0, f)),
            pl.BlockSpec((None, None, tf_head, d), lambda f: (layer, which, f, 0)),
        ],
        out_specs=[
            pl.BlockSpec((tm, d), lambda f: (0, 0), pipeline_mode=once),
            pl.BlockSpec((None, d, tf_head), lambda f: (f, 0, 0)),
            pl.BlockSpec((None, d, tf_head), lambda f: (f, 0, 0)),
            pl.BlockSpec((None, tf_head, d), lambda f: (f, 0, 0)),
        ],
        out_shape=[
            jax.ShapeDtypeStruct((out_rows, d), F32),
            jax.ShapeDtypeStruct((nf_head, d, tf_head), BF16),
            jax.ShapeDtypeStruct((nf_head, d, tf_head), BF16),
            jax.ShapeDtypeStruct((nf_head, tf_head, d), BF16),
        ],
        scratch_shapes=[pltpu.VMEM((tm, d), BF16)],
        compiler_params=_params(("arbitrary",)),
        name=name + "_head",
    )(x, g3, gf2, w_gate, w_up, w_down)

    in_specs = [
        pl.BlockSpec((tm, d), lambda i, f: (i + 1, 0)),
        pl.BlockSpec((None, 1, d), lambda i, f: (layer, 0, 0)),
        pl.BlockSpec((1, d), lambda i, f: (0, 0)),
        pl.BlockSpec((None, d, tf_tail), lambda i, f: (f, 0, 0)),
        pl.BlockSpec((None, d, tf_tail), lambda i, f: (f, 0, 0)),
        pl.BlockSpec((None, tf_tail, d), lambda i, f: (f, 0, 0)),
        pl.BlockSpec(memory_space=pl.ANY),
    ]
    args = [x, g3, gf2, wg16, wu16, wd16, head_out]
    if first:
        in_specs.append(pl.BlockSpec((n_sample, d), lambda i, f: (0, 0)))
        args.append(xs)
    nf = dff // tf_tail
    out_specs = [pl.BlockSpec((tm, d), lambda i, f: (i + 1, 0))]
    out_shape = [jax.ShapeDtypeStruct((out_rows, d), F32)]
    for w, lj, rb, tn in casts:
        _, rows, cols = w.shape
        n_blk = rows // rb
        assert rows % rb == 0 and cols % tn == 0 and n_blk <= (n_tiles - 1) * nf
        blk = lambda i, f, n_blk=n_blk: jnp.minimum(i * nf + f, n_blk - 1)
        in_specs.append(pl.BlockSpec((None, rb, cols), lambda i, f, lj=lj, blk=blk: (lj, blk(i, f), 0)))
        args.append(w)
        out_specs.append(pl.BlockSpec((cols // tn, rb, tn), lambda i, f, blk=blk: (0, blk(i, f), 0)))
        out_shape.append(jax.ShapeDtypeStruct((cols // tn, rows, tn), BF16))
    outs = pl.pallas_call(
        functools.partial(_ffn_tail_kernel, first=first, last=last, split_row=split_row,
                          n_casts=len(casts)),
        grid=(n_tiles - 1, nf),
        in_specs=in_specs,
        out_specs=out_specs,
        out_shape=out_shape,
        scratch_shapes=[pltpu.VMEM((tm, d), BF16)],
        input_output_aliases={6: 0},
        compiler_params=_params(("arbitrary", "arbitrary")),
        name=name + "_tail",
    )(*args)
    return outs[0], list(outs[1:])


def _a1_kernel(x_ref, g_ref, wa_ref, wg_ref, ba_ref, bg_ref, o_ref, h_ref):
    @pl.when(pl.program_id(1) == 0)
    def _():
        _rms_rows(x_ref, g_ref, h_ref)

    h = h_ref[...]
    a = _dot(h, wa_ref[...]) + ba_ref[...]
    gt = _dot(h, wg_ref[...]) + bg_ref[...]
    o_ref[...] = a * jax.nn.sigmoid(gt)


def _a1(x, g3, w1, b1, layer, j, *, row0, n_rows, tm):
    d = x.shape[1]
    nb, tn = w1.shape[0] // 2, w1.shape[2]
    d_a = nb * tn
    rb0 = row0 // tm
    return pl.pallas_call(
        _a1_kernel,
        grid=(n_rows // tm, nb),
        in_specs=[
            pl.BlockSpec((tm, d), lambda i, n: (rb0 + i, 0)),
            pl.BlockSpec((None, 1, d), lambda i, n: (layer, 0, 0)),
            pl.BlockSpec((None, d, tn), lambda i, n: (n, 0, 0)),
            pl.BlockSpec((None, d, tn), lambda i, n: (n + nb, 0, 0)),
            pl.BlockSpec((None, 1, tn), lambda i, n: (j, 0, n)),
            pl.BlockSpec((None, 1, tn), lambda i, n: (j, 0, n + nb)),
        ],
        out_specs=pl.BlockSpec((tm, tn), lambda i, n: (i, n)),
        out_shape=jax.ShapeDtypeStruct((n_rows, d_a), F32),
        scratch_shapes=[pltpu.VMEM((tm, d), BF16)],
        compiler_params=_params(("parallel", "arbitrary")),
        name=f"mix_a1s_{j}",
    )(x, g3, w1, w1, b1, b1)


def _dwconv_block(xp_ref, w_ref, r0, rows, cols, taps, halo):
    lead = halo - (taps - 1)
    assert halo % SUBLANES == 0 and lead >= 0
    acc = None
    for s in range(SUBLANES):
        group = [k for k in range(taps) if (k + lead) % SUBLANES == s]
        if not group:
            continue
        ext = rows + (SUBLANES if s else 0)
        part = None
        for k in group:
            base = r0 + ((k + lead) // SUBLANES) * SUBLANES
            term = xp_ref[pl.ds(base, ext), cols] * w_ref[k:k + 1, cols]
            part = term if part is None else part + term
        if s:
            part = part[s:s + rows]
        acc = part if acc is None else acc + part
    return acc


def _snake(i, k, nb):
    return jnp.where(i % 2 == 0, k, nb - 1 - k)


def _start_tile(x_ref, g_ref, h_ref, hal_ref, first_in_seq):
    _rms_rows(x_ref, g_ref, h_ref)

    @pl.when(first_in_seq)
    def _():
        hal_ref[...] = jnp.zeros(hal_ref.shape, hal_ref.dtype)


def _af_kernel(x_ref, xr_ref, g_ref, wa_ref, wg_ref, ba_ref, bgt_ref, wdw_ref, bdw_ref, lg_ref,
               lb_ref, w2_ref, o_ref, st_ref, h_ref, gbuf_ref, wbuf_ref, hal_ref, y_ref, act_ref,
               *, taps, tiles_per_seq, nb, rc):
    i, n = pl.program_id(0), pl.program_id(1)
    tm = x_ref.shape[0]
    tn = wa_ref.shape[1]
    halo = hal_ref.shape[0]
    cb = 128

    @pl.when(n == 0)
    def _():
        _start_tile(x_ref, g_ref, h_ref, hal_ref, i % tiles_per_seq == 0)

    @pl.when(n < nb)
    def _():
        col0 = pl.multiple_of(_snake(i, n, nb) * tn, tn)
        gbuf_ref[0:halo, :] = hal_ref[:, pl.ds(col0, tn)]
        wbuf_ref[0:taps, :] = wdw_ref[:, pl.ds(col0, tn)]
        wa, wg = wa_ref[...], wg_ref[...]
        ba, bgt = ba_ref[...], bgt_ref[...]
        for r in range(tm // rc):
            hc = h_ref[r * rc:(r + 1) * rc, :]
            glu = (_dot(hc, wa) + ba) * jax.nn.sigmoid(_dot(hc, wg) + bgt)
            gbuf_ref[halo + r * rc:halo + (r + 1) * rc, :] = glu
            for sub in range(rc // cb):
                r0 = r * rc + sub * cb
                for strip in range(tn // LANES):
                    cols = slice(strip * LANES, (strip + 1) * LANES)
                    y = _dwconv_block(gbuf_ref, wbuf_ref, r0, cb, cols, taps, halo)
                    y_ref[r0:r0 + cb, pl.ds(pl.multiple_of(col0 + strip * LANES, LANES), LANES)] = y
        hal_ref[:, pl.ds(col0, tn)] = gbuf_ref[tm:tm + halo, :]

    @pl.when(n == nb)
    def _():
        w2 = w2_ref[...]
        step, cw = 64, 512
        for r in range(tm // rc):
            for s in range(rc // step):
                blk = slice(r * rc + s * step, r * rc + (s + 1) * step)

                def load(c, blk=blk):
                    cols = slice(c * cw, (c + 1) * cw)
                    return y_ref[blk, cols] + bdw_ref[:, cols]

                def store(c, v, blk=blk):
                    act_ref[blk, c * cw:(c + 1) * cw] = _silu(v).astype(BF16)

                _layernorm_sweeps(load, store, y_ref.shape[1], lg_ref, lb_ref, cw)
            rows = slice(r * rc, (r + 1) * rc)
            o_ref[rows, :] = xr_ref[rows, :] + _dot(act_ref[rows, :], w2)

    @pl.when(n > nb)
    def _():
        o_ref[...] = xr_ref[...] + _dot(act_ref[...], w2_ref[...])

    @pl.when((n == 2 * nb - 1) & (i % tiles_per_seq == tiles_per_seq - 1))
    def _():
        st_ref[...] = hal_ref[halo - (taps - 1):halo, :]


def _a_fused(x, g3, w1, b1, w_dw, b_dw, ln_g, ln_b, w2, layer, j, *, n_prompt, seq, tm):
    m, d = x.shape
    taps = w_dw.shape[1]
    nb, tn = w2.shape[0], w2.shape[2]
    halo = 32
    assert w1.shape == (2 * nb, d, tn) and nb * tn == d and taps - 1 <= halo and seq % tm == 0
    kern = functools.partial(_af_kernel, taps=taps, tiles_per_seq=seq // tm, nb=nb, rc=256)
    taps_pad = -(-taps // SUBLANES) * SUBLANES
    blk1 = lambda i, n: _snake(i, jnp.minimum(n, nb - 1), nb)
    blk2 = lambda i, n: _snake(i, jnp.maximum(n - nb, 0), nb)
    row = lambda i, n: (j, 0, 0)
    return pl.pallas_call(
        kern,
        grid=(n_prompt // tm, 2 * nb),
        in_specs=[
            pl.BlockSpec((tm, d), _read_once_rows(n_prompt // tm)),
            pl.BlockSpec((tm, tn), lambda i, n: (i, blk2(i, n))),
            pl.BlockSpec((None, 1, d), lambda i, n: (layer, 0, 0)),
            pl.BlockSpec((None, d, tn), lambda i, n: (blk1(i, n), 0, 0)),
            pl.BlockSpec((None, d, tn), lambda i, n: (blk1(i, n) + nb, 0, 0)),
            pl.BlockSpec((None, 1, tn), lambda i, n: (j, 0, blk1(i, n))),
            pl.BlockSpec((None, 1, tn), lambda i, n: (j, 0, blk1(i, n) + nb)),
            pl.BlockSpec((None, taps, d), row),
            pl.BlockSpec((None, 1, d), row),
            pl.BlockSpec((None, 1, d), row),
            pl.BlockSpec((None, 1, d), row),
            pl.BlockSpec((None, d, tn), lambda i, n: (blk2(i, n), 0, 0)),
        ],
        out_specs=[
            pl.BlockSpec((tm, tn), lambda i, n: (i, blk2(i, n))),
            pl.BlockSpec((None, taps - 1, d), lambda i, n: (i // (seq // tm), 0, 0)),
        ],
        out_shape=[
            jax.ShapeDtypeStruct((m, d), F32),
            jax.ShapeDtypeStruct((n_prompt // seq, taps - 1, d), F32),
        ],
        scratch_shapes=[
            pltpu.VMEM((tm, d), BF16),
            pltpu.VMEM((halo + tm, tn), F32),
            pltpu.VMEM((taps_pad, tn), F32),
            pltpu.VMEM((halo, d), F32),
            pltpu.VMEM((tm, d), F32),
            pltpu.VMEM((tm, d), BF16),
        ],
        compiler_params=_params(("arbitrary", "arbitrary")),
        name=f"mix_a_{j}",
    )(x, x, g3, w1, w1, b1, b1, w_dw, b_dw, ln_g, ln_b, w2)


def _a_sconv_kernel(*refs):
    st_ref, glu_ref, w_ref, b_ref = refs[:4]
    y_ref, ns_ref = refs[-2:]
    past = st_ref.shape[0]
    acc = st_ref[0] * w_ref[0:1, :]
    for k in range(1, past):
        acc = acc + st_ref[k] * w_ref[k:k + 1, :]
        ns_ref[k - 1] = st_ref[k]
    g = glu_ref[...]
    ns_ref[past - 1] = g
    y_ref[...] = acc + g * w_ref[past:past + 1, :] + b_ref[...]


def _a_sample_conv(state_t, glu, w_dw, b_dw, prev, j, *, tn):
    n_layers, past, nb, d = state_t.shape
    taps = w_dw.shape[1]
    assert taps == past + 1 and glu.shape == (nb, d)
    in_specs = [
        pl.BlockSpec((None, past, nb, tn), lambda n: (j, 0, 0, n)),
        pl.BlockSpec((nb, tn), lambda n: (0, n)),
        pl.BlockSpec((None, taps, tn), lambda n: (j, 0, n)),
        pl.BlockSpec((None, 1, tn), lambda n: (j, 0, n)),
    ]
    args = [state_t, glu, w_dw, b_dw]
    aliases = {}
    if prev is not None:
        in_specs.append(pl.BlockSpec(memory_space=pl.ANY))
        args.append(prev)
        aliases = {4: 1}
    return pl.pallas_call(
        _a_sconv_kernel,
        grid=(d // tn,),
        in_specs=in_specs,
        out_specs=[
            pl.BlockSpec((nb, tn), lambda n: (0, n)),
            pl.BlockSpec((None, past, nb, tn), lambda n: (j, 0, 0, n)),
        ],
        out_shape=[
            jax.ShapeDtypeStruct((nb, d), F32),
            jax.ShapeDtypeStruct((n_layers, past, nb, d), F32),
        ],
        input_output_aliases=aliases,
        compiler_params=_params(("parallel",)),
        name=f"mix_a_sconv_{j}",
    )(*args)


def _a2s_kernel(x_ref, y_ref, lg_ref, lb_ref, w2_ref, _, o_ref, act_ref):
    @pl.when(pl.program_id(0) == 0)
    def _():
        act_ref[...] = _silu(_layernorm(y_ref[...], lg_ref[...], lb_ref[...])).astype(BF16)

    o_ref[...] = x_ref[...] + _dot(act_ref[...], w2_ref[...])


def _a2_sample(x, x_new, y, ln_g, ln_b, w2, j, *, n_prompt):
    m, d = x.shape
    ns = y.shape[0]
    tn = w2.shape[2]
    rb = n_prompt // ns
    return pl.pallas_call(
        _a2s_kernel,
        grid=(d // tn,),
        in_specs=[
            pl.BlockSpec((ns, tn), lambda n: (rb, n)),
            pl.BlockSpec((ns, d), lambda n: (0, 0)),
            pl.BlockSpec((None, 1, d), lambda n: (j, 0, 0)),
            pl.BlockSpec((None, 1, d), lambda n: (j, 0, 0)),
            pl.BlockSpec((None, d, tn), lambda n: (n, 0, 0)),
            pl.BlockSpec(memory_space=pl.ANY),
        ],
        out_specs=pl.BlockSpec((ns, tn), lambda n: (rb, n)),
        out_shape=jax.ShapeDtypeStruct((m, d), F32),
        scratch_shapes=[pltpu.VMEM((ns, d), BF16)],
        input_output_aliases={5: 0},
        compiler_params=_params(("arbitrary",)),
        name=f"mix_a2s_{j}",
    )(x, y, ln_g, ln_b, w2, x_new)


def _b1_kernel(x_ref, g_ref, w_ref, b_ref, lg_ref, lb_ref, o_ref, h_ref):
    n = pl.program_id(1)
    per_step, _, tn = w_ref.shape

    @pl.when(n == 0)
    def _():
        _rms_rows(x_ref, g_ref, h_ref)

    h = h_ref[...]
    for t in range(per_step):
        z = _dot(h, w_ref[t]) + b_ref[:, t * tn:(t + 1) * tn]
        o_ref[:, pl.ds(pl.multiple_of((n * per_step + t) * tn, tn), tn)] = _gelu(z)

    @pl.when(n == pl.num_programs(1) - 1)
    def _():
        rows, step, cw = o_ref.shape[0], 64, 512

        def body(c, carry):
            blk = pl.ds(pl.multiple_of(c * step, step), step)

            def load(k):
                return o_ref[blk, k * cw:(k + 1) * cw]

            def store(k, v):
                o_ref[blk, k * cw:(k + 1) * cw] = v

            _layernorm_sweeps(load, store, o_ref.shape[1], lg_ref, lb_ref, cw)
            return carry

        lax.fori_loop(0, rows // step, body, 0)


def _b1(x, g3, w_in, b_in, ln_g, ln_b, layer, j, *, tm):
    m, d = x.shape
    nb, tn = w_in.shape[0] // 2, w_in.shape[2]
    d_sgu = nb * tn
    per_step = 2
    steps = nb // per_step
    assert nb % per_step == 0
    return pl.pallas_call(
        _b1_kernel,
        grid=(m // tm, steps),
        in_specs=[
            pl.BlockSpec((tm, d), lambda i, n: (i, 0)),
            pl.BlockSpec((None, 1, d), lambda i, n: (layer, 0, 0)),
            pl.BlockSpec((per_step, d, tn), lambda i, n: (n + steps, 0, 0)),
            pl.BlockSpec((None, 1, per_step * tn), lambda i, n: (j, 0, n + steps)),
            pl.BlockSpec((None, 1, d_sgu), lambda i, n: (j, 0, 0)),
            pl.BlockSpec((None, 1, d_sgu), lambda i, n: (j, 0, 0)),
        ],
        out_specs=pl.BlockSpec((tm, d_sgu), lambda i, n: (i, 0)),
        out_shape=jax.ShapeDtypeStruct((m, d_sgu), F32),
        scratch_shapes=[pltpu.VMEM((tm, d), BF16)],
        compiler_params=_params(("parallel", "arbitrary")),
        name=f"mix_b1_{j}",
    )(x, g3, w_in, b_in, ln_g, ln_b)


def _b2_kernel(x_ref, g_ref, v_ref, wi_ref, bi_ref, ws_ref, bs_ref, wo_ref, o_ref, h_ref,
               *, n_prompt_chunks):
    i = pl.program_id(0)
    tm = x_ref.shape[0]

    @pl.when(pl.program_id(1) == 0)
    def _():
        _rms_rows(x_ref, g_ref, h_ref, copy_ref=o_ref)

    u = _gelu(_dot(h_ref[...], wi_ref[...]) + bi_ref[...])
    ws = ws_ref[...]
    row = lax.broadcasted_iota(jnp.int32, ws.shape, 0)
    col = lax.broadcasted_iota(jnp.int32, ws.shape, 1)
    w_tril = jnp.where(col <= row, ws, 0.0).astype(BF16)
    bs = bs_ref[...]
    parts = []
    for c in range(tm // CHUNK):
        rows = slice(c * CHUNK, (c + 1) * CHUNK)
        v = v_ref[rows, :]
        s_prompt = _dot(w_tril, v.astype(BF16)) + bs
        s_sample = ws[0:1, 0:1] * v + bs[0:1, :]
        is_sample = i * (tm // CHUNK) + c >= n_prompt_chunks
        parts.append(u[rows, :] * jnp.where(is_sample, s_sample, s_prompt))
    y = jnp.concatenate(parts, axis=0).astype(BF16)
    o_ref[...] += _dot(y, wo_ref[...])


def _b2(x, g3, vn, w_in, b_in, w_s, b_s3, w_out, layer, j, *, n_prompt, tm):
    m, d = x.shape
    d_sgu = vn.shape[1]
    gw = d_sgu // N_SGU_GROUPS
    assert w_in.shape == (2 * N_SGU_GROUPS, d, gw) and w_out.shape == (1, d_sgu, d)
    kern = functools.partial(_b2_kernel, n_prompt_chunks=n_prompt // CHUNK)
    return pl.pallas_call(
        kern,
        grid=(m // tm, N_SGU_GROUPS),
        in_specs=[
            pl.BlockSpec((tm, d), lambda i, g: (i, 0)),
            pl.BlockSpec((None, 1, d), lambda i, g: (layer, 0, 0)),
            pl.BlockSpec((tm, gw), lambda i, g: (i, g)),
            pl.BlockSpec((None, d, gw), lambda i, g: (g, 0, 0)),
            pl.BlockSpec((None, 1, gw), lambda i, g: (j, 0, g)),
            pl.BlockSpec((None, None, CHUNK, CHUNK), lambda i, g: (j, g, 0, 0)),
            pl.BlockSpec((None, None, CHUNK, 1), lambda i, g: (j, g, 0, 0)),
            pl.BlockSpec((None, gw, d), lambda i, g: (0, g, 0)),
        ],
        out_specs=pl.BlockSpec((tm, d), lambda i, g: (i, 0)),
        out_shape=jax.ShapeDtypeStruct((m, d), F32),
        scratch_shapes=[pltpu.VMEM((tm, d), BF16)],
        compiler_params=_params(("parallel", "arbitrary")),
        name=f"mix_b2_{j}",
    )(x, g3, vn, w_in, b_in, w_s, b_s3, w_out)


def _c1_kernel(x_ref, g_ref, wb_ref, wc_ref, wx_ref, bg_ref, cx_ref, h_ref):
    @pl.when(pl.program_id(1) == 0)
    def _():
        _rms_rows(x_ref, g_ref, h_ref)

    h = h_ref[...]
    bg_ref[...] = _dot(h, wb_ref[...])
    cx_ref[...] = _dot(h, wc_ref[...]) * _dot(h, wx_ref[...])


def _c1(x, g3, w_in, layer, j, *, row0, n_rows, tm):
    d = x.shape[1]
    nb, tn = w_in.shape[0] // 3, w_in.shape[2]
    d_c = nb * tn
    rb0 = row0 // tm
    out = jax.ShapeDtypeStruct((n_rows, d_c), F32)
    return pl.pallas_call(
        _c1_kernel,
        grid=(n_rows // tm, nb),
        in_specs=[
            pl.BlockSpec((tm, d), lambda i, n: (rb0 + i, 0)),
            pl.BlockSpec((None, 1, d), lambda i, n: (layer, 0, 0)),
            pl.BlockSpec((None, d, tn), lambda i, n: (n, 0, 0)),
            pl.BlockSpec((None, d, tn), lambda i, n: (n + nb, 0, 0)),
            pl.BlockSpec((None, d, tn), lambda i, n: (n + 2 * nb, 0, 0)),
        ],
        out_specs=[pl.BlockSpec((tm, tn), lambda i, n: (i, n))] * 2,
        out_shape=[out, out],
        scratch_shapes=[pltpu.VMEM((tm, d), BF16)],
        compiler_params=_params(("parallel", "arbitrary")),
        name=f"mix_c1s_{j}",
    )(x, g3, w_in, w_in, w_in)


def _cf_kernel(x_ref, xr_ref, g_ref, wb_ref, wc_ref, wx_ref, wcv_ref, wo_ref, o_ref, st_ref,
               h_ref, gbuf_ref, wbuf_ref, hal_ref, act_ref, *, taps, tiles_per_seq, nb, rc):
    i, n = pl.program_id(0), pl.program_id(1)
    tm = x_ref.shape[0]
    tn = wb_ref.shape[1]
    halo = hal_ref.shape[0]
    cb = 128

    @pl.when(n == 0)
    def _():
        _start_tile(x_ref, g_ref, h_ref, hal_ref, i % tiles_per_seq == 0)

    @pl.when(n < nb)
    def _():
        col0 = pl.multiple_of(_snake(i, n, nb) * tn, tn)
        gbuf_ref[0:halo, :] = hal_ref[:, pl.ds(col0, tn)]
        wbuf_ref[0:taps, :] = wcv_ref[:, pl.ds(col0, tn)]
        wb, wc, wx = wb_ref[...], wc_ref[...], wx_ref[...]
        for r in range(tm // rc):
            hc = h_ref[r * rc:(r + 1) * rc, :]
            bg = _dot(hc, wb)
            gbuf_ref[halo + r * rc:halo + (r + 1) * rc, :] = _dot(hc, wc) * _dot(hc, wx)
            for sub in range(rc // cb):
                r0 = r * rc + sub * cb
                for strip in range(tn // LANES):
                    cols = slice(strip * LANES, (strip + 1) * LANES)
                    y = _dwconv_block(gbuf_ref, wbuf_ref, r0, cb, cols, taps, halo)
                    gate = bg[sub * cb:(sub + 1) * cb, cols]
                    act_ref[r0:r0 + cb, pl.ds(pl.multiple_of(col0 + strip * LANES, LANES), LANES)] = (
                        gate * y).astype(BF16)
        hal_ref[:, pl.ds(col0, tn)] = gbuf_ref[tm:tm + halo, :]

    @pl.when(n >= nb)
    def _():
        o_ref[...] = xr_ref[...] + _dot(act_ref[...], wo_ref[...])

    @pl.when((n == 2 * nb - 1) & (i % tiles_per_seq == tiles_per_seq - 1))
    def _():
        st_ref[...] = hal_ref[halo - (taps - 1):halo, :]


def _c_fused(x, g3, w_in, w_conv, w_out, layer, j, *, n_prompt, seq, tm):
    m, d = x.shape
    taps = w_conv.shape[1]
    nb, tn = w_out.shape[0], w_out.shape[2]
    halo = SUBLANES
    assert w_in.shape == (3 * nb, d, tn) and nb * tn == d and taps - 1 <= halo and seq % tm == 0
    kern = functools.partial(_cf_kernel, taps=taps, tiles_per_seq=seq // tm, nb=nb, rc=256)
    taps_pad = -(-taps // SUBLANES) * SUBLANES
    blk1 = lambda i, n: _snake(i, jnp.minimum(n, nb - 1), nb)
    blk2 = lambda i, n: _snake(i, jnp.maximum(n - nb, 0), nb)
    blk = lambda part: (lambda i, n: (blk1(i, n) + part * nb, 0, 0))
    return pl.pallas_call(
        kern,
        grid=(n_prompt // tm, 2 * nb),
        in_specs=[
            pl.BlockSpec((tm, d), _read_once_rows(n_prompt // tm)),
            pl.BlockSpec((tm, tn), lambda i, n: (i, blk2(i, n))),
            pl.BlockSpec((None, 1, d), lambda i, n: (layer, 0, 0)),
            pl.BlockSpec((None, d, tn), blk(0)),
            pl.BlockSpec((None, d, tn), blk(1)),
            pl.BlockSpec((None, d, tn), blk(2)),
            pl.BlockSpec((None, taps, d), lambda i, n: (j, 0, 0)),
            pl.BlockSpec((None, d, tn), lambda i, n: (blk2(i, n), 0, 0)),
        ],
        out_specs=[
            pl.BlockSpec((tm, tn), lambda i, n: (i, blk2(i, n))),
            pl.BlockSpec((None, taps - 1, d), lambda i, n: (i // (seq // tm), 0, 0)),
        ],
        out_shape=[
            jax.ShapeDtypeStruct((m, d), F32),
            jax.ShapeDtypeStruct((n_prompt // seq, taps - 1, d), F32),
        ],
        scratch_shapes=[
            pltpu.VMEM((tm, d), BF16),
            pltpu.VMEM((halo + tm, tn), F32),
            pltpu.VMEM((taps_pad, tn), F32),
            pltpu.VMEM((halo, d), F32),
            pltpu.VMEM((tm, d), BF16),
        ],
        compiler_params=_params(("arbitrary", "arbitrary")),
        name=f"mix_c_{j}",
    )(x, x, g3, w_in, w_in, w_in, w_conv, w_out)


def _c2s_kernel(x_ref, bg_ref, cx_ref, s0_ref, s1_ref, w_ref, wo_ref, _, o_ref, act_ref):
    @pl.when(pl.program_id(0) == 0)
    def _():
        w = w_ref[...]
        y = s0_ref[...] * w[0:1] + s1_ref[...] * w[1:2] + cx_ref[...] * w[2:3]
        act_ref[...] = (bg_ref[...] * y).astype(BF16)

    o_ref[...] = x_ref[...] + _dot(act_ref[...], wo_ref[...])


def _c2_sample(x, x_new, bg, cx, s0, s1, w_conv, w_out, j, *, n_prompt):
    m, d = x.shape
    ns = s0.shape[0]
    tn = w_out.shape[2]
    rb = n_prompt // ns
    taps = w_conv.shape[1]
    assert taps == 3
    return pl.pallas_call(
        _c2s_kernel,
        grid=(d // tn,),
        in_specs=[
            pl.BlockSpec((ns, tn), lambda n: (rb, n)),
            pl.BlockSpec((ns, d), lambda n: (0, 0)),
            pl.BlockSpec((ns, d), lambda n: (0, 0)),
            pl.BlockSpec((ns, d), lambda n: (0, 0)),
            pl.BlockSpec((ns, d), lambda n: (0, 0)),
            pl.BlockSpec((None, taps, d), lambda n: (j, 0, 0)),
            pl.BlockSpec((None, d, tn), lambda n: (n, 0, 0)),
            pl.BlockSpec(memory_space=pl.ANY),
        ],
        out_specs=pl.BlockSpec((ns, tn), lambda n: (rb, n)),
        out_shape=jax.ShapeDtypeStruct((m, d), F32),
        scratch_shapes=[pltpu.VMEM((ns, d), BF16)],
        input_output_aliases={7: 0},
        compiler_params=_params(("arbitrary",)),
        name=f"mix_c2s_{j}",
    )(x, bg, cx, s0, s1, w_conv, w_out, x_new)


def _last_rows(a, batch, seq, rows):
    return jnp.stack([a[(b + 1) * seq - rows:(b + 1) * seq] for b in range(batch)])


def kernel(x_prompt, x_sample, state_conv_a, state_conv_c, g_ffn1, g_mix, g_ffn2, g_final,
           w_ffn_gate, w_ffn_up, w_ffn_down,
           a_w_pw1, a_b_pw1, a_w_dw, a_b_dw, a_ln_g, a_ln_b, a_w_pw2,
           b_w_in, b_b_in, b_ln_g, b_ln_b, b_w_s, b_b_s, b_w_out,
           c_w_in, c_w_conv, c_w_out):
    batch, seq, d = x_prompt.shape
    n_sample = x_sample.shape[0]
    assert x_sample.shape[1] == 1
    n_prompt = batch * seq
    depth = g_ffn1.shape[0]
    past_c = state_conv_c.shape[2]
    assert past_c == 2
    d_sgu = b_ln_g.shape[-1]

    row3 = lambda a: a.reshape(a.shape[0], 1, a.shape[1])
    g1, gm, g2 = row3(g_ffn1), row3(g_mix), row3(g_ffn2)
    gf = g_final.reshape(1, d)
    a_b1, a_bd, a_lg, a_lb = row3(a_b_pw1), row3(a_b_dw), row3(a_ln_g), row3(a_ln_b)
    b_bi, b_lg, b_lb = row3(b_b_in), row3(b_ln_g), row3(b_ln_b)
    b_bs = b_b_s.reshape(*b_b_s.shape, 1)
    state_a_t = jnp.transpose(state_conv_a, (0, 2, 1, 3))

    ffn = functools.partial(_ffn, n_prompt=n_prompt, n_sample=n_sample,
                            tm=832, tf_head=512, tf_tail=512)
    new_a_p, new_b_p, new_b_s, new_c_p, new_c_s = [], [], [], [], []
    new_a_s_t = None

    x = x_prompt.reshape(n_prompt, d)
    xs = x_sample.reshape(n_sample, d)
    gw = d_sgu // N_SGU_GROUPS
    for i in range(depth):
        kind, j = i % 3, i // 3
        if kind == 0:
            casts = [(a_w_pw1, j, 32, 512), (a_w_pw2, j, 32, 512)]
        elif kind == 1:
            casts = [(b_w_in, j, 32, gw), (b_w_out, j, 64, d)]
        else:
            casts = [(c_w_in, j, 32, 512), (c_w_out, j, 32, 512)]
        x, (w_a16, w_b16) = ffn(x, xs, g1, gf, w_ffn_gate, w_ffn_up, w_ffn_down, i, 0, casts,
                                first=(i == 0), last=False)
        if kind == 0:
            glu_s = _a1(x, gm, w_a16, a_b1, i, j, row0=n_prompt, n_rows=n_sample, tm=n_sample)
            y_s, new_a_s_t = _a_sample_conv(state_a_t, glu_s, a_w_dw, a_bd, new_a_s_t, j, tn=256)
            x_new, st_p = _a_fused(x, gm, w_a16, a_b1, a_w_dw, a_bd, a_lg, a_lb, w_b16, i, j,
                                   n_prompt=n_prompt, seq=seq, tm=1024)
            x = _a2_sample(x, x_new, y_s, a_lg, a_lb, w_b16, j, n_prompt=n_prompt)
            new_a_p.append(st_p)
        elif kind == 1:
            vn = _b1(x, gm, w_a16, b_bi, b_lg, b_lb, i, j, tm=832)
            x = _b2(x, gm, vn, w_a16, b_bi, b_w_s, b_bs, w_b16, i, j, n_prompt=n_prompt, tm=640)
            new_b_p.append(_last_rows(vn, batch, seq, CHUNK))
            new_b_s.append(vn[n_prompt:].reshape(n_sample, 1, d_sgu))
        else:
            bg_s, cx_s = _c1(x, gm, w_a16, i, j, row0=n_prompt, n_rows=n_sample, tm=n_sample)
            s0, s1 = state_conv_c[j, :, 0], state_conv_c[j, :, 1]
            x_new, st_p = _c_fused(x, gm, w_a16, c_w_conv, w_b16, i, j,
                                   n_prompt=n_prompt, seq=seq, tm=1024)
            x = _c2_sample(x, x_new, bg_s, cx_s, s0, s1, c_w_conv, w_b16, j, n_prompt=n_prompt)
            new_c_p.append(st_p)
            new_c_s.append(jnp.stack([s1, cx_s], axis=1))
        last = i == depth - 1
        x, _ = ffn(x, xs, g2, gf, w_ffn_gate, w_ffn_up, w_ffn_down, i, 1, [],
                   first=False, last=last)

    return (x[:n_prompt].reshape(batch, seq, d), x[n_prompt:].reshape(n_sample, 1, d),
            jnp.stack(new_a_p), jnp.transpose(new_a_s_t, (0, 2, 1, 3)),
            jnp.stack(new_b_p), jnp.stack(new_b_s),
            jnp.stack(new_c_p), jnp.stack(new_c_s))
```

```python
import functools

import jax
import jax.numpy as jnp
from jax import lax
from jax.experimental import pallas as pl
from jax.experimental.pallas import tpu as pltpu

F32 = jnp.float32
BF16 = jnp.bfloat16

EPS = 1e-6
FFN_HALF = 0.5
CHUNK = 128
N_SGU_GROUPS = 8
INV_SQRT2 = 0.7071067811865476
SUBLANES = 8
LANES = 128

V7X_VMEM_BYTES = 64 * 1024 * 1024
VMEM_LIMIT = V7X_VMEM_BYTES - 6 * 1024 * 1024


def _params(sem):
    return pltpu.CompilerParams(dimension_semantics=sem, vmem_limit_bytes=VMEM_LIMIT)


def _read_once_rows(n_tiles):
    return lambda i, s: (jnp.minimum(i + jnp.minimum(s, 1), n_tiles - 1), 0)


def _layernorm(y, g, b):
    mu = jnp.mean(y, axis=-1, keepdims=True)
    yc = y - mu
    var = jnp.mean(yc * yc, axis=-1, keepdims=True)
    return (yc * lax.rsqrt(var + EPS)) * g + b


def _lane_sum(v):
    acc = v[:, 0:LANES]
    for t in range(1, v.shape[1] // LANES):
        acc = acc + v[:, t * LANES:(t + 1) * LANES]
    return acc


def _layernorm_sweeps(load, store, width, g_ref, b_ref, cw=512):
    lane_sum = _lane_sum
    n = width // cw
    acc = lane_sum(load(0))
    for c in range(1, n):
        acc = acc + lane_sum(load(c))
    mu = jnp.sum(acc, axis=-1, keepdims=True) * (1.0 / width)
    acc = None
    for c in range(n):
        dev = load(c) - mu
        sq = lane_sum(dev * dev)
        acc = sq if acc is None else acc + sq
    var = jnp.sum(acc, axis=-1, keepdims=True) * (1.0 / width)
    scale = lax.rsqrt(var + EPS)
    for c in range(n):
        cols = slice(c * cw, (c + 1) * cw)
        store(c, ((load(c) - mu) * scale) * g_ref[:, cols] + b_ref[:, cols])


def _silu(x):
    return x * jax.nn.sigmoid(x)


def _gelu(x):
    return 0.5 * x * (1.0 + lax.erf(x * INV_SQRT2))


def _dot(a, w):
    return jnp.dot(a, w, preferred_element_type=F32)


def _rms_rows(x_ref, g_ref, h_ref, copy_ref=None, cw=512):
    rows, width = x_ref.shape
    step = rows // 4
    assert rows % 4 == 0 and step % 16 == 0 and width % cw == 0

    def body(c, carry):
        blk = pl.ds(pl.multiple_of(c * step, step), step)
        acc = None
        for k in range(width // cw):
            x = x_ref[blk, k * cw:(k + 1) * cw]
            sq = _lane_sum(x * x)
            acc = sq if acc is None else acc + sq
        scale = lax.rsqrt(jnp.sum(acc, axis=-1, keepdims=True) * (1.0 / width) + EPS)
        for k in range(width // cw):
            cols = slice(k * cw, (k + 1) * cw)
            x = x_ref[blk, cols]
            h_ref[blk, cols] = ((x * scale) * g_ref[:, cols]).astype(BF16)
            if copy_ref is not None:
                copy_ref[blk, cols] = x
        return carry

    lax.fori_loop(0, 4, body, 0)


def _rms_rows_inplace(o_ref, g_ref, cw=512):
    rows, width = o_ref.shape
    step = rows // 4
    assert rows % 4 == 0 and step % SUBLANES == 0 and width % cw == 0

    def body(c, carry):
        blk = pl.ds(pl.multiple_of(c * step, step), step)
        acc = None
        for k in range(width // cw):
            x = o_ref[blk, k * cw:(k + 1) * cw]
            sq = _lane_sum(x * x)
            acc = sq if acc is None else acc + sq
        scale = lax.rsqrt(jnp.sum(acc, axis=-1, keepdims=True) * (1.0 / width) + EPS)
        for k in range(width // cw):
            cols = slice(k * cw, (k + 1) * cw)
            o_ref[blk, cols] = (o_ref[blk, cols] * scale) * g_ref[:, cols]
        return carry

    lax.fori_loop(0, 4, body, 0)


def _ffn_rows(h, wg, wu, wd):
    gate = _dot(h, wg)
    up = _dot(h, wu)
    act = (_silu(gate) * (up * FFN_HALF)).astype(BF16)
    return _dot(act, wd)


def _ffn_step(h_ref, wg, wu, wd, o_ref):
    o_ref[...] += _ffn_rows(h_ref[...], wg, wu, wd)


def _ffn_first_step(x_ref, g_ref, h_ref, wg, wu, wd, o_ref, tail, rc):
    tm, width = x_ref.shape
    cw = 512

    def norm_rows(a, b):
        def load(k):
            cols = slice(k * cw, (k + 1) * cw)
            x = x_ref[a:b, cols]
            if tail is not None and a >= tail[1]:
                use_tail, split_row, tail_ref = tail
                x = jnp.where(use_tail, tail_ref[a - split_row:b - split_row, cols], x)
            return x

        acc = None
        for k in range(width // cw):
            x = load(k)
            sq = _lane_sum(x * x)
            acc = sq if acc is None else acc + sq
        scale = lax.rsqrt(jnp.sum(acc, axis=-1, keepdims=True) * (1.0 / width) + EPS)
        for k in range(width // cw):
            cols = slice(k * cw, (k + 1) * cw)
            x = load(k)
            h_ref[a:b, cols] = ((x * scale) * g_ref[:, cols]).astype(BF16)
            o_ref[a:b, cols] = x

    for r in range(tm // rc):
        a, b = r * rc, (r + 1) * rc
        if tail is not None and a < tail[1] < b:
            norm_rows(a, tail[1])
            norm_rows(tail[1], b)
        else:
            norm_rows(a, b)
        o_ref[a:b, :] += _ffn_rows(h_ref[a:b, :], wg, wu, wd)


def _ffn_head_kernel(x_ref, g_ref, gf_ref, wg_ref, wu_ref, wd_ref,
                     o_ref, wg16_ref, wu16_ref, wd16_ref, h_ref, *, final_norm):
    f = pl.program_id(0)

    @pl.when(f == 0)
    def _():
        _rms_rows(x_ref, g_ref, h_ref, copy_ref=o_ref)

    wg = wg_ref[...].astype(BF16)
    wu = wu_ref[...].astype(BF16)
    wd = wd_ref[...].astype(BF16)
    wg16_ref[...] = wg
    wu16_ref[...] = wu
    wd16_ref[...] = wd
    _ffn_step(h_ref, wg, wu, wd, o_ref)

    if final_norm:
        @pl.when(f == pl.num_programs(0) - 1)
        def _():
            _rms_rows_inplace(o_ref, gf_ref)


def _ffn_tail_kernel(*refs, first, last, split_row, n_casts):
    refs = list(refs)
    x_ref, g_ref, gf_ref, wg_ref, wu_ref, wd_ref, _ = refs[:7]
    xs_ref = refs[7] if first else None
    n_in = 8 if first else 7
    cast_src = refs[n_in:n_in + n_casts]
    o_ref = refs[n_in + n_casts]
    cast_dst = refs[n_in + n_casts + 1:n_in + 2 * n_casts + 1]
    h_ref = refs[-1]
    i, f = pl.program_id(0), pl.program_id(1)
    last_tile = i == pl.num_programs(0) - 1

    for src, dst in zip(cast_src, cast_dst):
        tn = dst.shape[2]
        for c in range(dst.shape[0]):
            dst[c] = src[:, c * tn:(c + 1) * tn].astype(BF16)

    @pl.when(f == 0)
    def _():
        tail = (last_tile, split_row, xs_ref) if first else None
        _ffn_first_step(x_ref, g_ref, h_ref, wg_ref[...], wu_ref[...], wd_ref[...], o_ref,
                        tail, rc=x_ref.shape[0] // 4)

    @pl.when(f > 0)
    def _():
        _ffn_step(h_ref, wg_ref[...], wu_ref[...], wd_ref[...], o_ref)

    if last:
        @pl.when(f == pl.num_programs(1) - 1)
        def _():
            _rms_rows_inplace(o_ref, gf_ref)


def _ffn(x, xs, g3, gf2, w_gate, w_up, w_down, layer, which, casts, *, n_prompt, n_sample,
         first, last, tm, tf_head, tf_tail):
    d = x.shape[1]
    dff = w_gate.shape[-1]
    m = n_prompt + n_sample
    assert m % tm == 0
    n_tiles = m // tm
    split_row = n_prompt - (n_tiles - 1) * tm
    assert 0 < split_row < tm and split_row % 64 == 0 and tm - split_row == n_sample
    out_rows = m
    name = f"ffn_{layer}_{which}"

    once = pl.Buffered(1)
    nf_head = dff // tf_head
    assert tf_head == tf_tail
    head_out, wg16, wu16, wd16 = pl.pallas_call(
        functools.partial(_ffn_head_kernel, final_norm=last),
        grid=(nf_head,),
        in_specs=[
            pl.BlockSpec((tm, d), lambda f: (0, 0), pipeline_mode=once),
            pl.BlockSpec((None, 1, d), lambda f: (layer, 0, 0)),
            pl.BlockSpec((1, d), lambda f: (0, 0)),
            pl.BlockSpec((None, None, d, tf_head), lambda f: (layer, which, 0, f)),
            pl.BlockSpec((None, None, d, tf_head), lambda f: (layer, which, 0, f)),
            pl.BlockSpec((None, None, tf_head, d), lambda f: (layer, which, f, 0)),
        ],
        out_specs=[
            pl.BlockSpec((tm, d), lambda f: (0, 0), pipeline_mode=once),
            pl.BlockSpec((None, d, tf_head), lambda f: (f, 0, 0)),
            pl.BlockSpec((None, d, tf_head), lambda f: (f, 0, 0)),
            pl.BlockSpec((None, tf_head, d), lambda f: (f, 0, 0)),
        ],
        out_shape=[
            jax.ShapeDtypeStruct((out_rows, d), F32),
            jax.ShapeDtypeStruct((nf_head, d, tf_head), BF16),
            jax.ShapeDtypeStruct((nf_head, d, tf_head), BF16),
            jax.ShapeDtypeStruct((nf_head, tf_head, d), BF16),
        ],
        scratch_shapes=[pltpu.VMEM((tm, d), BF16)],
        compiler_params=_params(("arbitrary",)),
        name=name + "_head",
    )(x, g3, gf2, w_gate, w_up, w_down)

    in_specs = [
        pl.BlockSpec((tm, d), lambda i, f: (i + 1, 0)),
        pl.BlockSpec((None, 1, d), lambda i, f: (layer, 0, 0)),
        pl.BlockSpec((1, d), lambda i, f: (0, 0)),
        pl.BlockSpec((None, d, tf_tail), lambda i, f: (f, 0, 0)),
        pl.BlockSpec((None, d, tf_tail), lambda i, f: (f, 0, 0)),
        pl.BlockSpec((None, tf_tail, d), lambda i, f: (f, 0, 0)),
        pl.BlockSpec(memory_space=pl.ANY),
    ]
    args = [x, g3, gf2, wg16, wu16, wd16, head_out]
    if first:
        in_specs.append(pl.BlockSpec((n_sample, d), lambda i, f: (0, 0)))
        args.append(xs)
    nf = dff // tf_tail
    out_specs = [pl.BlockSpec((tm, d), lambda i, f: (i + 1, 0))]
    out_shape = [jax.ShapeDtypeStruct((out_rows, d), F32)]
    for w, lj, rb, tn in casts:
        _, rows, cols = w.shape
        n_blk = rows // rb
        assert rows % rb == 0 and cols % tn == 0 and n_blk <= (n_tiles - 1) * nf
        blk = lambda i, f, n_blk=n_blk: jnp.minimum(i * nf + f, n_blk - 1)
        in_specs.append(pl.BlockSpec((None, rb, cols), lambda i, f, lj=lj, blk=blk: (lj, blk(i, f), 0)))
        args.append(w)
        out_specs.append(pl.BlockSpec((cols // tn, rb, tn), lambda i, f, blk=blk: (0, blk(i, f), 0)))
        out_shape.append(jax.ShapeDtypeStruct((cols // tn, rows, tn), BF16))
    outs = pl.pallas_call(
        functools.partial(_ffn_tail_kernel, first=first, last=last, split_row=split_row,
                          n_casts=len(casts)),
        grid=(n_tiles - 1, nf),
        in_specs=in_specs,
        out_specs=out_specs,
        out_shape=out_shape,
        scratch_shapes=[pltpu.VMEM((tm, d), BF16)],
        input_output_aliases={6: 0},
        compiler_params=_params(("arbitrary", "arbitrary")),
        name=name + "_tail",
    )(*args)
    return outs[0], list(outs[1:])


def _a1_kernel(x_ref, g_ref, wa_ref, wg_ref, ba_ref, bg_ref, o_ref, h_ref):
    @pl.when(pl.program_id(1) == 0)
    def _():
        _rms_rows(x_ref, g_ref, h_ref)

    h = h_ref[...]
    a = _dot(h, wa_ref[...]) + ba_ref[...]
    gt = _dot(h, wg_ref[...]) + bg_ref[...]
    o_ref[...] = a * jax.nn.sigmoid(gt)


def _a1(x, g3, w1, b1, layer, j, *, row0, n_rows, tm):
    d = x.shape[1]
    nb, tn = w1.shape[0] // 2, w1.shape[2]
    d_a = nb * tn
    rb0 = row0 // tm
    return pl.pallas_call(
        _a1_kernel,
        grid=(n_rows // tm, nb),
        in_specs=[
            pl.BlockSpec((tm, d), lambda i, n: (rb0 + i, 0)),
            pl.BlockSpec((None, 1, d), lambda i, n: (layer, 0, 0)),
            pl.BlockSpec((None, d, tn), lambda i, n: (n, 0, 0)),
            pl.BlockSpec((None, d, tn), lambda i, n: (n + nb, 0, 0)),
            pl.BlockSpec((None, 1, tn), lambda i, n: (j, 0, n)),
            pl.BlockSpec((None, 1, tn), lambda i, n: (j, 0, n + nb)),
        ],
        out_specs=pl.BlockSpec((tm, tn), lambda i, n: (i, n)),
        out_shape=jax.ShapeDtypeStruct((n_rows, d_a), F32),
        scratch_shapes=[pltpu.VMEM((tm, d), BF16)],
        compiler_params=_params(("parallel", "arbitrary")),
        name=f"mix_a1s_{j}",
    )(x, g3, w1, w1, b1, b1)


def _dwconv_block(xp_ref, w_ref, r0, rows, cols, taps, halo):
    lead = halo - (taps - 1)
    assert halo % SUBLANES == 0 and lead >= 0
    acc = None
    for s in range(SUBLANES):
        group = [k for k in range(taps) if (k + lead) % SUBLANES == s]
        if not group:
            continue
        ext = rows + (SUBLANES if s else 0)
        part = None
        for k in group:
            base = r0 + ((k + lead) // SUBLANES) * SUBLANES
            term = xp_ref[pl.ds(base, ext), cols] * w_ref[k:k + 1, cols]
            part = term if part is None else part + term
        if s:
            part = part[s:s + rows]
        acc = part if acc is None else acc + part
    return acc


def _snake(i, k, nb):
    return jnp.where(i % 2 == 0, k, nb - 1 - k)


def _start_tile(x_ref, g_ref, h_ref, hal_ref, first_in_seq):
    _rms_rows(x_ref, g_ref, h_ref)

    @pl.when(first_in_seq)
    def _():
        hal_ref[...] = jnp.zeros(hal_ref.shape, hal_ref.dtype)


def _af_kernel(x_ref, xr_ref, g_ref, wa_ref, wg_ref, ba_ref, bgt_ref, wdw_ref, bdw_ref, lg_ref,
               lb_ref, w2_ref, o_ref, st_ref, h_ref, gbuf_ref, wbuf_ref, hal_ref, y_ref, act_ref,
               *, taps, tiles_per_seq, nb, rc):
    i, n = pl.program_id(0), pl.program_id(1)
    tm = x_ref.shape[0]
    tn = wa_ref.shape[1]
    halo = hal_ref.shape[0]
    cb = 128

    @pl.when(n == 0)
    def _():
        _start_tile(x_ref, g_ref, h_ref, hal_ref, i % tiles_per_seq == 0)

    @pl.when(n < nb)
    def _():
        col0 = pl.multiple_of(_snake(i, n, nb) * tn, tn)
        gbuf_ref[0:halo, :] = hal_ref[:, pl.ds(col0, tn)]
        wbuf_ref[0:taps, :] = wdw_ref[:, pl.ds(col0, tn)]
        wa, wg = wa_ref[...], wg_ref[...]
        ba, bgt = ba_ref[...], bgt_ref[...]
        for r in range(tm // rc):
            hc = h_ref[r * rc:(r + 1) * rc, :]
            glu = (_dot(hc, wa) + ba) * jax.nn.sigmoid(_dot(hc, wg) + bgt)
            gbuf_ref[halo + r * rc:halo + (r + 1) * rc, :] = glu
            for sub in range(rc // cb):
                r0 = r * rc + sub * cb
                for strip in range(tn // LANES):
                    cols = slice(strip * LANES, (strip + 1) * LANES)
                    y = _dwconv_block(gbuf_ref, wbuf_ref, r0, cb, cols, taps, halo)
                    y_ref[r0:r0 + cb, pl.ds(pl.multiple_of(col0 + strip * LANES, LANES), LANES)] = y
        hal_ref[:, pl.ds(col0, tn)] = gbuf_ref[tm:tm + halo, :]

    @pl.when(n == nb)
    def _():
        w2 = w2_ref[...]
        step, cw = 64, 512
        for r in range(tm // rc):
            for s in range(rc // step):
                blk = slice(r * rc + s * step, r * rc + (s + 1) * step)

                def load(c, blk=blk):
                    cols = slice(c * cw, (c + 1) * cw)
                    return y_ref[blk, cols] + bdw_ref[:, cols]

                def store(c, v, blk=blk):
                    act_ref[blk, c * cw:(c + 1) * cw] = _silu(v).astype(BF16)

                _layernorm_sweeps(load, store, y_ref.shape[1], lg_ref, lb_ref, cw)
            rows = slice(r * rc, (r + 1) * rc)
            o_ref[rows, :] = xr_ref[rows, :] + _dot(act_ref[rows, :], w2)

    @pl.when(n > nb)
    def _():
        o_ref[...] = xr_ref[...] + _dot(act_ref[...], w2_ref[...])

    @pl.when((n == 2 * nb - 1) & (i % tiles_per_seq == tiles_per_seq - 1))
    def _():
        st_ref[...] = hal_ref[halo - (taps - 1):halo, :]


def _a_fused(x, g3, w1, b1, w_dw, b_dw, ln_g, ln_b, w2, layer, j, *, n_prompt, seq, tm):
    m, d = x.shape
    taps = w_dw.shape[1]
    nb, tn = w2.shape[0], w2.shape[2]
    halo = 32
    assert w1.shape == (2 * nb, d, tn) and nb * tn == d and taps - 1 <= halo and seq % tm == 0
    kern = functools.partial(_af_kernel, taps=taps, tiles_per_seq=seq // tm, nb=nb, rc=256)
    taps_pad = -(-taps // SUBLANES) * SUBLANES
    blk1 = lambda i, n: _snake(i, jnp.minimum(n, nb - 1), nb)
    blk2 = lambda i, n: _snake(i, jnp.maximum(n - nb, 0), nb)
    row = lambda i, n: (j, 0, 0)
    return pl.pallas_call(
        kern,
        grid=(n_prompt // tm, 2 * nb),
        in_specs=[
            pl.BlockSpec((tm, d), _read_once_rows(n_prompt // tm)),
            pl.BlockSpec((tm, tn), lambda i, n: (i, blk2(i, n))),
            pl.BlockSpec((None, 1, d), lambda i, n: (layer, 0, 0)),
            pl.BlockSpec((None, d, tn), lambda i, n: (blk1(i, n), 0, 0)),
            pl.BlockSpec((None, d, tn), lambda i, n: (blk1(i, n) + nb, 0, 0)),
            pl.BlockSpec((None, 1, tn), lambda i, n: (j, 0, blk1(i, n))),
            pl.BlockSpec((None, 1, tn), lambda i, n: (j, 0, blk1(i, n) + nb)),
            pl.BlockSpec((None, taps, d), row),
            pl.BlockSpec((None, 1, d), row),
            pl.BlockSpec((None, 1, d), row),
            pl.BlockSpec((None, 1, d), row),
            pl.BlockSpec((None, d, tn), lambda i, n: (blk2(i, n), 0, 0)),
        ],
        out_specs=[
            pl.BlockSpec((tm, tn), lambda i, n: (i, blk2(i, n))),
            pl.BlockSpec((None, taps - 1, d), lambda i, n: (i // (seq // tm), 0, 0)),
        ],
        out_shape=[
            jax.ShapeDtypeStruct((m, d), F32),
            jax.ShapeDtypeStruct((n_prompt // seq, taps - 1, d), F32),
        ],
        scratch_shapes=[
            pltpu.VMEM((tm, d), BF16),
            pltpu.VMEM((halo + tm, tn), F32),
            pltpu.VMEM((taps_pad, tn), F32),
            pltpu.VMEM((halo, d), F32),
            pltpu.VMEM((tm, d), F32),
            pltpu.VMEM((tm, d), BF16),
        ],
        compiler_params=_params(("arbitrary", "arbitrary")),
        name=f"mix_a_{j}",
    )(x, x, g3, w1, w1, b1, b1, w_dw, b_dw, ln_g, ln_b, w2)


def _a_sconv_kernel(*refs):
    st_ref, glu_ref, w_ref, b_ref = refs[:4]
    y_ref, ns_ref = refs[-2:]
    past = st_ref.shape[0]
    acc = st_ref[0] * w_ref[0:1, :]
    for k in range(1, past):
        acc = acc + st_ref[k] * w_ref[k:k + 1, :]
        ns_ref[k - 1] = st_ref[k]
    g = glu_ref[...]
    ns_ref[past - 1] = g
    y_ref[...] = acc + g * w_ref[past:past + 1, :] + b_ref[...]


def _a_sample_conv(state_t, glu, w_dw, b_dw, prev, j, *, tn):
    n_layers, past, nb, d = state_t.shape
    taps = w_dw.shape[1]
    assert taps == past + 1 and glu.shape == (nb, d)
    in_specs = [
        pl.BlockSpec((None, past, nb, tn), lambda n: (j, 0, 0, n)),
        pl.BlockSpec((nb, tn), lambda n: (0, n)),
        pl.BlockSpec((None, taps, tn), lambda n: (j, 0, n)),
        pl.BlockSpec((None, 1, tn), lambda n: (j, 0, n)),
    ]
    args = [state_t, glu, w_dw, b_dw]
    aliases = {}
    if prev is not None:
        in_specs.append(pl.BlockSpec(memory_space=pl.ANY))
        args.append(prev)
        aliases = {4: 1}
    return pl.pallas_call(
        _a_sconv_kernel,
        grid=(d // tn,),
        in_specs=in_specs,
        out_specs=[
            pl.BlockSpec((nb, tn), lambda n: (0, n)),
            pl.BlockSpec((None, past, nb, tn), lambda n: (j, 0, 0, n)),
        ],
        out_shape=[
            jax.ShapeDtypeStruct((nb, d), F32),
            jax.ShapeDtypeStruct((n_layers, past, nb, d), F32),
        ],
        input_output_aliases=aliases,
        compiler_params=_params(("parallel",)),
        name=f"mix_a_sconv_{j}",
    )(*args)


def _a2s_kernel(x_ref, y_ref, lg_ref, lb_ref, w2_ref, _, o_ref, act_ref):
    @pl.when(pl.program_id(0) == 0)
    def _():
        act_ref[...] = _silu(_layernorm(y_ref[...], lg_ref[...], lb_ref[...])).astype(BF16)

    o_ref[...] = x_ref[...] + _dot(act_ref[...], w2_ref[...])


def _a2_sample(x, x_new, y, ln_g, ln_b, w2, j, *, n_prompt):
    m, d = x.shape
    ns = y.shape[0]
    tn = w2.shape[2]
    rb = n_prompt // ns
    return pl.pallas_call(
        _a2s_kernel,
        grid=(d // tn,),
        in_specs=[
            pl.BlockSpec((ns, tn), lambda n: (rb, n)),
            pl.BlockSpec((ns, d), lambda n: (0, 0)),
            pl.BlockSpec((None, 1, d), lambda n: (j, 0, 0)),
            pl.BlockSpec((None, 1, d), lambda n: (j, 0, 0)),
            pl.BlockSpec((None, d, tn), lambda n: (n, 0, 0)),
            pl.BlockSpec(memory_space=pl.ANY),
        ],
        out_specs=pl.BlockSpec((ns, tn), lambda n: (rb, n)),
        out_shape=jax.ShapeDtypeStruct((m, d), F32),
        scratch_shapes=[pltpu.VMEM((ns, d), BF16)],
        input_output_aliases={5: 0},
        compiler_params=_params(("arbitrary",)),
        name=f"mix_a2s_{j}",
    )(x, y, ln_g, ln_b, w2, x_new)


def _b1_kernel(x_ref, g_ref, w_ref, b_ref, lg_ref, lb_ref, o_ref, h_ref):
    n = pl.program_id(1)
    per_step, _, tn = w_ref.shape

    @pl.when(n == 0)
    def _():
        _rms_rows(x_ref, g_ref, h_ref)

    h = h_ref[...]
    for t in range(per_step):
        z = _dot(h, w_ref[t]) + b_ref[:, t * tn:(t + 1) * tn]
        o_ref[:, pl.ds(pl.multiple_of((n * per_step + t) * tn, tn), tn)] = _gelu(z)

    @pl.when(n == pl.num_programs(1) - 1)
    def _():
        rows, step, cw = o_ref.shape[0], 64, 512

        def body(c, carry):
            blk = pl.ds(pl.multiple_of(c * step, step), step)

            def load(k):
                return o_ref[blk, k * cw:(k + 1) * cw]

            def store(k, v):
                o_ref[blk, k * cw:(k + 1) * cw] = v

            _layernorm_sweeps(load, store, o_ref.shape[1], lg_ref, lb_ref, cw)
            return carry

        lax.fori_loop(0, rows // step, body, 0)


def _b1(x, g3, w_in, b_in, ln_g, ln_b, layer, j, *, tm):
    m, d = x.shape
    nb, tn = w_in.shape[0] // 2, w_in.shape[2]
    d_sgu = nb * tn
    per_step = 2
    steps = nb // per_step
    assert nb % per_step == 0
    return pl.pallas_call(
        _b1_kernel,
        grid=(m // tm, steps),
        in_specs=[
            pl.BlockSpec((tm, d), lambda i, n: (i, 0)),
            pl.BlockSpec((None, 1, d), lambda i, n: (layer, 0, 0)),
            pl.BlockSpec((per_step, d, tn), lambda i, n: (n + steps, 0, 0)),
            pl.BlockSpec((None, 1, per_step * tn), lambda i, n: (j, 0, n + steps)),
            pl.BlockSpec((None, 1, d_sgu), lambda i, n: (j, 0, 0)),
            pl.BlockSpec((None, 1, d_sgu), lambda i, n: (j, 0, 0)),
        ],
        out_specs=pl.BlockSpec((tm, d_sgu), lambda i, n: (i, 0)),
        out_shape=jax.ShapeDtypeStruct((m, d_sgu), F32),
        scratch_shapes=[pltpu.VMEM((tm, d), BF16)],
        compiler_params=_params(("parallel", "arbitrary")),
        name=f"mix_b1_{j}",
    )(x, g3, w_in, b_in, ln_g, ln_b)


def _b2_kernel(x_ref, g_ref, v_ref, wi_ref, bi_ref, ws_ref, bs_ref, wo_ref, o_ref, h_ref,
               *, n_prompt_chunks):
    i = pl.program_id(0)
    tm = x_ref.shape[0]

    @pl.when(pl.program_id(1) == 0)
    def _():
        _rms_rows(x_ref, g_ref, h_ref, copy_ref=o_ref)

    per_step, _, gw = wi_ref.shape
    h = h_ref[...]
    row = lax.broadcasted_iota(jnp.int32, (CHUNK, CHUNK), 0)
    col = lax.broadcasted_iota(jnp.int32, (CHUNK, CHUNK), 1)
    ys = []
    for t in range(per_step):
        gcols = slice(t * gw, (t + 1) * gw)
        u = _gelu(_dot(h, wi_ref[t]) + bi_ref[:, gcols])
        ws = ws_ref[t]
        w_tril = jnp.where(col <= row, ws, 0.0).astype(BF16)
        bs = bs_ref[t]
        parts = []
        for c in range(tm // CHUNK):
            rows = slice(c * CHUNK, (c + 1) * CHUNK)
            v = v_ref[rows, gcols]
            s_prompt = _dot(w_tril, v.astype(BF16)) + bs
            s_sample = ws[0:1, 0:1] * v + bs[0:1, :]
            is_sample = i * (tm // CHUNK) + c >= n_prompt_chunks
            parts.append(u[rows, :] * jnp.where(is_sample, s_sample, s_prompt))
        ys.append(jnp.concatenate(parts, axis=0))
    y = jnp.concatenate(ys, axis=1).astype(BF16)
    o_ref[...] += _dot(y, wo_ref[...])


def _b2(x, g3, vn, w_in, b_in, w_s, b_s3, w_out, layer, j, *, n_prompt, tm):
    m, d = x.shape
    d_sgu = vn.shape[1]
    gw = d_sgu // N_SGU_GROUPS
    assert w_in.shape == (2 * N_SGU_GROUPS, d, gw) and w_out.shape == (1, d_sgu, d)
    kern = functools.partial(_b2_kernel, n_prompt_chunks=n_prompt // CHUNK)
    per_step = 2
    assert N_SGU_GROUPS % per_step == 0
    pw = per_step * gw
    return pl.pallas_call(
        kern,
        grid=(m // tm, N_SGU_GROUPS // per_step),
        in_specs=[
            pl.BlockSpec((tm, d), lambda i, g: (i, 0)),
            pl.BlockSpec((None, 1, d), lambda i, g: (layer, 0, 0)),
            pl.BlockSpec((tm, pw), lambda i, g: (i, g)),
            pl.BlockSpec((per_step, d, gw), lambda i, g: (g, 0, 0)),
            pl.BlockSpec((None, 1, pw), lambda i, g: (j, 0, g)),
            pl.BlockSpec((None, per_step, CHUNK, CHUNK), lambda i, g: (j, g, 0, 0)),
            pl.BlockSpec((None, per_step, CHUNK, 1), lambda i, g: (j, g, 0, 0)),
            pl.BlockSpec((None, pw, d), lambda i, g: (0, g, 0)),
        ],
        out_specs=pl.BlockSpec((tm, d), lambda i, g: (i, 0)),
        out_shape=jax.ShapeDtypeStruct((m, d), F32),
        scratch_shapes=[pltpu.VMEM((tm, d), BF16)],
        compiler_params=_params(("parallel", "arbitrary")),
        name=f"mix_b2_{j}",
    )(x, g3, vn, w_in, b_in, w_s, b_s3, w_out)


def _c1_kernel(x_ref, g_ref, wb_ref, wc_ref, wx_ref, bg_ref, cx_ref, h_ref):
    @pl.when(pl.program_id(1) == 0)
    def _():
        _rms_rows(x_ref, g_ref, h_ref)

    h = h_ref[...]
    bg_ref[...] = _dot(h, wb_ref[...])
    cx_ref[...] = _dot(h, wc_ref[...]) * _dot(h, wx_ref[...])


def _c1(x, g3, w_in, layer, j, *, row0, n_rows, tm):
    d = x.shape[1]
    nb, tn = w_in.shape[0] // 3, w_in.shape[2]
    d_c = nb * tn
    rb0 = row0 // tm
    out = jax.ShapeDtypeStruct((n_rows, d_c), F32)
    return pl.pallas_call(
        _c1_kernel,
        grid=(n_rows // tm, nb),
        in_specs=[
            pl.BlockSpec((tm, d), lambda i, n: (rb0 + i, 0)),
            pl.BlockSpec((None, 1, d), lambda i, n: (layer, 0, 0)),
            pl.BlockSpec((None, d, tn), lambda i, n: (n, 0, 0)),
            pl.BlockSpec((None, d, tn), lambda i, n: (n + nb, 0, 0)),
            pl.BlockSpec((None, d, tn), lambda i, n: (n + 2 * nb, 0, 0)),
        ],
        out_specs=[pl.BlockSpec((tm, tn), lambda i, n: (i, n))] * 2,
        out_shape=[out, out],
        scratch_shapes=[pltpu.VMEM((tm, d), BF16)],
        compiler_params=_params(("parallel", "arbitrary")),
        name=f"mix_c1s_{j}",
    )(x, g3, w_in, w_in, w_in)


def _cf_kernel(x_ref, xr_ref, g_ref, wb_ref, wc_ref, wx_ref, wcv_ref, wo_ref, o_ref, st_ref,
               h_ref, gbuf_ref, wbuf_ref, hal_ref, act_ref, *, taps, tiles_per_seq, nb, rc):
    i, n = pl.program_id(0), pl.program_id(1)
    tm = x_ref.shape[0]
    tn = wb_ref.shape[1]
    halo = hal_ref.shape[0]
    cb = 128

    @pl.when(n == 0)
    def _():
        _start_tile(x_ref, g_ref, h_ref, hal_ref, i % tiles_per_seq == 0)

    @pl.when(n < nb)
    def _():
        col0 = pl.multiple_of(_snake(i, n, nb) * tn, tn)
        gbuf_ref[0:halo, :] = hal_ref[:, pl.ds(col0, tn)]
        wbuf_ref[0:taps, :] = wcv_ref[:, pl.ds(col0, tn)]
        wb, wc, wx = wb_ref[...], wc_ref[...], wx_ref[...]
        for r in range(tm // rc):
            hc = h_ref[r * rc:(r + 1) * rc, :]
            bg = _dot(hc, wb)
            gbuf_ref[halo + r * rc:halo + (r + 1) * rc, :] = _dot(hc, wc) * _dot(hc, wx)
            for sub in range(rc // cb):
                r0 = r * rc + sub * cb
                for strip in range(tn // LANES):
                    cols = slice(strip * LANES, (strip + 1) * LANES)
                    y = _dwconv_block(gbuf_ref, wbuf_ref, r0, cb, cols, taps, halo)
                    gate = bg[sub * cb:(sub + 1) * cb, cols]
                    act_ref[r0:r0 + cb, pl.ds(pl.multiple_of(col0 + strip * LANES, LANES), LANES)] = (
                        gate * y).astype(BF16)
        hal_ref[:, pl.ds(col0, tn)] = gbuf_ref[tm:tm + halo, :]

    @pl.when(n >= nb)
    def _():
        o_ref[...] = xr_ref[...] + _dot(act_ref[...], wo_ref[...])

    @pl.when((n == 2 * nb - 1) & (i % tiles_per_seq == tiles_per_seq - 1))
    def _():
        st_ref[...] = hal_ref[halo - (taps - 1):halo, :]


def _c_fused(x, g3, w_in, w_conv, w_out, layer, j, *, n_prompt, seq, tm):
    m, d = x.shape
    taps = w_conv.shape[1]
    nb, tn = w_out.shape[0], w_out.shape[2]
    halo = SUBLANES
    assert w_in.shape == (3 * nb, d, tn) and nb * tn == d and taps - 1 <= halo and seq % tm == 0
    kern = functools.partial(_cf_kernel, taps=taps, tiles_per_seq=seq // tm, nb=nb, rc=256)
    taps_pad = -(-taps // SUBLANES) * SUBLANES
    blk1 = lambda i, n: _snake(i, jnp.minimum(n, nb - 1), nb)
    blk2 = lambda i, n: _snake(i, jnp.maximum(n - nb, 0), nb)
    blk = lambda part: (lambda i, n: (blk1(i, n) + part * nb, 0, 0))
    return pl.pallas_call(
        kern,
        grid=(n_prompt // tm, 2 * nb),
        in_specs=[
            pl.BlockSpec((tm, d), _read_once_rows(n_prompt // tm)),
            pl.BlockSpec((tm, tn), lambda i, n: (i, blk2(i, n))),
            pl.BlockSpec((None, 1, d), lambda i, n: (layer, 0, 0)),
            pl.BlockSpec((None, d, tn), blk(0)),
            pl.BlockSpec((None, d, tn), blk(1)),
            pl.BlockSpec((None, d, tn), blk(2)),
            pl.BlockSpec((None, taps, d), lambda i, n: (j, 0, 0)),
            pl.BlockSpec((None, d, tn), lambda i, n: (blk2(i, n), 0, 0)),
        ],
        out_specs=[
            pl.BlockSpec((tm, tn), lambda i, n: (i, blk2(i, n))),
            pl.BlockSpec((None, taps - 1, d), lambda i, n: (i // (seq // tm), 0, 0)),
        ],
        out_shape=[
            jax.ShapeDtypeStruct((m, d), F32),
            jax.ShapeDtypeStruct((n_prompt // seq, taps - 1, d), F32),
        ],
        scratch_shapes=[
            pltpu.VMEM((tm, d), BF16),
            pltpu.VMEM((halo + tm, tn), F32),
            pltpu.VMEM((taps_pad, tn), F32),
            pltpu.VMEM((halo, d), F32),
            pltpu.VMEM((tm, d), BF16),
        ],
        compiler_params=_params(("arbitrary", "arbitrary")),
        name=f"mix_c_{j}",
    )(x, x, g3, w_in, w_in, w_in, w_conv, w_out)


def _c2s_kernel(x_ref, bg_ref, cx_ref, s0_ref, s1_ref, w_ref, wo_ref, _, o_ref, act_ref):
    @pl.when(pl.program_id(0) == 0)
    def _():
        w = w_ref[...]
        y = s0_ref[...] * w[0:1] + s1_ref[...] * w[1:2] + cx_ref[...] * w[2:3]
        act_ref[...] = (bg_ref[...] * y).astype(BF16)

    o_ref[...] = x_ref[...] + _dot(act_ref[...], wo_ref[...])


def _c2_sample(x, x_new, bg, cx, s0, s1, w_conv, w_out, j, *, n_prompt):
    m, d = x.shape
    ns = s0.shape[0]
    tn = w_out.shape[2]
    rb = n_prompt // ns
    taps = w_conv.shape[1]
    assert taps == 3
    return pl.pallas_call(
        _c2s_kernel,
        grid=(d // tn,),
        in_specs=[
            pl.BlockSpec((ns, tn), lambda n: (rb, n)),
            pl.BlockSpec((ns, d), lambda n: (0, 0)),
            pl.BlockSpec((ns, d), lambda n: (0, 0)),
            pl.BlockSpec((ns, d), lambda n: (0, 0)),
            pl.BlockSpec((ns, d), lambda n: (0, 0)),
            pl.BlockSpec((None, taps, d), lambda n: (j, 0, 0)),
            pl.BlockSpec((None, d, tn), lambda n: (n, 0, 0)),
            pl.BlockSpec(memory_space=pl.ANY),
        ],
        out_specs=pl.BlockSpec((ns, tn), lambda n: (rb, n)),
        out_shape=jax.ShapeDtypeStruct((m, d), F32),
        scratch_shapes=[pltpu.VMEM((ns, d), BF16)],
        input_output_aliases={7: 0},
        compiler_params=_params(("arbitrary",)),
        name=f"mix_c2s_{j}",
    )(x, bg, cx, s0, s1, w_conv, w_out, x_new)


def _last_rows(a, batch, seq, rows):
    return jnp.stack([a[(b + 1) * seq - rows:(b + 1) * seq] for b in range(batch)])


def kernel(x_prompt, x_sample, state_conv_a, state_conv_c, g_ffn1, g_mix, g_ffn2, g_final,
           w_ffn_gate, w_ffn_up, w_ffn_down,
           a_w_pw1, a_b_pw1, a_w_dw, a_b_dw, a_ln_g, a_ln_b, a_w_pw2,
           b_w_in, b_b_in, b_ln_g, b_ln_b, b_w_s, b_b_s, b_w_out,
           c_w_in, c_w_conv, c_w_out):
    batch, seq, d = x_prompt.shape
    n_sample = x_sample.shape[0]
    assert x_sample.shape[1] == 1
    n_prompt = batch * seq
    depth = g_ffn1.shape[0]
    past_c = state_conv_c.shape[2]
    assert past_c == 2
    d_sgu = b_ln_g.shape[-1]

    row3 = lambda a: a.reshape(a.shape[0], 1, a.shape[1])
    g1, gm, g2 = row3(g_ffn1), row3(g_mix), row3(g_ffn2)
    gf = g_final.reshape(1, d)
    a_b1, a_bd, a_lg, a_lb = row3(a_b_pw1), row3(a_b_dw), row3(a_ln_g), row3(a_ln_b)
    b_bi, b_lg, b_lb = row3(b_b_in), row3(b_ln_g), row3(b_ln_b)
    b_bs = b_b_s.reshape(*b_b_s.shape, 1)
    state_a_t = jnp.transpose(state_conv_a, (0, 2, 1, 3))

    ffn = functools.partial(_ffn, n_prompt=n_prompt, n_sample=n_sample,
                            tm=832, tf_head=512, tf_tail=512)
    new_a_p, new_b_p, new_b_s, new_c_p, new_c_s = [], [], [], [], []
    new_a_s_t = None

    x = x_prompt.reshape(n_prompt, d)
    xs = x_sample.reshape(n_sample, d)
    gw = d_sgu // N_SGU_GROUPS
    for i in range(depth):
        kind, j = i % 3, i // 3
        if kind == 0:
            casts = [(a_w_pw1, j, 32, 512), (a_w_pw2, j, 32, 512)]
        elif kind == 1:
            casts = [(b_w_in, j, 32, gw), (b_w_out, j, 64, d)]
        else:
            casts = [(c_w_in, j, 32, 512), (c_w_out, j, 32, 512)]
        x, (w_a16, w_b16) = ffn(x, xs, g1, gf, w_ffn_gate, w_ffn_up, w_ffn_down, i, 0, casts,
                                first=(i == 0), last=False)
        if kind == 0:
            glu_s = _a1(x, gm, w_a16, a_b1, i, j, row0=n_prompt, n_rows=n_sample, tm=n_sample)
            y_s, new_a_s_t = _a_sample_conv(state_a_t, glu_s, a_w_dw, a_bd, new_a_s_t, j, tn=256)
            x_new, st_p = _a_fused(x, gm, w_a16, a_b1, a_w_dw, a_bd, a_lg, a_lb, w_b16, i, j,
                                   n_prompt=n_prompt, seq=seq, tm=1024)
            x = _a2_sample(x, x_new, y_s, a_lg, a_lb, w_b16, j, n_prompt=n_prompt)
            new_a_p.append(st_p)
        elif kind == 1:
            vn = _b1(x, gm, w_a16, b_bi, b_lg, b_lb, i, j, tm=832)
            x = _b2(x, gm, vn, w_a16, b_bi, b_w_s, b_bs, w_b16, i, j, n_prompt=n_prompt, tm=640)
            new_b_p.append(_last_rows(vn, batch, seq, CHUNK))
            new_b_s.append(vn[n_prompt:].reshape(n_sample, 1, d_sgu))
        else:
            bg_s, cx_s = _c1(x, gm, w_a16, i, j, row0=n_prompt, n_rows=n_sample, tm=n_sample)
            s0, s1 = state_conv_c[j, :, 0], state_conv_c[j, :, 1]
            x_new, st_p = _c_fused(x, gm, w_a16, c_w_conv, w_b16, i, j,
                                   n_prompt=n_prompt, seq=seq, tm=1024)
            x = _c2_sample(x, x_new, bg_s, cx_s, s0, s1, c_w_conv, w_b16, j, n_prompt=n_prompt)
            new_c_p.append(st_p)
            new_c_s.append(jnp.stack([s1, cx_s], axis=1))
        last = i == depth - 1
        x, _ = ffn(x, xs, g2, gf, w_ffn_gate, w_ffn_up, w_ffn_down, i, 1, [],
                   first=False, last=last)

    return (x[:n_prompt].reshape(batch, seq, d), x[n_prompt:].reshape(n_sample, 1, d),
            jnp.stack(new_a_p), jnp.transpose(new_a_s_t, (0, 2, 1, 3)),
            jnp.stack(new_b_p), jnp.stack(new_b_s),
            jnp.stack(new_c_p), jnp.stack(new_c_s))
```
